```python
import jax, jax.numpy as jnp
from jax import lax
import numpy as np

D_MODEL = 1024
BATCH = 8
SEQ = 4096
DEPTH = 2

D_PLE = 256
D_CONF = 512
CONF_KERNEL = 31
D_SC = 512
SC_KERNEL = 3
D_FF = 2816
N_EXPERTS = 8
TOP_K = 2
D_EXPERT = 3584
MOE_BLOCK = 256
LN_EPS = 1e-5
DEEPNORM_ALPHA = (2 * DEPTH) ** 0.25
DEEPNORM_BETA = (8 * DEPTH) ** -0.25
N_DENSE = (DEPTH + 1) // 2
N_MOE = DEPTH // 2
IN_COLS = 2 * D_CONF + 3 * D_SC + 2 * D_MODEL
IN_SPLITS = (D_CONF, 2 * D_CONF, 2 * D_CONF + D_SC, 2 * D_CONF + 2 * D_SC,
             2 * D_CONF + 3 * D_SC, 2 * D_CONF + 3 * D_SC + D_MODEL)

kernel_name = "hybrid_conformer_shortconv_gated_moe_deepnorm"


def layer_norm(x, g, b):
    xf = x.astype(jnp.float32)
    mu = jnp.mean(xf, axis=-1, keepdims=True)
    var = jnp.mean(jnp.square(xf - mu), axis=-1, keepdims=True)
    return ((xf - mu) * lax.rsqrt(var + LN_EPS)).astype(x.dtype) * g + b


def causal_depthwise_conv(u, w):
    k, c = w.shape
    return lax.conv_general_dilated(
        u, w[:, None, :].astype(u.dtype), window_strides=(1,), padding=[(k - 1, 0)],
        dimension_numbers=("NWC", "WIO", "NWC"), feature_group_count=c)


def token_mixer(x, w_in, b_in, conf_conv_w, conf_conv_b, conf_ln_g, conf_ln_b,
                w_conf_out, sc_conv_w, w_sc_out, w_o):
    z = jnp.einsum("bsd,dc->bsc", x, w_in) + b_in
    conf_val, conf_gate, sc_b, sc_c, sc_h, gate_a, gate_b = jnp.split(z, IN_SPLITS, axis=-1)
    a = conf_val * jax.nn.sigmoid(conf_gate)
    a = causal_depthwise_conv(a, conf_conv_w) + conf_conv_b
    a = jax.nn.silu(layer_norm(a, conf_ln_g, conf_ln_b))
    y_a = jnp.einsum("bsc,cd->bsd", a, w_conf_out)
    s = causal_depthwise_conv(sc_c * sc_h, sc_conv_w)
    y_b = jnp.einsum("bsc,cd->bsd", sc_b * s, w_sc_out)
    m = jax.nn.sigmoid(gate_a) * y_a + jax.nn.sigmoid(gate_b) * y_b
    return jnp.einsum("bsd,de->bse", m, w_o)


def dense_swiglu(x, w_gate, w_up, w_down):
    h = jax.nn.silu(jnp.einsum("bsd,df->bsf", x, w_gate)) * jnp.einsum("bsd,df->bsf", x, w_up)
    return jnp.einsum("bsf,fd->bsd", h, w_down)


def moe_swiglu(x, w_router, b_router, we_gate, we_up, we_down):
    bsz, seq, d = x.shape
    n_tok = bsz * seq
    xt = x.reshape(n_tok, d)
    logits = jnp.einsum("td,de->te", xt, w_router).astype(jnp.float32) + b_router.astype(jnp.float32)
    top_vals, top_idx = lax.top_k(logits, TOP_K)
    top_w = jax.nn.softmax(top_vals, axis=-1)
    n_assign = n_tok * TOP_K
    e_flat = top_idx.reshape(-1).astype(jnp.int32)
    tok_flat = jnp.repeat(jnp.arange(n_tok, dtype=jnp.int32), TOP_K)
    w_flat = top_w.reshape(-1)
    order = jnp.argsort(e_flat)
    e_sorted = e_flat[order]
    counts = jnp.zeros((N_EXPERTS,), jnp.int32).at[e_flat].add(1)
    padded = (counts + MOE_BLOCK - 1) // MOE_BLOCK * MOE_BLOCK
    pad_end = jnp.cumsum(padded)
    pad_start = pad_end - padded
    start = jnp.cumsum(counts) - counts
    rank = jnp.arange(n_assign, dtype=jnp.int32) - start[e_sorted]
    dest = pad_start[e_sorted] + rank
    n_rows = (-(-n_assign // MOE_BLOCK)) * MOE_BLOCK + N_EXPERTS * MOE_BLOCK
    n_blocks = n_rows // MOE_BLOCK
    row_tok = jnp.zeros((n_rows,), jnp.int32).at[dest].set(tok_flat[order])
    row_w = jnp.zeros((n_rows,), jnp.float32).at[dest].set(w_flat[order])
    block_e = jnp.minimum(
        jnp.searchsorted(pad_end, jnp.arange(n_blocks, dtype=jnp.int32) * MOE_BLOCK, side="right"),
        N_EXPERTS - 1).astype(jnp.int32)
    xb = xt[row_tok].reshape(n_blocks, MOE_BLOCK, d)

    def expert_block(args):
        xblk, e = args
        h = jax.nn.silu(xblk @ we_gate[e]) * (xblk @ we_up[e])
        return h @ we_down[e]

    yb = lax.map(expert_block, (xb, block_e))
    y_rows = yb.reshape(n_rows, d) * row_w[:, None].astype(x.dtype)
    out = jnp.zeros((n_tok, d), x.dtype).at[row_tok].add(y_rows)
    return out.reshape(bsz, seq, d)


def setup_inputs(seed: int = 0) -> dict:
    key = jax.random.key(seed)
    ks = iter(jax.random.split(key, 32))

    def nrm(shape, scale):
        return jax.random.normal(next(ks), shape, jnp.float32) * scale

    L = DEPTH
    return {
        "x": nrm((BATCH, SEQ, D_MODEL), 1.0),
        "p": nrm((DEPTH, BATCH, SEQ, D_PLE), 1.0),
        "w_in": nrm((L, D_MODEL, IN_COLS), D_MODEL ** -0.5),
        "b_in": nrm((L, IN_COLS), 0.02),
        "conf_conv_w": nrm((L, CONF_KERNEL, D_CONF), CONF_KERNEL ** -0.5),
        "conf_conv_b": nrm((L, D_CONF), 0.02),
        "conf_ln_g": 1.0 + nrm((L, D_CONF), 0.02),
        "conf_ln_b": nrm((L, D_CONF), 0.02),
        "w_conf_out": nrm((L, D_CONF, D_MODEL), D_CONF ** -0.5),
        "sc_conv_w": nrm((L, SC_KERNEL, D_SC), SC_KERNEL ** -0.5),
        "w_sc_out": nrm((L, D_SC, D_MODEL), D_SC ** -0.5),
        "w_o": nrm((L, D_MODEL, D_MODEL), D_MODEL ** -0.5 * DEEPNORM_BETA),
        "ln1_g": 1.0 + nrm((L, D_MODEL), 0.02),
        "ln1_b": nrm((L, D_MODEL), 0.02),
        "w_ff_gate": nrm((N_DENSE, D_MODEL, D_FF), D_MODEL ** -0.5),
        "w_ff_up": nrm((N_DENSE, D_MODEL, D_FF), D_MODEL ** -0.5),
        "w_ff_down": nrm((N_DENSE, D_FF, D_MODEL), D_FF ** -0.5 * DEEPNORM_BETA),
        "w_router": nrm((N_MOE, D_MODEL, N_EXPERTS), D_MODEL ** -0.5),
        "b_router": nrm((N_MOE, N_EXPERTS), 0.01),
        "we_gate": nrm((N_MOE, N_EXPERTS, D_MODEL, D_EXPERT), D_MODEL ** -0.5),
        "we_up": nrm((N_MOE, N_EXPERTS, D_MODEL, D_EXPERT), D_MODEL ** -0.5),
        "we_down": nrm((N_MOE, N_EXPERTS, D_EXPERT, D_MODEL), D_EXPERT ** -0.5 * DEEPNORM_BETA),
        "w_ple_gate": nrm((L, D_MODEL, D_MODEL), D_MODEL ** -0.5),
        "b_ple_gate": nrm((L, D_MODEL), 0.02),
        "w_ple_proj": nrm((L, D_PLE, D_MODEL), D_PLE ** -0.5 * DEEPNORM_BETA),
        "ln2_g": 1.0 + nrm((L, D_MODEL), 0.02),
        "ln2_b": nrm((L, D_MODEL), 0.02),
    }


def reference(x, p, w_in, b_in, conf_conv_w, conf_conv_b, conf_ln_g, conf_ln_b,
              w_conf_out, sc_conv_w, w_sc_out, w_o, ln1_g, ln1_b,
              w_ff_gate, w_ff_up, w_ff_down, w_router, b_router, we_gate, we_up, we_down,
              w_ple_gate, b_ple_gate, w_ple_proj, ln2_g, ln2_b):
    for i in range(DEPTH):
        mix = token_mixer(x, w_in[i], b_in[i], conf_conv_w[i], conf_conv_b[i], conf_ln_g[i],
                          conf_ln_b[i], w_conf_out[i], sc_conv_w[i], w_sc_out[i], w_o[i])
        x = layer_norm(DEEPNORM_ALPHA * x + mix, ln1_g[i], ln1_b[i])
        j = i // 2
        if i % 2 == 0:
            ffn = dense_swiglu(x, w_ff_gate[j], w_ff_up[j], w_ff_down[j])
        else:
            ffn = moe_swiglu(x, w_router[j], b_router[j], we_gate[j], we_up[j], we_down[j])
        ple = jax.nn.sigmoid(jnp.einsum("bsd,de->bse", x, w_ple_gate[i]) + b_ple_gate[i]) \
            * jnp.einsum("bsq,qd->bsd", p[i], w_ple_proj[i])
        x = layer_norm(DEEPNORM_ALPHA * x + ffn + ple, ln2_g[i], ln2_b[i])
    return x
```

```python
import functools

import jax
import jax.numpy as jnp
from jax import lax
from jax.experimental import pallas as pl
from jax.experimental.pallas import tpu as pltpu

LN_EPS = 1e-5
TOP_K = 2
LANES = 128
SUBLANES = 8
CONV_HALO = 32
SC_HALO = 8
CONV_ROWS = 64
NEG_BIG = -1e30
VMEM_LIMIT = 56 * 1024 * 1024

F32 = jnp.float32
BF16 = jnp.bfloat16


def _dot(a, b):
    return jnp.dot(a, b, preferred_element_type=F32)


def _sigmoid(x):
    return 1.0 / (1.0 + jnp.exp(-x))


def _layer_norm(x, g, b):
    mu = jnp.mean(x, axis=-1, keepdims=True)
    xc = x - mu
    var = jnp.mean(xc * xc, axis=-1, keepdims=True)
    return xc * lax.rsqrt(var + LN_EPS) * g + b


def _const_spec(shape):
    nd = len(shape)
    return pl.BlockSpec(shape, lambda *_: (0,) * nd, pipeline_mode=pl.Buffered(1))


def _causal_dwconv_chunk(src_ref, row0, halo, w_ref, taps, lanes, init):
    acc = init
    first = halo - (taps - 1)
    for r in range(SUBLANES):
        offs = [o for o in range(first, halo + 1) if o % SUBLANES == r]
        if not offs:
            continue
        base = offs[0]
        span = offs[-1] - base + CONV_ROWS
        blk = src_ref[row0 + base:row0 + base + span, lanes]
        for o in offs:
            j = o - first
            acc = acc + w_ref[j:j + 1, lanes] * blk[o - base:o - base + CONV_ROWS, :]
    return acc


def _mixer_kernel(x_ref, w_in_ref, b_in_ref, ccw_ref, ccb_ref, clg_ref, clb_ref, wa_ref,
                  scw_ref, wb_ref, wo_ref, g1_ref, b1_ref, o_ref,
                  a_ext, u_ext, scb_buf, act_a, act_b, *, alpha, ts, dc, ds, kc, ks):
    s = pl.program_id(1)

    @pl.when(s == 0)
    def _():
        a_ext[0:CONV_HALO, :] = jnp.zeros((CONV_HALO, dc), F32)
        u_ext[0:SC_HALO, :] = jnp.zeros((SC_HALO, ds), F32)

    @pl.when(s > 0)
    def _():
        a_ext[0:CONV_HALO, :] = a_ext[ts:ts + CONV_HALO, :]
        u_ext[0:SC_HALO, :] = u_ext[ts:ts + SC_HALO, :]

    x = x_ref[0]
    xb = x.astype(BF16)

    c0 = 2 * dc
    c1 = c0 + 3 * ds
    zc = _dot(xb, w_in_ref[:, 0:c0]) + b_in_ref[:, 0:c0]
    a_ext[CONV_HALO:CONV_HALO + ts, :] = zc[:, :dc] * _sigmoid(zc[:, dc:])
    zs = _dot(xb, w_in_ref[:, c0:c1]) + b_in_ref[:, c0:c1]
    scb_buf[...] = zs[:, :ds]
    u_ext[SC_HALO:SC_HALO + ts, :] = zs[:, ds:2 * ds] * zs[:, 2 * ds:]

    for ci in range(ts // CONV_ROWS):
        row0 = ci * CONV_ROWS
        conv = []
        for cg in range(dc // LANES):
            lanes = slice(cg * LANES, (cg + 1) * LANES)
            init = jnp.broadcast_to(ccb_ref[:, lanes], (CONV_ROWS, LANES))
            conv.append(_causal_dwconv_chunk(a_ext, row0, CONV_HALO, ccw_ref, kc, lanes, init))
        conv = jnp.concatenate(conv, axis=-1)
        ln = _layer_norm(conv, clg_ref[...], clb_ref[...])
        act_a[row0:row0 + CONV_ROWS, :] = (ln * _sigmoid(ln)).astype(BF16)
        sconv = []
        for cg in range(ds // LANES):
            lanes = slice(cg * LANES, (cg + 1) * LANES)
            init = jnp.zeros((CONV_ROWS, LANES), F32)
            sconv.append(_causal_dwconv_chunk(u_ext, row0, SC_HALO, scw_ref, ks, lanes, init))
        sconv = jnp.concatenate(sconv, axis=-1)
        act_b[row0:row0 + CONV_ROWS, :] = (scb_buf[row0:row0 + CONV_ROWS, :] * sconv).astype(BF16)

    y_a = _dot(act_a[...], wa_ref[...])
    y_b = _dot(act_b[...], wb_ref[...])
    d = y_a.shape[-1]
    zg = _dot(xb, w_in_ref[:, c1:c1 + 2 * d]) + b_in_ref[:, c1:c1 + 2 * d]
    m = _sigmoid(zg[:, :d]) * y_a + _sigmoid(zg[:, d:]) * y_b
    mix = _dot(m.astype(BF16), wo_ref[...])
    o_ref[0] = _layer_norm(alpha * x + mix, g1_ref[...], b1_ref[...])


def _mixer_call(x, w_in, b_in, ccw, ccb, clg, clb, wa, scw, wb, wo, g1, b1, *, alpha, ts):
    bsz, seq, d = x.shape
    kc, dc = ccw.shape
    ks, ds = scw.shape
    assert seq % ts == 0 and ts % CONV_ROWS == 0 and ts >= CONV_HALO
    assert kc - 1 <= CONV_HALO and ks - 1 <= SC_HALO
    assert dc % LANES == 0 and ds % LANES == 0
    kern = functools.partial(_mixer_kernel, alpha=alpha, ts=ts, dc=dc, ds=ds, kc=kc, ks=ks)
    consts = (w_in, b_in, ccw, ccb, clg, clb, wa, scw, wb, wo, g1, b1)
    return pl.pallas_call(
        kern,
        grid=(bsz, seq // ts),
        in_specs=[pl.BlockSpec((1, ts, d), lambda b, s: (b, s, 0))]
        + [_const_spec(c.shape) for c in consts],
        out_specs=pl.BlockSpec((1, ts, d), lambda b, s: (b, s, 0)),
        out_shape=jax.ShapeDtypeStruct((bsz, seq, d), F32),
        scratch_shapes=[
            pltpu.VMEM((ts + CONV_HALO, dc), F32),
            pltpu.VMEM((ts + SC_HALO, ds), F32),
            pltpu.VMEM((ts, ds), F32),
            pltpu.VMEM((ts, dc), BF16),
            pltpu.VMEM((ts, ds), BF16),
        ],
        compiler_params=pltpu.CompilerParams(
            dimension_semantics=("arbitrary", "arbitrary"), vmem_limit_bytes=VMEM_LIMIT),
        name="mixer",
    )(x, *consts)


def _ple(xb, p, wpg_ref, bpg_ref, wpp_ref):
    gate = _sigmoid(_dot(xb, wpg_ref[...]) + bpg_ref[...])
    return gate * _dot(p.astype(BF16), wpp_ref[...])


def _dense_ffn_kernel(x_ref, p_ref, wg_ref, wu_ref, wd_ref, wpg_ref, bpg_ref, wpp_ref,
                      g2_ref, b2_ref, o_ref, *, alpha, fc):
    x = x_ref[...]
    xb = x.astype(BF16)
    f = wg_ref.shape[1]
    acc = alpha * x + _ple(xb, p_ref[...], wpg_ref, bpg_ref, wpp_ref)
    for c in range(f // fc):
        cols = slice(c * fc, (c + 1) * fc)
        g = _dot(xb, wg_ref[:, cols])
        u = _dot(xb, wu_ref[:, cols])
        h = (g * _sigmoid(g) * u).astype(BF16)
        acc = acc + _dot(h, wd_ref[cols, :])
    o_ref[...] = _layer_norm(acc, g2_ref[...], b2_ref[...])


def _dense_ffn_call(x, p, wg, wu, wd, wpg, bpg, wpp, g2, b2, *, alpha, tm, fc):
    t, d = x.shape
    f = wg.shape[1]
    assert t % tm == 0 and f % fc == 0 and fc % LANES == 0
    consts = (wg, wu, wd, wpg, bpg, wpp, g2, b2)
    return pl.pallas_call(
        functools.partial(_dense_ffn_kernel, alpha=alpha, fc=fc),
        grid=(t // tm,),
        in_specs=[pl.BlockSpec((tm, d), lambda i: (i, 0)),
                  pl.BlockSpec((tm, p.shape[1]), lambda i: (i, 0))]
        + [_const_spec(c.shape) for c in consts],
        out_specs=pl.BlockSpec((tm, d), lambda i: (i, 0)),
        out_shape=jax.ShapeDtypeStruct((t, d), F32),
        compiler_params=pltpu.CompilerParams(
            dimension_semantics=("arbitrary",), vmem_limit_bytes=VMEM_LIMIT),
        name="dense_ffn",
    )(x, p, *consts)


def _router_kernel(x_ref, wr_ref, br_ref, info_ref, cnt_ref, carry, *, tm):
    i = pl.program_id(0)

    @pl.when(i == 0)
    def _():
        carry[...] = jnp.zeros_like(carry)

    logits = _dot(x_ref[...].astype(BF16), wr_ref[...]) + br_ref[...]
    lane = lax.broadcasted_iota(jnp.int32, logits.shape, 1)
    m1 = jnp.max(logits, axis=-1, keepdims=True)
    i1 = jnp.min(jnp.where(logits == m1, lane, LANES), axis=-1, keepdims=True)
    rest = jnp.where(lane == i1, 2.0 * NEG_BIG, logits)
    m2 = jnp.max(rest, axis=-1, keepdims=True)
    i2 = jnp.min(jnp.where(rest == m2, lane, LANES), axis=-1, keepdims=True)
    e21 = jnp.exp(m2 - m1)
    w1 = 1.0 / (1.0 + e21)
    w2 = e21 / (1.0 + e21)

    hit1 = lane == i1
    hit2 = lane == i2
    onehot = jnp.where(hit1 | hit2, 1.0, 0.0)
    row = lax.broadcasted_iota(jnp.int32, (tm, tm), 0)
    col = lax.broadcasted_iota(jnp.int32, (tm, tm), 1)
    lower = jnp.where(col < row, 1.0, 0.0).astype(BF16)
    ranks = _dot(lower, onehot.astype(BF16)) + carry[...]
    r1 = jnp.sum(jnp.where(hit1, ranks, 0.0), axis=-1, keepdims=True)
    r2 = jnp.sum(jnp.where(hit2, ranks, 0.0), axis=-1, keepdims=True)
    carry[...] = carry[...] + jnp.sum(onehot, axis=0, keepdims=True)

    cols = (i1.astype(F32), i2.astype(F32), r1, r2, w1, w2)
    info = jnp.zeros(logits.shape, F32)
    for k, v in enumerate(cols):
        info = jnp.where(lane == k, v, info)
    info_ref[...] = info
    cnt_ref[...] = jnp.broadcast_to(carry[...], cnt_ref.shape)


def _router_call(x, wr, br, *, tm):
    t, d = x.shape
    assert t % tm == 0 and t < 2 ** 24
    return pl.pallas_call(
        functools.partial(_router_kernel, tm=tm),
        grid=(t // tm,),
        in_specs=[pl.BlockSpec((tm, d), lambda i: (i, 0)),
                  _const_spec(wr.shape), _const_spec(br.shape)],
        out_specs=[pl.BlockSpec((tm, LANES), lambda i: (i, 0)),
                   pl.BlockSpec((SUBLANES, LANES), lambda i: (0, 0))],
        out_shape=[jax.ShapeDtypeStruct((t, LANES), F32),
                   jax.ShapeDtypeStruct((SUBLANES, LANES), F32)],
        scratch_shapes=[pltpu.VMEM((1, LANES), F32)],
        compiler_params=pltpu.CompilerParams(
            dimension_semantics=("arbitrary",), vmem_limit_bytes=VMEM_LIMIT),
        name="router",
    )(x, wr, br)


def _dispatch_kernel(pos_ref, x_ref, xs_in_ref, xs_ref, sem, *, tm):
    del xs_in_ref

    def row_copy(r, k):
        return pltpu.make_async_copy(
            x_ref.at[pl.ds(r, 1)], xs_ref.at[pl.ds(pos_ref[TOP_K * r + k], 1)], sem)

    def issue(r, c):
        for k in range(TOP_K):
            row_copy(r, k).start()
        return c

    lax.fori_loop(0, tm, issue, 0)
    for _ in range(TOP_K):
        pltpu.make_async_copy(x_ref, xs_ref.at[pl.ds(0, tm)], sem).wait()


def _dispatch_call(pos_flat, x, xs_init, *, tm):
    t, d = x.shape
    return pl.pallas_call(
        functools.partial(_dispatch_kernel, tm=tm),
        grid=(t // tm,),
        in_specs=[pl.BlockSpec((TOP_K * tm,), lambda i: (i,), memory_space=pltpu.SMEM),
                  pl.BlockSpec((tm, d), lambda i: (i, 0)),
                  pl.BlockSpec(memory_space=pl.ANY)],
        out_specs=pl.BlockSpec(memory_space=pl.ANY),
        out_shape=jax.ShapeDtypeStruct(xs_init.shape, xs_init.dtype),
        scratch_shapes=[pltpu.SemaphoreType.DMA(())],
        input_output_aliases={2: 0},
        compiler_params=pltpu.CompilerParams(
            dimension_semantics=("arbitrary",), vmem_limit_bytes=VMEM_LIMIT),
        name="dispatch",
    )(pos_flat, x, xs_init)


def _expert_kernel(tile_e_ref, tile_n_ref, xs_ref, wg_ref, wu_ref, wd_ref, ys_ref, *, fc):
    del tile_e_ref
    i = pl.program_id(0)

    @pl.when(tile_n_ref[i] > 0)
    def _():
        xb = xs_ref[...].astype(BF16)
        f = wg_ref.shape[2]
        acc = jnp.zeros(ys_ref.shape, F32)
        for c in range(f // fc):
            cols = slice(c * fc, (c + 1) * fc)
            g = _dot(xb, wg_ref[0, :, cols])
            u = _dot(xb, wu_ref[0, :, cols])
            h = (g * _sigmoid(g) * u).astype(BF16)
            acc = acc + _dot(h, wd_ref[0, cols, :])
        ys_ref[...] = acc

    @pl.when(tile_n_ref[i] == 0)
    def _():
        ys_ref[...] = jnp.zeros(ys_ref.shape, F32)


def _expert_call(tile_e, tile_n, xs, wg, wu, wd, *, tme, fc):
    n_rows, d = xs.shape
    f = wg.shape[2]
    assert n_rows % tme == 0 and f % fc == 0 and fc % LANES == 0

    def w_spec(shape):
        return pl.BlockSpec((1,) + shape[1:], lambda i, te, tn: (te[i], 0, 0),
                            pipeline_mode=pl.Buffered(1))

    return pl.pallas_call(
        functools.partial(_expert_kernel, fc=fc),
        grid_spec=pltpu.PrefetchScalarGridSpec(
            num_scalar_prefetch=2,
            grid=(n_rows // tme,),
            in_specs=[pl.BlockSpec((tme, d), lambda i, te, tn: (i, 0)),
                      w_spec(wg.shape), w_spec(wu.shape), w_spec(wd.shape)],
            out_specs=pl.BlockSpec((tme, d), lambda i, te, tn: (i, 0)),
        ),
        out_shape=jax.ShapeDtypeStruct((n_rows, d), F32),
        compiler_params=pltpu.CompilerParams(
            dimension_semantics=("arbitrary",), vmem_limit_bytes=VMEM_LIMIT),
        name="experts",
    )(tile_e, tile_n, xs, wg, wu, wd)


def _combine_kernel(pos_ref, x_ref, p_ref, info_ref, ys_ref, wpg_ref, bpg_ref, wpp_ref,
                    g2_ref, b2_ref, o_ref, ybuf, sem, *, alpha, tm):
    def row_copy(r, k):
        return pltpu.make_async_copy(
            ys_ref.at[pl.ds(pos_ref[TOP_K * r + k], 1)], ybuf.at[k, pl.ds(r, 1)], sem)

    def issue(r, c):
        for k in range(TOP_K):
            row_copy(r, k).start()
        return c

    lax.fori_loop(0, tm, issue, 0)

    x = x_ref[...]
    acc = alpha * x + _ple(x.astype(BF16), p_ref[...], wpg_ref, bpg_ref, wpp_ref)

    for k in range(TOP_K):
        pltpu.make_async_copy(ys_ref.at[pl.ds(0, tm)], ybuf.at[k], sem).wait()
    info = info_ref[...]
    for k in range(TOP_K):
        acc = acc + info[:, 4 + k:5 + k] * ybuf[k]
    o_ref[...] = _layer_norm(acc, g2_ref[...], b2_ref[...])


def _combine_call(pos_flat, x, p, info, ys, wpg, bpg, wpp, g2, b2, *, alpha, tm):
    t, d = x.shape
    consts = (wpg, bpg, wpp, g2, b2)
    return pl.pallas_call(
        functools.partial(_combine_kernel, alpha=alpha, tm=tm),
        grid=(t // tm,),
        in_specs=[pl.BlockSpec((TOP_K * tm,), lambda i: (i,), memory_space=pltpu.SMEM),
                  pl.BlockSpec((tm, d), lambda i: (i, 0)),
                  pl.BlockSpec((tm, p.shape[1]), lambda i: (i, 0)),
                  pl.BlockSpec((tm, LANES), lambda i: (i, 0)),
                  pl.BlockSpec(memory_space=pl.ANY)]
        + [_const_spec(c.shape) for c in consts],
        out_specs=pl.BlockSpec((tm, d), lambda i: (i, 0)),
        out_shape=jax.ShapeDtypeStruct((t, d), F32),
        scratch_shapes=[pltpu.VMEM((TOP_K, tm, d), F32), pltpu.SemaphoreType.DMA(())],
        compiler_params=pltpu.CompilerParams(
            dimension_semantics=("arbitrary",), vmem_limit_bytes=VMEM_LIMIT),
        name="combine",
    )(pos_flat, x, p, info, ys, *consts)


def _moe_layer(x, p, w_router, b_router, we_gate, we_up, we_down, wpg, bpg, wpp, g2, b2,
               *, alpha, tm, tme, fc):
    t, d = x.shape
    n_exp = w_router.shape[1]
    assert n_exp <= LANES
    wr = jnp.zeros((d, LANES), BF16).at[:, :n_exp].set(w_router.astype(BF16))
    br = jnp.full((1, LANES), NEG_BIG, F32).at[0, :n_exp].set(b_router)
    info, cnt = _router_call(x, wr, br, tm=tm)

    counts = cnt[0, :n_exp].astype(jnp.int32)
    padded = (counts + tme - 1) // tme * tme
    pad_end = jnp.cumsum(padded)
    pad_start = pad_end - padded
    experts = info[:, 0:TOP_K].astype(jnp.int32)
    ranks = info[:, TOP_K:2 * TOP_K].astype(jnp.int32)
    pos_flat = (pad_start[experts] + ranks).reshape(-1)
    n_tiles = (t * TOP_K) // tme + n_exp
    tile_start = jnp.arange(n_tiles, dtype=jnp.int32) * tme
    tile_e = jnp.minimum(jnp.searchsorted(pad_end, tile_start, side="right"),
                         n_exp - 1).astype(jnp.int32)
    tile_n = jnp.clip(counts[tile_e] - (tile_start - pad_start[tile_e]), 0, tme).astype(jnp.int32)

    xs = _dispatch_call(pos_flat, x, jnp.zeros((n_tiles * tme, d), F32), tm=tm)
    ys = _expert_call(tile_e, tile_n, xs, we_gate.astype(BF16), we_up.astype(BF16),
                      we_down.astype(BF16), tme=tme, fc=fc)
    return _combine_call(pos_flat, x, p, info, ys, wpg, bpg, wpp, g2, b2, alpha=alpha, tm=tm)


def _row(v):
    return v.reshape(1, -1)


def kernel(x, p, w_in, b_in, conf_conv_w, conf_conv_b, conf_ln_g, conf_ln_b, w_conf_out,
           sc_conv_w, w_sc_out, w_o, ln1_g, ln1_b, w_ff_gate, w_ff_up, w_ff_down, w_router,
           b_router, we_gate, we_up, we_down, w_ple_gate, b_ple_gate, w_ple_proj, ln2_g, ln2_b,
           *, ts=512, tm=512, tme=512, fc_dense=1408, fc_moe=512):
    depth = w_in.shape[0]
    alpha = (2 * depth) ** 0.25
    bsz, seq, d = x.shape
    for i in range(depth):
        x = _mixer_call(
            x, w_in[i].astype(BF16), _row(b_in[i]), conf_conv_w[i], _row(conf_conv_b[i]),
            _row(conf_ln_g[i]), _row(conf_ln_b[i]), w_conf_out[i].astype(BF16), sc_conv_w[i],
            w_sc_out[i].astype(BF16), w_o[i].astype(BF16), _row(ln1_g[i]), _row(ln1_b[i]),
            alpha=alpha, ts=ts)
        xt = x.reshape(bsz * seq, d)
        pt = p[i].reshape(bsz * seq, -1)
        ple_w = (w_ple_gate[i].astype(BF16), _row(b_ple_gate[i]), w_ple_proj[i].astype(BF16),
                 _row(ln2_g[i]), _row(ln2_b[i]))
        j = i // 2
        if i % 2 == 0:
            xt = _dense_ffn_call(xt, pt, w_ff_gate[j].astype(BF16), w_ff_up[j].astype(BF16),
                                 w_ff_down[j].astype(BF16), *ple_w, alpha=alpha, tm=tm,
                                 fc=fc_dense)
        else:
            xt = _moe_layer(xt, pt, w_router[j], b_router[j], we_gate[j], we_up[j], we_down[j],
                            *ple_w, alpha=alpha, tm=tm, tme=tme, fc=fc_moe)
        x = xt.reshape(bsz, seq, d)
    return x
```

```python
import functools

import jax
import jax.numpy as jnp
from jax import lax
from jax.experimental import pallas as pl
from jax.experimental.pallas import tpu as pltpu

LN_EPS = 1e-5
TOP_K = 2
LANES = 128
SUBLANES = 8
CONV_HALO = 32
SC_HALO = 8
MXU_COLS = 256
CONV_ROWS = 64
NEG_BIG = -1e30
VMEM_LIMIT = 56 * 1024 * 1024

F32 = jnp.float32
BF16 = jnp.bfloat16


def _dot(a, b):
    return jnp.dot(a, b, preferred_element_type=F32)


def _sigmoid(x):
    return 1.0 / (1.0 + jnp.exp(-x))


def _layer_norm(x, g, b):
    mu = jnp.mean(x, axis=-1, keepdims=True)
    xc = x - mu
    var = jnp.mean(xc * xc, axis=-1, keepdims=True)
    return xc * lax.rsqrt(var + LN_EPS) * g + b


def _const_spec(shape):
    nd = len(shape)
    return pl.BlockSpec(shape, lambda *_: (0,) * nd, pipeline_mode=pl.Buffered(1))


def _causal_dwconv_chunk(src_ref, row0, halo, w_ref, taps, lanes, init):
    acc = init
    first = halo - (taps - 1)
    for r in range(SUBLANES):
        offs = [o for o in range(first, halo + 1) if o % SUBLANES == r]
        if not offs:
            continue
        base = offs[0]
        span = offs[-1] - base + CONV_ROWS
        blk = src_ref[row0 + base:row0 + base + span, lanes]
        part = None
        for o in offs:
            j = o - first
            term = w_ref[j:j + 1, lanes] * blk[o - base:o - base + CONV_ROWS, :]
            part = term if part is None else part + term
        acc = acc + part
    return acc


def _mixer_kernel(x_ref, w_in_ref, b_in_ref, ccw_ref, ccb_ref, clg_ref, clb_ref, wa_ref,
                  scw_ref, wb_ref, wo_ref, g1_ref, b1_ref, o_ref,
                  a_ext, u_ext, scb_buf, act_a, act_b, zg_buf, *, alpha, ts, dc, ds, kc, ks):
    s = pl.program_id(1)

    @pl.when(s == 0)
    def _():
        a_ext[0:CONV_HALO, :] = jnp.zeros((CONV_HALO, dc), F32)
        u_ext[0:SC_HALO, :] = jnp.zeros((SC_HALO, ds), F32)

    @pl.when(s > 0)
    def _():
        a_ext[0:CONV_HALO, :] = a_ext[ts:ts + CONV_HALO, :]
        u_ext[0:SC_HALO, :] = u_ext[ts:ts + SC_HALO, :]

    x = x_ref[0]
    xb = x.astype(BF16)

    c0 = 2 * dc
    c1 = c0 + 3 * ds
    zc = _dot(xb, w_in_ref[:, 0:c0]) + b_in_ref[:, 0:c0]
    a_ext[CONV_HALO:CONV_HALO + ts, :] = zc[:, :dc] * _sigmoid(zc[:, dc:])
    zs = _dot(xb, w_in_ref[:, c0:c1]) + b_in_ref[:, c0:c1]
    scb_buf[...] = zs[:, :ds]
    u_ext[SC_HALO:SC_HALO + ts, :] = zs[:, ds:2 * ds] * zs[:, 2 * ds:]

    d = x.shape[-1]
    n_piece = 2 * d // MXU_COLS
    units = (ts // CONV_ROWS) * (dc // LANES)
    zg_pieces = []

    def gate_piece(j):
        cols = slice(c1 + j * MXU_COLS, c1 + (j + 1) * MXU_COLS)
        z = _dot(xb, w_in_ref[:, cols]) + b_in_ref[:, cols]
        zg_buf[:, j * MXU_COLS:(j + 1) * MXU_COLS] = z
        zg_pieces.append(z[0:SUBLANES, 0:LANES])

    def after(piece):
        bits = pltpu.bitcast(piece, jnp.uint32)
        return pltpu.bitcast((bits >> 16) >> 16, F32)

    u = 0
    for ci in range(ts // CONV_ROWS):
        row0 = ci * CONV_ROWS
        conv = []
        for cg in range(dc // LANES):
            lanes = slice(cg * LANES, (cg + 1) * LANES)
            want = min(n_piece, -(-(u + 1) * n_piece // units))
            while len(zg_pieces) < want:
                gate_piece(len(zg_pieces))
            u += 1
            init = jnp.broadcast_to(ccb_ref[:, lanes], (CONV_ROWS, LANES))
            init = init + jnp.tile(after(zg_pieces[-1]), (CONV_ROWS // SUBLANES, 1))
            conv.append(_causal_dwconv_chunk(a_ext, row0, CONV_HALO, ccw_ref, kc, lanes, init))
        conv = jnp.concatenate(conv, axis=-1)
        ln = _layer_norm(conv, clg_ref[...], clb_ref[...])
        act_a[row0:row0 + CONV_ROWS, :] = (ln * _sigmoid(ln)).astype(BF16)
        sconv = []
        for cg in range(ds // LANES):
            lanes = slice(cg * LANES, (cg + 1) * LANES)
            init = jnp.zeros((CONV_ROWS, LANES), F32)
            sconv.append(_causal_dwconv_chunk(u_ext, row0, SC_HALO, scw_ref, ks, lanes, init))
        sconv = jnp.concatenate(sconv, axis=-1)
        act_b[row0:row0 + CONV_ROWS, :] = (scb_buf[row0:row0 + CONV_ROWS, :] * sconv).astype(BF16)
    zg = zg_buf[...]

    y_a = _dot(act_a[...], wa_ref[...])
    y_b = _dot(act_b[...], wb_ref[...])
    m = _sigmoid(zg[:, :d]) * y_a + _sigmoid(zg[:, d:]) * y_b
    mix = _dot(m.astype(BF16), wo_ref[...])
    o_ref[0] = _layer_norm(alpha * x + mix, g1_ref[...], b1_ref[...])


def _mixer_call(x, w_in, b_in, ccw, ccb, clg, clb, wa, scw, wb, wo, g1, b1, *, alpha, ts):
    bsz, seq, d = x.shape
    kc, dc = ccw.shape
    ks, ds = scw.shape
    assert seq % ts == 0 and ts % CONV_ROWS == 0 and ts >= CONV_HALO
    assert kc - 1 <= CONV_HALO and ks - 1 <= SC_HALO
    assert dc % LANES == 0 and ds % LANES == 0 and (2 * d) % MXU_COLS == 0
    kern = functools.partial(_mixer_kernel, alpha=alpha, ts=ts, dc=dc, ds=ds, kc=kc, ks=ks)
    consts = (w_in, b_in, ccw, ccb, clg, clb, wa, scw, wb, wo, g1, b1)
    return pl.pallas_call(
        kern,
        grid=(bsz, seq // ts),
        in_specs=[pl.BlockSpec((1, ts, d), lambda b, s: (b, s, 0))]
        + [_const_spec(c.shape) for c in consts],
        out_specs=pl.BlockSpec((1, ts, d), lambda b, s: (b, s, 0)),
        out_shape=jax.ShapeDtypeStruct((bsz, seq, d), F32),
        scratch_shapes=[
            pltpu.VMEM((ts + CONV_HALO, dc), F32),
            pltpu.VMEM((ts + SC_HALO, ds), F32),
            pltpu.VMEM((ts, ds), F32),
            pltpu.VMEM((ts, dc), BF16),
            pltpu.VMEM((ts, ds), BF16),
            pltpu.VMEM((ts, 2 * d), F32),
        ],
        compiler_params=pltpu.CompilerParams(
            dimension_semantics=("arbitrary", "arbitrary"), vmem_limit_bytes=VMEM_LIMIT),
        name="mixer",
    )(x, *consts)


def _ple(xb, p, wpg_ref, bpg_ref, wpp_ref):
    gate = _sigmoid(_dot(xb, wpg_ref[...]) + bpg_ref[...])
    return gate * _dot(p.astype(BF16), wpp_ref[...])


def _dense_ffn_kernel(x_ref, p_ref, wg_ref, wu_ref, wd_ref, wpg_ref, bpg_ref, wpp_ref,
                      g2_ref, b2_ref, o_ref, *, alpha, fc):
    x = x_ref[...]
    xb = x.astype(BF16)
    f = wg_ref.shape[1]
    acc = alpha * x + _ple(xb, p_ref[...], wpg_ref, bpg_ref, wpp_ref)
    for c in range(f // fc):
        cols = slice(c * fc, (c + 1) * fc)
        g = _dot(xb, wg_ref[:, cols])
        u = _dot(xb, wu_ref[:, cols])
        h = (g * _sigmoid(g) * u).astype(BF16)
        acc = acc + _dot(h, wd_ref[cols, :])
    o_ref[...] = _layer_norm(acc, g2_ref[...], b2_ref[...])


def _dense_ffn_call(x, p, wg, wu, wd, wpg, bpg, wpp, g2, b2, *, alpha, tm, fc):
    t, d = x.shape
    f = wg.shape[1]
    assert t % tm == 0 and f % fc == 0 and fc % LANES == 0
    consts = (wg, wu, wd, wpg, bpg, wpp, g2, b2)
    return pl.pallas_call(
        functools.partial(_dense_ffn_kernel, alpha=alpha, fc=fc),
        grid=(t // tm,),
        in_specs=[pl.BlockSpec((tm, d), lambda i: (i, 0)),
                  pl.BlockSpec((tm, p.shape[1]), lambda i: (i, 0))]
        + [_const_spec(c.shape) for c in consts],
        out_specs=pl.BlockSpec((tm, d), lambda i: (i, 0)),
        out_shape=jax.ShapeDtypeStruct((t, d), F32),
        compiler_params=pltpu.CompilerParams(
            dimension_semantics=("arbitrary",), vmem_limit_bytes=VMEM_LIMIT),
        name="dense_ffn",
    )(x, p, *consts)


def _router_kernel(x_ref, wr_ref, br_ref, info_ref, cnt_ref, carry, *, tm):
    i = pl.program_id(0)

    @pl.when(i == 0)
    def _():
        carry[...] = jnp.zeros_like(carry)

    logits = _dot(x_ref[...].astype(BF16), wr_ref[...]) + br_ref[...]
    lane = lax.broadcasted_iota(jnp.int32, logits.shape, 1)
    m1 = jnp.max(logits, axis=-1, keepdims=True)
    i1 = jnp.min(jnp.where(logits == m1, lane, LANES), axis=-1, keepdims=True)
    rest = jnp.where(lane == i1, 2.0 * NEG_BIG, logits)
    m2 = jnp.max(rest, axis=-1, keepdims=True)
    i2 = jnp.min(jnp.where(rest == m2, lane, LANES), axis=-1, keepdims=True)
    e21 = jnp.exp(m2 - m1)
    w1 = 1.0 / (1.0 + e21)
    w2 = e21 / (1.0 + e21)

    hit1 = lane == i1
    hit2 = lane == i2
    onehot = jnp.where(hit1 | hit2, 1.0, 0.0)
    row = lax.broadcasted_iota(jnp.int32, (tm, tm), 0)
    col = lax.broadcasted_iota(jnp.int32, (tm, tm), 1)
    lower = jnp.where(col < row, 1.0, 0.0).astype(BF16)
    ranks = _dot(lower, onehot.astype(BF16)) + carry[...]
    r1 = jnp.sum(jnp.where(hit1, ranks, 0.0), axis=-1, keepdims=True)
    r2 = jnp.sum(jnp.where(hit2, ranks, 0.0), axis=-1, keepdims=True)
    carry[...] = carry[...] + jnp.sum(onehot, axis=0, keepdims=True)

    cols = (i1.astype(F32), i2.astype(F32), r1, r2, w1, w2)
    info = jnp.zeros(logits.shape, F32)
    for k, v in enumerate(cols):
        info = jnp.where(lane == k, v, info)
    info_ref[...] = info
    cnt_ref[...] = jnp.broadcast_to(carry[...], cnt_ref.shape)


def _router_call(x, wr, br, *, tm):
    t, d = x.shape
    assert t % tm == 0 and t < 2 ** 24
    return pl.pallas_call(
        functools.partial(_router_kernel, tm=tm),
        grid=(t // tm,),
        in_specs=[pl.BlockSpec((tm, d), lambda i: (i, 0)),
                  _const_spec(wr.shape), _const_spec(br.shape)],
        out_specs=[pl.BlockSpec((tm, LANES), lambda i: (i, 0)),
                   pl.BlockSpec((SUBLANES, LANES), lambda i: (0, 0))],
        out_shape=[jax.ShapeDtypeStruct((t, LANES), F32),
                   jax.ShapeDtypeStruct((SUBLANES, LANES), F32)],
        scratch_shapes=[pltpu.VMEM((1, LANES), F32)],
        compiler_params=pltpu.CompilerParams(
            dimension_semantics=("arbitrary",), vmem_limit_bytes=VMEM_LIMIT),
        name="router",
    )(x, wr, br)


def _dispatch_kernel(pos_ref, x_ref, xs_in_ref, xs_ref, sem, *, tm):
    del xs_in_ref

    def row_copy(r, k):
        return pltpu.make_async_copy(
            x_ref.at[pl.ds(r, 1)], xs_ref.at[pl.ds(pos_ref[TOP_K * r + k], 1)], sem)

    def issue(r, c):
        for k in range(TOP_K):
            row_copy(r, k).start()
        return c

    lax.fori_loop(0, tm, issue, 0)
    for _ in range(TOP_K):
        pltpu.make_async_copy(x_ref, xs_ref.at[pl.ds(0, tm)], sem).wait()


def _dispatch_call(pos_flat, x, xs_init, *, tm):
    t, d = x.shape
    return pl.pallas_call(
        functools.partial(_dispatch_kernel, tm=tm),
        grid=(t // tm,),
        in_specs=[pl.BlockSpec((TOP_K * tm,), lambda i: (i,), memory_space=pltpu.SMEM),
                  pl.BlockSpec((tm, d), lambda i: (i, 0)),
                  pl.BlockSpec(memory_space=pl.ANY)],
        out_specs=pl.BlockSpec(memory_space=pl.ANY),
        out_shape=jax.ShapeDtypeStruct(xs_init.shape, xs_init.dtype),
        scratch_shapes=[pltpu.SemaphoreType.DMA(())],
        input_output_aliases={2: 0},
        compiler_params=pltpu.CompilerParams(
            dimension_semantics=("arbitrary",), vmem_limit_bytes=VMEM_LIMIT),
        name="dispatch",
    )(pos_flat, x, xs_init)


def _expert_kernel(tile_e_ref, tile_n_ref, xs_ref, wg_ref, wu_ref, wd_ref, ys_ref, *, fc):
    del tile_e_ref
    i = pl.program_id(0)

    @pl.when(tile_n_ref[i] > 0)
    def _():
        xb = xs_ref[...].astype(BF16)
        f = wg_ref.shape[2]
        acc = jnp.zeros(ys_ref.shape, F32)
        for c in range(f // fc):
            cols = slice(c * fc, (c + 1) * fc)
            g = _dot(xb, wg_ref[0, :, cols])
            u = _dot(xb, wu_ref[0, :, cols])
            h = (g * _sigmoid(g) * u).astype(BF16)
            acc = acc + _dot(h, wd_ref[0, cols, :])
        ys_ref[...] = acc

    @pl.when(tile_n_ref[i] == 0)
    def _():
        ys_ref[...] = jnp.zeros(ys_ref.shape, F32)


def _expert_call(tile_e, tile_n, xs, wg, wu, wd, *, tme, fc):
    n_rows, d = xs.shape
    f = wg.shape[2]
    assert n_rows % tme == 0 and f % fc == 0 and fc % LANES == 0

    def w_spec(shape):
        return pl.BlockSpec((1,) + shape[1:], lambda i, te, tn: (te[i], 0, 0),
                            pipeline_mode=pl.Buffered(1))

    return pl.pallas_call(
        functools.partial(_expert_kernel, fc=fc),
        grid_spec=pltpu.PrefetchScalarGridSpec(
            num_scalar_prefetch=2,
            grid=(n_rows // tme,),
            in_specs=[pl.BlockSpec((tme, d), lambda i, te, tn: (i, 0)),
                      w_spec(wg.shape), w_spec(wu.shape), w_spec(wd.shape)],
            out_specs=pl.BlockSpec((tme, d), lambda i, te, tn: (i, 0)),
        ),
        out_shape=jax.ShapeDtypeStruct((n_rows, d), F32),
        compiler_params=pltpu.CompilerParams(
            dimension_semantics=("arbitrary",), vmem_limit_bytes=VMEM_LIMIT),
        name="experts",
    )(tile_e, tile_n, xs, wg, wu, wd)


def _combine_kernel(pos_ref, x_ref, p_ref, info_ref, ys_ref, wpg_ref, bpg_ref, wpp_ref,
                    g2_ref, b2_ref, o_ref, ybuf, sem, *, alpha, tm):
    def row_copy(r, k):
        return pltpu.make_async_copy(
            ys_ref.at[pl.ds(pos_ref[TOP_K * r + k], 1)], ybuf.at[k, pl.ds(r, 1)], sem)

    def issue(r, c):
        for k in range(TOP_K):
            row_copy(r, k).start()
        return c

    lax.fori_loop(0, tm, issue, 0)

    x = x_ref[...]
    acc = alpha * x + _ple(x.astype(BF16), p_ref[...], wpg_ref, bpg_ref, wpp_ref)

    for k in range(TOP_K):
        pltpu.make_async_copy(ys_ref.at[pl.ds(0, tm)], ybuf.at[k], sem).wait()
    info = info_ref[...]
    for k in range(TOP_K):
        acc = acc + info[:, 4 + k:5 + k] * ybuf[k]
    o_ref[...] = _layer_norm(acc, g2_ref[...], b2_ref[...])


def _combine_call(pos_flat, x, p, info, ys, wpg, bpg, wpp, g2, b2, *, alpha, tm):
    t, d = x.shape
    consts = (wpg, bpg, wpp, g2, b2)
    return pl.pallas_call(
        functools.partial(_combine_kernel, alpha=alpha, tm=tm),
        grid=(t // tm,),
        in_specs=[pl.BlockSpec((TOP_K * tm,), lambda i: (i,), memory_space=pltpu.SMEM),
                  pl.BlockSpec((tm, d), lambda i: (i, 0)),
                  pl.BlockSpec((tm, p.shape[1]), lambda i: (i, 0)),
                  pl.BlockSpec((tm, LANES), lambda i: (i, 0)),
                  pl.BlockSpec(memory_space=pl.ANY)]
        + [_const_spec(c.shape) for c in consts],
        out_specs=pl.BlockSpec((tm, d), lambda i: (i, 0)),
        out_shape=jax.ShapeDtypeStruct((t, d), F32),
        scratch_shapes=[pltpu.VMEM((TOP_K, tm, d), F32), pltpu.SemaphoreType.DMA(())],
        compiler_params=pltpu.CompilerParams(
            dimension_semantics=("arbitrary",), vmem_limit_bytes=VMEM_LIMIT),
        name="combine",
    )(pos_flat, x, p, info, ys, *consts)


def _moe_layer(x, p, w_router, b_router, we_gate, we_up, we_down, wpg, bpg, wpp, g2, b2,
               *, alpha, tm, tme, fc):
    t, d = x.shape
    n_exp = w_router.shape[1]
    assert n_exp <= LANES
    wr = jnp.zeros((d, LANES), BF16).at[:, :n_exp].set(w_router.astype(BF16))
    br = jnp.full((1, LANES), NEG_BIG, F32).at[0, :n_exp].set(b_router)
    info, cnt = _router_call(x, wr, br, tm=tm)

    counts = cnt[0, :n_exp].astype(jnp.int32)
    padded = (counts + tme - 1) // tme * tme
    pad_end = jnp.cumsum(padded)
    pad_start = pad_end - padded
    experts = info[:, 0:TOP_K].astype(jnp.int32)
    ranks = info[:, TOP_K:2 * TOP_K].astype(jnp.int32)
    pos_flat = (pad_start[experts] + ranks).reshape(-1)
    n_tiles = (t * TOP_K) // tme + n_exp
    tile_start = jnp.arange(n_tiles, dtype=jnp.int32) * tme
    tile_e = jnp.minimum(jnp.searchsorted(pad_end, tile_start, side="right"),
                         n_exp - 1).astype(jnp.int32)
    tile_n = jnp.clip(counts[tile_e] - (tile_start - pad_start[tile_e]), 0, tme).astype(jnp.int32)

    xs = _dispatch_call(pos_flat, x, jnp.zeros((n_tiles * tme, d), F32), tm=tm)
    ys = _expert_call(tile_e, tile_n, xs, we_gate.astype(BF16), we_up.astype(BF16),
                      we_down.astype(BF16), tme=tme, fc=fc)
    return _combine_call(pos_flat, x, p, info, ys, wpg, bpg, wpp, g2, b2, alpha=alpha, tm=tm)


def _row(v):
    return v.reshape(1, -1)


def kernel(x, p, w_in, b_in, conf_conv_w, conf_conv_b, conf_ln_g, conf_ln_b, w_conf_out,
           sc_conv_w, w_sc_out, w_o, ln1_g, ln1_b, w_ff_gate, w_ff_up, w_ff_down, w_router,
           b_router, we_gate, we_up, we_down, w_ple_gate, b_ple_gate, w_ple_proj, ln2_g, ln2_b,
           *, ts=512, tm=512, tme=512, fc_dense=1408, fc_moe=512):
    depth = w_in.shape[0]
    alpha = (2 * depth) ** 0.25
    bsz, seq, d = x.shape
    for i in range(depth):
        x = _mixer_call(
            x, w_in[i].astype(BF16), _row(b_in[i]), conf_conv_w[i], _row(conf_conv_b[i]),
            _row(conf_ln_g[i]), _row(conf_ln_b[i]), w_conf_out[i].astype(BF16), sc_conv_w[i],
            w_sc_out[i].astype(BF16), w_o[i].astype(BF16), _row(ln1_g[i]), _row(ln1_b[i]),
            alpha=alpha, ts=ts)
        xt = x.reshape(bsz * seq, d)
        pt = p[i].reshape(bsz * seq, -1)
        ple_w = (w_ple_gate[i].astype(BF16), _row(b_ple_gate[i]), w_ple_proj[i].astype(BF16),
                 _row(ln2_g[i]), _row(ln2_b[i]))
        j = i // 2
        if i % 2 == 0:
            xt = _dense_ffn_call(xt, pt, w_ff_gate[j].astype(BF16), w_ff_up[j].astype(BF16),
                                 w_ff_down[j].astype(BF16), *ple_w, alpha=alpha, tm=tm,
                                 fc=fc_dense)
        else:
            xt = _moe_layer(xt, pt, w_router[j], b_router[j], we_gate[j], we_up[j], we_down[j],
                            *ple_w, alpha=alpha, tm=tm, tme=tme, fc=fc_moe)
        x = xt.reshape(bsz, seq, d)
    return x
```

```python
import functools

import jax
import jax.numpy as jnp
from jax import lax
from jax.experimental import pallas as pl
from jax.experimental.pallas import tpu as pltpu

LN_EPS = 1e-5
TOP_K = 2
LANES = 128
SUBLANES = 8
CONV_HALO = 32
SC_HALO = 8
MXU_COLS = 256
CONV_ROWS = 64
NEG_BIG = -1e30
VMEM_LIMIT = 56 * 1024 * 1024

F32 = jnp.float32
BF16 = jnp.bfloat16


def _dot(a, b):
    return jnp.dot(a, b, preferred_element_type=F32)


def _sigmoid(x):
    return 1.0 / (1.0 + jnp.exp(-x))


def _layer_norm(x, g, b):
    mu = jnp.mean(x, axis=-1, keepdims=True)
    xc = x - mu
    var = jnp.mean(xc * xc, axis=-1, keepdims=True)
    return xc * lax.rsqrt(var + LN_EPS) * g + b


def _const_spec(shape):
    nd = len(shape)
    return pl.BlockSpec(shape, lambda *_: (0,) * nd, pipeline_mode=pl.Buffered(1))


def _causal_dwconv_chunk(src_ref, row0, halo, w_ref, taps, lanes, init):
    acc = init
    first = halo - (taps - 1)
    for r in range(SUBLANES):
        offs = [o for o in range(first, halo + 1) if o % SUBLANES == r]
        if not offs:
            continue
        base = offs[0]
        span = offs[-1] - base + CONV_ROWS
        blk = src_ref[row0 + base:row0 + base + span, lanes]
        part = None
        for o in offs:
            j = o - first
            term = w_ref[j:j + 1, lanes] * blk[o - base:o - base + CONV_ROWS, :]
            part = term if part is None else part + term
        acc = acc + part
    return acc


def _mixer_kernel(x_ref, w_in_ref, b_in_ref, ccw_ref, ccb_ref, clg_ref, clb_ref, wa_ref,
                  scw_ref, wb_ref, wo_ref, g1_ref, b1_ref, o_ref,
                  a_ext, u_ext, scb_buf, act_a, act_b, zg_buf, *, alpha, ts, dc, ds, kc, ks):
    s = pl.program_id(1)

    @pl.when(s == 0)
    def _():
        a_ext[0:CONV_HALO, :] = jnp.zeros((CONV_HALO, dc), F32)
        u_ext[0:SC_HALO, :] = jnp.zeros((SC_HALO, ds), F32)

    @pl.when(s > 0)
    def _():
        a_ext[0:CONV_HALO, :] = a_ext[ts:ts + CONV_HALO, :]
        u_ext[0:SC_HALO, :] = u_ext[ts:ts + SC_HALO, :]

    x = x_ref[0]
    xb = x.astype(BF16)

    c0 = 2 * dc
    c1 = c0 + 3 * ds
    zc = _dot(xb, w_in_ref[:, 0:c0]) + b_in_ref[:, 0:c0]
    a_ext[CONV_HALO:CONV_HALO + ts, :] = zc[:, :dc] * _sigmoid(zc[:, dc:])
    zs = _dot(xb, w_in_ref[:, c0:c1]) + b_in_ref[:, c0:c1]
    scb_buf[...] = zs[:, :ds]
    u_ext[SC_HALO:SC_HALO + ts, :] = zs[:, ds:2 * ds] * zs[:, 2 * ds:]

    d = x.shape[-1]
    n_piece = 2 * d // MXU_COLS
    units = (ts // CONV_ROWS) * (dc // LANES)
    zg_pieces = []

    def gate_piece(j):
        cols = slice(c1 + j * MXU_COLS, c1 + (j + 1) * MXU_COLS)
        z = _dot(xb, w_in_ref[:, cols]) + b_in_ref[:, cols]
        zg_buf[:, j * MXU_COLS:(j + 1) * MXU_COLS] = z
        zg_pieces.append(z[0:SUBLANES, 0:LANES])

    def after(piece):
        bits = pltpu.bitcast(piece, jnp.uint32)
        return pltpu.bitcast((bits >> 16) >> 16, F32)

    u = 0
    for ci in range(ts // CONV_ROWS):
        row0 = ci * CONV_ROWS
        conv = []
        for cg in range(dc // LANES):
            lanes = slice(cg * LANES, (cg + 1) * LANES)
            want = min(n_piece, -(-(u + 1) * n_piece // units))
            while len(zg_pieces) < want:
                gate_piece(len(zg_pieces))
            u += 1
            init = jnp.broadcast_to(ccb_ref[:, lanes], (CONV_ROWS, LANES))
            init = init + jnp.tile(after(zg_pieces[-1]), (CONV_ROWS // SUBLANES, 1))
            conv.append(_causal_dwconv_chunk(a_ext, row0, CONV_HALO, ccw_ref, kc, lanes, init))
        conv = jnp.concatenate(conv, axis=-1)
        ln = _layer_norm(conv, clg_ref[...], clb_ref[...])
        act_a[row0:row0 + CONV_ROWS, :] = (ln * _sigmoid(ln)).astype(BF16)
        sconv = []
        for cg in range(ds // LANES):
            lanes = slice(cg * LANES, (cg + 1) * LANES)
            init = jnp.zeros((CONV_ROWS, LANES), F32)
            sconv.append(_causal_dwconv_chunk(u_ext, row0, SC_HALO, scw_ref, ks, lanes, init))
        sconv = jnp.concatenate(sconv, axis=-1)
        act_b[row0:row0 + CONV_ROWS, :] = (scb_buf[row0:row0 + CONV_ROWS, :] * sconv).astype(BF16)
    zg = zg_buf[...]

    y_a = _dot(act_a[...], wa_ref[...])
    y_b = _dot(act_b[...], wb_ref[...])
    m = _sigmoid(zg[:, :d]) * y_a + _sigmoid(zg[:, d:]) * y_b
    mix = _dot(m.astype(BF16), wo_ref[...])
    o_ref[0] = _layer_norm(alpha * x + mix, g1_ref[...], b1_ref[...])


def _mixer_call(x, w_in, b_in, ccw, ccb, clg, clb, wa, scw, wb, wo, g1, b1, *, alpha, ts):
    bsz, seq, d = x.shape
    kc, dc = ccw.shape
    ks, ds = scw.shape
    assert seq % ts == 0 and ts % CONV_ROWS == 0 and ts >= CONV_HALO
    assert kc - 1 <= CONV_HALO and ks - 1 <= SC_HALO
    assert dc % LANES == 0 and ds % LANES == 0 and (2 * d) % MXU_COLS == 0
    kern = functools.partial(_mixer_kernel, alpha=alpha, ts=ts, dc=dc, ds=ds, kc=kc, ks=ks)
    consts = (w_in, b_in, ccw, ccb, clg, clb, wa, scw, wb, wo, g1, b1)
    return pl.pallas_call(
        kern,
        grid=(bsz, seq // ts),
        in_specs=[pl.BlockSpec((1, ts, d), lambda b, s: (b, s, 0))]
        + [_const_spec(c.shape) for c in consts],
        out_specs=pl.BlockSpec((1, ts, d), lambda b, s: (b, s, 0)),
        out_shape=jax.ShapeDtypeStruct((bsz, seq, d), F32),
        scratch_shapes=[
            pltpu.VMEM((ts + CONV_HALO, dc), F32),
            pltpu.VMEM((ts + SC_HALO, ds), F32),
            pltpu.VMEM((ts, ds), F32),
            pltpu.VMEM((ts, dc), BF16),
            pltpu.VMEM((ts, ds), BF16),
            pltpu.VMEM((ts, 2 * d), F32),
        ],
        compiler_params=pltpu.CompilerParams(
            dimension_semantics=("arbitrary", "arbitrary"), vmem_limit_bytes=VMEM_LIMIT),
        name="mixer",
    )(x, *consts)


def _ple(xb, p, wpg_ref, bpg_ref, wpp_ref):
    gate = _sigmoid(_dot(xb, wpg_ref[...]) + bpg_ref[...])
    return gate * _dot(p.astype(BF16), wpp_ref[...])


def _dense_ffn_kernel(x_ref, p_ref, wg_ref, wu_ref, wd_ref, wpg_ref, bpg_ref, wpp_ref,
                      g2_ref, b2_ref, o_ref, *, alpha, fc):
    x = x_ref[...]
    xb = x.astype(BF16)
    f = wg_ref.shape[1]
    acc = alpha * x + _ple(xb, p_ref[...], wpg_ref, bpg_ref, wpp_ref)
    for c in range(f // fc):
        cols = slice(c * fc, (c + 1) * fc)
        g = _dot(xb, wg_ref[:, cols])
        u = _dot(xb, wu_ref[:, cols])
        h = (g * _sigmoid(g) * u).astype(BF16)
        acc = acc + _dot(h, wd_ref[cols, :])
    o_ref[...] = _layer_norm(acc, g2_ref[...], b2_ref[...])


def _dense_ffn_call(x, p, wg, wu, wd, wpg, bpg, wpp, g2, b2, *, alpha, tm, fc):
    t, d = x.shape
    f = wg.shape[1]
    assert t % tm == 0 and f % fc == 0 and fc % LANES == 0
    consts = (wg, wu, wd, wpg, bpg, wpp, g2, b2)
    return pl.pallas_call(
        functools.partial(_dense_ffn_kernel, alpha=alpha, fc=fc),
        grid=(t // tm,),
        in_specs=[pl.BlockSpec((tm, d), lambda i: (i, 0)),
                  pl.BlockSpec((tm, p.shape[1]), lambda i: (i, 0))]
        + [_const_spec(c.shape) for c in consts],
        out_specs=pl.BlockSpec((tm, d), lambda i: (i, 0)),
        out_shape=jax.ShapeDtypeStruct((t, d), F32),
        compiler_params=pltpu.CompilerParams(
            dimension_semantics=("arbitrary",), vmem_limit_bytes=VMEM_LIMIT),
        name="dense_ffn",
    )(x, p, *consts)


ROW_CHUNKS = (512, 256, 128, 64, 32, 16, 8)


def _for_row_chunks(n, fn):
    off = 0
    for rows in ROW_CHUNKS:
        take = (n & rows) != 0
        pl.when(take)(functools.partial(fn, off, rows))
        off = off + jnp.where(take, rows, 0)


def _router_kernel(x_ref, wr_ref, br_ref, info_ref, cnt_ref, *, tm):
    logits = _dot(x_ref[...].astype(BF16), wr_ref[...]) + br_ref[...]
    lane = lax.broadcasted_iota(jnp.int32, logits.shape, 1)
    m1 = jnp.max(logits, axis=-1, keepdims=True)
    i1 = jnp.min(jnp.where(logits == m1, lane, LANES), axis=-1, keepdims=True)
    rest = jnp.where(lane == i1, 2.0 * NEG_BIG, logits)
    m2 = jnp.max(rest, axis=-1, keepdims=True)
    i2 = jnp.min(jnp.where(rest == m2, lane, LANES), axis=-1, keepdims=True)
    e21 = jnp.exp(m2 - m1)
    w1 = 1.0 / (1.0 + e21)
    w2 = e21 / (1.0 + e21)

    hit1 = lane == i1
    hit2 = lane == i2
    onehot = jnp.where(hit1 | hit2, 1.0, 0.0)
    row = lax.broadcasted_iota(jnp.int32, (tm, tm), 0)
    col = lax.broadcasted_iota(jnp.int32, (tm, tm), 1)
    lower = jnp.where(col < row, 1.0, 0.0).astype(BF16)
    ranks = _dot(lower, onehot.astype(BF16))
    r1 = jnp.sum(jnp.where(hit1, ranks, 0.0), axis=-1, keepdims=True)
    r2 = jnp.sum(jnp.where(hit2, ranks, 0.0), axis=-1, keepdims=True)

    cols = (i1.astype(F32), i2.astype(F32), r1, r2, w1, w2)
    info = jnp.zeros(logits.shape, F32)
    for k, v in enumerate(cols):
        info = jnp.where(lane == k, v, info)
    info_ref[...] = info
    cnt_ref[...] = jnp.broadcast_to(jnp.sum(onehot, axis=0, keepdims=True), cnt_ref.shape)


def _router_call(x, wr, br, *, tm):
    t, d = x.shape
    assert t % tm == 0 and tm < 2 ** 24
    return pl.pallas_call(
        functools.partial(_router_kernel, tm=tm),
        grid=(t // tm,),
        in_specs=[pl.BlockSpec((tm, d), lambda i: (i, 0)),
                  _const_spec(wr.shape), _const_spec(br.shape)],
        out_specs=[pl.BlockSpec((tm, LANES), lambda i: (i, 0)),
                   pl.BlockSpec((SUBLANES, LANES), lambda i: (i, 0))],
        out_shape=[jax.ShapeDtypeStruct((t, LANES), F32),
                   jax.ShapeDtypeStruct((t // tm * SUBLANES, LANES), F32)],
        compiler_params=pltpu.CompilerParams(
            dimension_semantics=("arbitrary",), vmem_limit_bytes=VMEM_LIMIT),
        name="router",
    )(x, wr, br)


def _local_slot(expert, rank, loff_ref, base, n_exp):
    off = jnp.zeros(expert.shape, F32)
    for e in range(n_exp):
        off = jnp.where(expert == float(e), loff_ref[base + e].astype(F32), off)
    return (off + rank).astype(jnp.int32)


def _dispatch_kernel(n8_ref, loff_ref, gdst_ref, gap_ref, x_ref, info_ref, xs_ref,
                     comp, zbuf, sems, *, tm, n_exp):
    i = pl.program_id(0)
    n_steps = pl.num_programs(0)
    slot = i % 2
    nc = comp.shape[1]

    def group_copy(step, e, s, off, rows):
        src = comp.at[s, pl.ds(pl.multiple_of(loff_ref[step * n_exp + e] + off, SUBLANES), rows)]
        dst = xs_ref.at[pl.ds(pl.multiple_of(gdst_ref[step * n_exp + e] + off, SUBLANES), rows)]
        return pltpu.make_async_copy(src, dst, sems.at[s])

    def start_groups(step, s):
        for e in range(n_exp):
            _for_row_chunks(n8_ref[step * n_exp + e],
                            lambda off, rows, e=e: group_copy(step, e, s, off, rows).start())

    def wait_groups(step, s):
        for e in range(n_exp):
            _for_row_chunks(n8_ref[step * n_exp + e],
                            lambda off, rows, e=e: group_copy(step, e, s, off, rows).wait())

    info_t = info_ref[...].T
    slot_row = lax.broadcasted_iota(jnp.int32, (nc, tm), 0)
    hit = None
    for k in range(TOP_K):
        s_k = _local_slot(info_t[k:k + 1, :], info_t[TOP_K + k:TOP_K + k + 1, :],
                          loff_ref, i * n_exp, n_exp)
        hit = (slot_row == s_k) if hit is None else hit | (slot_row == s_k)
    onehot = jnp.where(hit, 1.0, 0.0).astype(BF16)
    comp[slot] = _dot(onehot, x_ref[...].astype(BF16))

    @pl.when(i > 0)
    def _():
        wait_groups(i - 1, 1 - slot)

    start_groups(i, slot)

    @pl.when(i == n_steps - 1)
    def _():
        wait_groups(i, slot)
        zbuf[...] = jnp.zeros(zbuf.shape, F32)

        def gap_copy(e, off, rows):
            dst = xs_ref.at[pl.ds(pl.multiple_of(gap_ref[2 * e] + off, SUBLANES), rows)]
            return pltpu.make_async_copy(zbuf.at[pl.ds(0, rows)], dst, sems.at[0])

        for e in range(n_exp):
            _for_row_chunks(gap_ref[2 * e + 1],
                            lambda off, rows, e=e: gap_copy(e, off, rows).start())
        for e in range(n_exp):
            _for_row_chunks(gap_ref[2 * e + 1],
                            lambda off, rows, e=e: gap_copy(e, off, rows).wait())

        tail_rows = zbuf.shape[0]

        def tail_copy(j):
            start = pl.multiple_of(gap_ref[2 * n_exp] + j * tail_rows, SUBLANES)
            return pltpu.make_async_copy(zbuf, xs_ref.at[pl.ds(start, tail_rows)], sems.at[0])

        def tail(j, c):
            tail_copy(j).start()
            tail_copy(j).wait()
            return c

        lax.fori_loop(0, gap_ref[2 * n_exp + 1], tail, 0)


def _dispatch_call(n8, loff, gdst, gap, x, info, *, tm, tme, n_exp, n_rows):
    t, d = x.shape
    nc = TOP_K * tm + SUBLANES * n_exp
    return pl.pallas_call(
        functools.partial(_dispatch_kernel, tm=tm, n_exp=n_exp),
        grid_spec=pltpu.PrefetchScalarGridSpec(
            num_scalar_prefetch=4,
            grid=(t // tm,),
            in_specs=[pl.BlockSpec((tm, d), lambda i, *_: (i, 0)),
                      pl.BlockSpec((tm, LANES), lambda i, *_: (i, 0))],
            out_specs=pl.BlockSpec(memory_space=pl.ANY),
            scratch_shapes=[pltpu.VMEM((2, nc, d), F32),
                            pltpu.VMEM((tme, d), F32),
                            pltpu.SemaphoreType.DMA((2,))],
        ),
        out_shape=jax.ShapeDtypeStruct((n_rows, d), F32),
        compiler_params=pltpu.CompilerParams(
            dimension_semantics=("arbitrary",), vmem_limit_bytes=VMEM_LIMIT),
        name="dispatch",
    )(n8, loff, gdst, gap, x, info)


def _expert_kernel(tile_e_ref, tile_n_ref, xs_ref, wg_ref, wu_ref, wd_ref, ys_ref, *, fc):
    del tile_e_ref
    i = pl.program_id(0)

    @pl.when(tile_n_ref[i] > 0)
    def _():
        xb = xs_ref[...].astype(BF16)
        f = wg_ref.shape[2]
        acc = jnp.zeros(ys_ref.shape, F32)
        for c in range(f // fc):
            cols = slice(c * fc, (c + 1) * fc)
            g = _dot(xb, wg_ref[0, :, cols])
            u = _dot(xb, wu_ref[0, :, cols])
            h = (g * _sigmoid(g) * u).astype(BF16)
            acc = acc + _dot(h, wd_ref[0, cols, :])
        ys_ref[...] = acc

    @pl.when(tile_n_ref[i] == 0)
    def _():
        ys_ref[...] = jnp.zeros(ys_ref.shape, F32)


def _expert_call(tile_e, tile_n, xs, wg, wu, wd, *, tme, fc):
    n_rows, d = xs.shape
    f = wg.shape[2]
    assert n_rows % tme == 0 and f % fc == 0 and fc % LANES == 0

    def w_spec(shape):
        return pl.BlockSpec((1,) + shape[1:], lambda i, te, tn: (te[i], 0, 0),
                            pipeline_mode=pl.Buffered(1))

    return pl.pallas_call(
        functools.partial(_expert_kernel, fc=fc),
        grid_spec=pltpu.PrefetchScalarGridSpec(
            num_scalar_prefetch=2,
            grid=(n_rows // tme,),
            in_specs=[pl.BlockSpec((tme, d), lambda i, te, tn: (i, 0)),
                      w_spec(wg.shape), w_spec(wu.shape), w_spec(wd.shape)],
            out_specs=pl.BlockSpec((tme, d), lambda i, te, tn: (i, 0)),
        ),
        out_shape=jax.ShapeDtypeStruct((n_rows, d), F32),
        compiler_params=pltpu.CompilerParams(
            dimension_semantics=("arbitrary",), vmem_limit_bytes=VMEM_LIMIT),
        name="experts",
    )(tile_e, tile_n, xs, wg, wu, wd)


def _combine_kernel(n8_ref, loff_ref, gdst_ref, x_ref, p_ref, info_ref, ys_ref, wpg_ref, bpg_ref,
                    wpp_ref, g2_ref, b2_ref, o_ref, ycomp, sems, *, alpha, tm, n_exp):
    i = pl.program_id(0)
    n_steps = pl.num_programs(0)
    slot = i % 2
    nc = ycomp.shape[1]

    def group_copy(step, e, s, off, rows):
        src = ys_ref.at[pl.ds(pl.multiple_of(gdst_ref[step * n_exp + e] + off, SUBLANES), rows)]
        dst = ycomp.at[s, pl.ds(pl.multiple_of(loff_ref[step * n_exp + e] + off, SUBLANES), rows)]
        return pltpu.make_async_copy(src, dst, sems.at[s])

    def start_groups(step, s):
        for e in range(n_exp):
            _for_row_chunks(n8_ref[step * n_exp + e],
                            lambda off, rows, e=e: group_copy(step, e, s, off, rows).start())

    def wait_groups(step, s):
        for e in range(n_exp):
            _for_row_chunks(n8_ref[step * n_exp + e],
                            lambda off, rows, e=e: group_copy(step, e, s, off, rows).wait())

    @pl.when(i == 0)
    def _():
        ycomp[...] = jnp.zeros(ycomp.shape, F32)
        start_groups(0, 0)

    @pl.when(i + 1 < n_steps)
    def _():
        start_groups(i + 1, 1 - slot)

    x = x_ref[...]
    acc = alpha * x + _ple(x.astype(BF16), p_ref[...], wpg_ref, bpg_ref, wpp_ref)

    wait_groups(i, slot)
    info = info_ref[...]
    yb = ycomp[slot].astype(BF16)
    slot_col = lax.broadcasted_iota(jnp.int32, (tm, nc), 1)
    for k in range(TOP_K):
        s_k = _local_slot(info[:, k:k + 1], info[:, TOP_K + k:TOP_K + k + 1],
                          loff_ref, i * n_exp, n_exp)
        pick = jnp.where(slot_col == s_k, 1.0, 0.0).astype(BF16)
        acc = acc + info[:, 4 + k:5 + k] * _dot(pick, yb)
    o_ref[...] = _layer_norm(acc, g2_ref[...], b2_ref[...])


def _combine_call(n8, loff, gdst, x, p, info, ys, wpg, bpg, wpp, g2, b2, *, alpha, tm, n_exp):
    t, d = x.shape
    nc = TOP_K * tm + SUBLANES * n_exp
    consts = (wpg, bpg, wpp, g2, b2)

    def const_spec(shape):
        nd = len(shape)
        return pl.BlockSpec(shape, lambda i, *_: (0,) * nd, pipeline_mode=pl.Buffered(1))

    return pl.pallas_call(
        functools.partial(_combine_kernel, alpha=alpha, tm=tm, n_exp=n_exp),
        grid_spec=pltpu.PrefetchScalarGridSpec(
            num_scalar_prefetch=3,
            grid=(t // tm,),
            in_specs=[pl.BlockSpec((tm, d), lambda i, *_: (i, 0)),
                      pl.BlockSpec((tm, p.shape[1]), lambda i, *_: (i, 0)),
                      pl.BlockSpec((tm, LANES), lambda i, *_: (i, 0)),
                      pl.BlockSpec(memory_space=pl.ANY)]
            + [const_spec(c.shape) for c in consts],
            out_specs=pl.BlockSpec((tm, d), lambda i, *_: (i, 0)),
            scratch_shapes=[pltpu.VMEM((2, nc, d), F32), pltpu.SemaphoreType.DMA((2,))],
        ),
        out_shape=jax.ShapeDtypeStruct((t, d), F32),
        compiler_params=pltpu.CompilerParams(
            dimension_semantics=("arbitrary",), vmem_limit_bytes=VMEM_LIMIT),
        name="combine",
    )(n8, loff, gdst, x, p, info, ys, *consts)


def _moe_layer(x, p, w_router, b_router, we_gate, we_up, we_down, wpg, bpg, wpp, g2, b2,
               *, alpha, tm, tme, fc):
    t, d = x.shape
    n_exp = w_router.shape[1]
    n_tok_tiles = t // tm
    assert n_exp <= LANES and tm <= ROW_CHUNKS[0] and tme <= ROW_CHUNKS[0]
    wr = jnp.zeros((d, LANES), BF16).at[:, :n_exp].set(w_router.astype(BF16))
    br = jnp.full((1, LANES), NEG_BIG, F32).at[0, :n_exp].set(b_router)
    info, cnt = _router_call(x, wr, br, tm=tm)

    i32 = jnp.int32
    n = cnt[::SUBLANES, :n_exp].astype(i32)
    n8 = (n + SUBLANES - 1) // SUBLANES * SUBLANES
    loff = jnp.cumsum(n8, axis=1) - n8
    tot = jnp.sum(n8, axis=0)
    reg = (tot + tme - 1) // tme * tme
    reg_end = jnp.cumsum(reg)
    base = reg_end - reg
    gdst = base[None, :] + jnp.cumsum(n8, axis=0) - n8
    n_tiles = (t * TOP_K + n_tok_tiles * n_exp * (SUBLANES - 1)) // tme + n_exp + 1
    n_rows = n_tiles * tme
    tile_start = jnp.arange(n_tiles, dtype=i32) * tme
    tile_e = jnp.minimum(jnp.searchsorted(reg_end, tile_start, side="right"), n_exp - 1).astype(i32)
    tile_n = jnp.clip(tot[tile_e] - (tile_start - base[tile_e]), 0, tme).astype(i32)
    gap = jnp.stack([base + tot, reg - tot], axis=1).reshape(-1)
    gap = jnp.concatenate([gap, jnp.stack([reg_end[-1], (n_rows - reg_end[-1]) // tme])]).astype(i32)
    n8f, lofff, gdstf = (a.reshape(-1).astype(i32) for a in (n8, loff, gdst))

    xs = _dispatch_call(n8f, lofff, gdstf, gap, x, info, tm=tm, tme=tme, n_exp=n_exp, n_rows=n_rows)
    ys = _expert_call(tile_e, tile_n, xs, we_gate.astype(BF16), we_up.astype(BF16),
                      we_down.astype(BF16), tme=tme, fc=fc)
    return _combine_call(n8f, lofff, gdstf, x, p, info, ys, wpg, bpg, wpp, g2, b2,
                         alpha=alpha, tm=tm, n_exp=n_exp)


def _row(v):
    return v.reshape(1, -1)


def kernel(x, p, w_in, b_in, conf_conv_w, conf_conv_b, conf_ln_g, conf_ln_b, w_conf_out,
           sc_conv_w, w_sc_out, w_o, ln1_g, ln1_b, w_ff_gate, w_ff_up, w_ff_down, w_router,
           b_router, we_gate, we_up, we_down, w_ple_gate, b_ple_gate, w_ple_proj, ln2_g, ln2_b,
           *, ts=512, tm=512, tme=512, fc_dense=1408, fc_moe=512):
    depth = w_in.shape[0]
    alpha = (2 * depth) ** 0.25
    bsz, seq, d = x.shape
    for i in range(depth):
        x = _mixer_call(
            x, w_in[i].astype(BF16), _row(b_in[i]), conf_conv_w[i], _row(conf_conv_b[i]),
            _row(conf_ln_g[i]), _row(conf_ln_b[i]), w_conf_out[i].astype(BF16), sc_conv_w[i],
            w_sc_out[i].astype(BF16), w_o[i].astype(BF16), _row(ln1_g[i]), _row(ln1_b[i]),
            alpha=alpha, ts=ts)
        xt = x.reshape(bsz * seq, d)
        pt = p[i].reshape(bsz * seq, -1)
        ple_w = (w_ple_gate[i].astype(BF16), _row(b_ple_gate[i]), w_ple_proj[i].astype(BF16),
                 _row(ln2_g[i]), _row(ln2_b[i]))
        j = i // 2
        if i % 2 == 0:
            xt = _dense_ffn_call(xt, pt, w_ff_gate[j].astype(BF16), w_ff_up[j].astype(BF16),
                                 w_ff_down[j].astype(BF16), *ple_w, alpha=alpha, tm=tm,
                                 fc=fc_dense)
        else:
            xt = _moe_layer(xt, pt, w_router[j], b_router[j], we_gate[j], we_up[j], we_down[j],
                            *ple_w, alpha=alpha, tm=tm, tme=tme, fc=fc_moe)
        x = xt.reshape(bsz, seq, d)
    return x
```

```python
import functools

import jax
import jax.numpy as jnp
from jax import lax
from jax.experimental import pallas as pl
from jax.experimental.pallas import tpu as pltpu

LN_EPS = 1e-5
TOP_K = 2
LANES = 128
SUBLANES = 8
CONV_HALO = 32
SC_HALO = 8
MXU_COLS = 256
CONV_ROWS = 64
NEG_BIG = -1e30
VMEM_LIMIT = 56 * 1024 * 1024

F32 = jnp.float32
BF16 = jnp.bfloat16


def _dot(a, b):
    return jnp.dot(a, b, preferred_element_type=F32)


def _sigmoid(x):
    return 1.0 / (1.0 + jnp.exp(-x))


def _layer_norm(x, g, b):
    mu = jnp.mean(x, axis=-1, keepdims=True)
    xc = x - mu
    var = jnp.mean(xc * xc, axis=-1, keepdims=True)
    return xc * lax.rsqrt(var + LN_EPS) * g + b


def _const_spec(shape):
    nd = len(shape)
    return pl.BlockSpec(shape, lambda *_: (0,) * nd, pipeline_mode=pl.Buffered(1))


def _causal_dwconv_chunk(src_ref, row0, halo, w_ref, taps, lanes, init):
    acc = init
    first = halo - (taps - 1)
    for r in range(SUBLANES):
        offs = [o for o in range(first, halo + 1) if o % SUBLANES == r]
        if not offs:
            continue
        base = offs[0]
        span = offs[-1] - base + CONV_ROWS
        blk = src_ref[row0 + base:row0 + base + span, lanes]
        part = None
        for o in offs:
            j = o - first
            term = w_ref[j:j + 1, lanes] * blk[o - base:o - base + CONV_ROWS, :]
            part = term if part is None else part + term
        acc = acc + part
    return acc


def _mixer_kernel(x_ref, w_in_ref, b_in_ref, ccw_ref, ccb_ref, clg_ref, clb_ref, wa_ref,
                  scw_ref, wb_ref, wo_ref, g1_ref, b1_ref, o_ref,
                  a_ext, u_ext, scb_buf, act_a, act_b, zg_buf, *, alpha, ts, dc, ds, kc, ks):
    s = pl.program_id(1)

    @pl.when(s == 0)
    def _():
        a_ext[0:CONV_HALO, :] = jnp.zeros((CONV_HALO, dc), F32)
        u_ext[0:SC_HALO, :] = jnp.zeros((SC_HALO, ds), F32)

    @pl.when(s > 0)
    def _():
        a_ext[0:CONV_HALO, :] = a_ext[ts:ts + CONV_HALO, :]
        u_ext[0:SC_HALO, :] = u_ext[ts:ts + SC_HALO, :]

    x = x_ref[0]
    xb = x.astype(BF16)

    c0 = 2 * dc
    c1 = c0 + 3 * ds
    zc = _dot(xb, w_in_ref[:, 0:c0]) + b_in_ref[:, 0:c0]
    a_ext[CONV_HALO:CONV_HALO + ts, :] = zc[:, :dc] * _sigmoid(zc[:, dc:])
    zs = _dot(xb, w_in_ref[:, c0:c1]) + b_in_ref[:, c0:c1]
    scb_buf[...] = zs[:, :ds]
    u_ext[SC_HALO:SC_HALO + ts, :] = zs[:, ds:2 * ds] * zs[:, 2 * ds:]

    d = x.shape[-1]
    n_piece = 2 * d // MXU_COLS
    units = (ts // CONV_ROWS) * (dc // LANES)
    zg_pieces = []

    def gate_piece(j):
        cols = slice(c1 + j * MXU_COLS, c1 + (j + 1) * MXU_COLS)
        z = _dot(xb, w_in_ref[:, cols]) + b_in_ref[:, cols]
        zg_buf[:, j * MXU_COLS:(j + 1) * MXU_COLS] = z
        zg_pieces.append(z[0:SUBLANES, 0:LANES])

    def after(piece):
        bits = pltpu.bitcast(piece, jnp.uint32)
        return pltpu.bitcast((bits >> 16) >> 16, F32)

    u = 0
    for ci in range(ts // CONV_ROWS):
        row0 = ci * CONV_ROWS
        conv = []
        for cg in range(dc // LANES):
            lanes = slice(cg * LANES, (cg + 1) * LANES)
            want = min(n_piece, -(-(u + 1) * n_piece // units))
            while len(zg_pieces) < want:
                gate_piece(len(zg_pieces))
            u += 1
            init = jnp.broadcast_to(ccb_ref[:, lanes], (CONV_ROWS, LANES))
            init = init + jnp.tile(after(zg_pieces[-1]), (CONV_ROWS // SUBLANES, 1))
            conv.append(_causal_dwconv_chunk(a_ext, row0, CONV_HALO, ccw_ref, kc, lanes, init))
        conv = jnp.concatenate(conv, axis=-1)
        ln = _layer_norm(conv, clg_ref[...], clb_ref[...])
        act_a[row0:row0 + CONV_ROWS, :] = (ln * _sigmoid(ln)).astype(BF16)
        sconv = []
        for cg in range(ds // LANES):
            lanes = slice(cg * LANES, (cg + 1) * LANES)
            init = jnp.zeros((CONV_ROWS, LANES), F32)
            sconv.append(_causal_dwconv_chunk(u_ext, row0, SC_HALO, scw_ref, ks, lanes, init))
        sconv = jnp.concatenate(sconv, axis=-1)
        act_b[row0:row0 + CONV_ROWS, :] = (scb_buf[row0:row0 + CONV_ROWS, :] * sconv).astype(BF16)
    zg = zg_buf[...]

    y_a = _dot(act_a[...], wa_ref[...])
    y_b = _dot(act_b[...], wb_ref[...])
    m = _sigmoid(zg[:, :d]) * y_a + _sigmoid(zg[:, d:]) * y_b
    mix = _dot(m.astype(BF16), wo_ref[...])
    o_ref[0] = _layer_norm(alpha * x + mix, g1_ref[...], b1_ref[...])


def _mixer_call(x, w_in, b_in, ccw, ccb, clg, clb, wa, scw, wb, wo, g1, b1, *, alpha, ts):
    bsz, seq, d = x.shape
    kc, dc = ccw.shape
    ks, ds = scw.shape
    assert seq % ts == 0 and ts % CONV_ROWS == 0 and ts >= CONV_HALO
    assert kc - 1 <= CONV_HALO and ks - 1 <= SC_HALO
    assert dc % LANES == 0 and ds % LANES == 0 and (2 * d) % MXU_COLS == 0
    kern = functools.partial(_mixer_kernel, alpha=alpha, ts=ts, dc=dc, ds=ds, kc=kc, ks=ks)
    consts = (w_in, b_in, ccw, ccb, clg, clb, wa, scw, wb, wo, g1, b1)
    return pl.pallas_call(
        kern,
        grid=(bsz, seq // ts),
        in_specs=[pl.BlockSpec((1, ts, d), lambda b, s: (b, s, 0))]
        + [_const_spec(c.shape) for c in consts],
        out_specs=pl.BlockSpec((1, ts, d), lambda b, s: (b, s, 0)),
        out_shape=jax.ShapeDtypeStruct((bsz, seq, d), F32),
        scratch_shapes=[
            pltpu.VMEM((ts + CONV_HALO, dc), F32),
            pltpu.VMEM((ts + SC_HALO, ds), F32),
            pltpu.VMEM((ts, ds), F32),
            pltpu.VMEM((ts, dc), BF16),
            pltpu.VMEM((ts, ds), BF16),
            pltpu.VMEM((ts, 2 * d), F32),
        ],
        compiler_params=pltpu.CompilerParams(
            dimension_semantics=("arbitrary", "arbitrary"), vmem_limit_bytes=VMEM_LIMIT),
        name="mixer",
    )(x, *consts)


def _ple(xb, p, wpg_ref, bpg_ref, wpp_ref):
    gate = _sigmoid(_dot(xb, wpg_ref[...]) + bpg_ref[...])
    return gate * _dot(p.astype(BF16), wpp_ref[...])


def _dense_ffn_kernel(x_ref, p_ref, wg_ref, wu_ref, wd_ref, wpg_ref, bpg_ref, wpp_ref,
                      g2_ref, b2_ref, o_ref, *, alpha):
    x = x_ref[...]
    xb = x.astype(BF16)
    acc = alpha * x + _ple(xb, p_ref[...], wpg_ref, bpg_ref, wpp_ref)
    g = _dot(xb, wg_ref[...])
    u = _dot(xb, wu_ref[...])
    h = (g * _sigmoid(g) * u).astype(BF16)
    acc = acc + _dot(h, wd_ref[...])
    o_ref[...] = _layer_norm(acc, g2_ref[...], b2_ref[...])


def _dense_ffn_call(x, p, layer, wg, wu, wd, wpg, bpg, wpp, g2, b2, *, alpha, tm):
    t, d = x.shape
    assert t % tm == 0
    consts = (wg, wu, wd, wpg, bpg, wpp, g2, b2)
    return pl.pallas_call(
        functools.partial(_dense_ffn_kernel, alpha=alpha),
        grid=(t // tm,),
        in_specs=[pl.BlockSpec((tm, d), lambda i: (i, 0)),
                  pl.BlockSpec((None, tm, p.shape[2]), lambda i: (layer, i, 0))]
        + [_const_spec(c.shape) for c in consts],
        out_specs=pl.BlockSpec((tm, d), lambda i: (i, 0)),
        out_shape=jax.ShapeDtypeStruct((t, d), F32),
        compiler_params=pltpu.CompilerParams(
            dimension_semantics=("arbitrary",), vmem_limit_bytes=VMEM_LIMIT),
        name="dense_ffn",
    )(x, p, *consts)


ROW_CHUNKS = (512, 256, 128, 64, 32, 16, 8)


def _for_row_chunks(n, fn):
    off = 0
    for rows in ROW_CHUNKS:
        take = (n & rows) != 0
        pl.when(take)(functools.partial(fn, off, rows))
        off = off + jnp.where(take, rows, 0)


def _router_kernel(x_ref, wr_ref, br_ref, lower_ref, info_ref, cnt_ref):
    logits = _dot(x_ref[...].astype(BF16), wr_ref[...]) + br_ref[...]
    lane = lax.broadcasted_iota(jnp.int32, logits.shape, 1)
    m1 = jnp.max(logits, axis=-1, keepdims=True)
    i1 = jnp.min(jnp.where(logits == m1, lane, LANES), axis=-1, keepdims=True)
    rest = jnp.where(lane == i1, 2.0 * NEG_BIG, logits)
    m2 = jnp.max(rest, axis=-1, keepdims=True)
    i2 = jnp.min(jnp.where(rest == m2, lane, LANES), axis=-1, keepdims=True)
    e21 = jnp.exp(m2 - m1)
    w1 = 1.0 / (1.0 + e21)
    w2 = e21 / (1.0 + e21)

    hit1 = lane == i1
    hit2 = lane == i2
    onehot = jnp.where(hit1 | hit2, 1.0, 0.0)
    ranks = _dot(lower_ref[...], onehot.astype(BF16))
    r1 = jnp.sum(jnp.where(hit1, ranks, 0.0), axis=-1, keepdims=True)
    r2 = jnp.sum(jnp.where(hit2, ranks, 0.0), axis=-1, keepdims=True)

    cols = (i1.astype(F32), i2.astype(F32), r1, r2, w1, w2)
    info = jnp.zeros(logits.shape, F32)
    for k, v in enumerate(cols):
        info = jnp.where(lane == k, v, info)
    info_ref[...] = info
    cnt_ref[...] = jnp.broadcast_to(jnp.sum(onehot, axis=0, keepdims=True), cnt_ref.shape)


def _router_call(x, wr, br, *, tm):
    t, d = x.shape
    assert t % tm == 0 and tm < 2 ** 24
    lower = jnp.tril(jnp.ones((tm, tm), BF16), -1)
    return pl.pallas_call(
        _router_kernel,
        grid=(t // tm,),
        in_specs=[pl.BlockSpec((tm, d), lambda i: (i, 0)),
                  _const_spec(wr.shape), _const_spec(br.shape), _const_spec(lower.shape)],
        out_specs=[pl.BlockSpec((tm, LANES), lambda i: (i, 0)),
                   pl.BlockSpec((SUBLANES, LANES), lambda i: (i, 0))],
        out_shape=[jax.ShapeDtypeStruct((t, LANES), F32),
                   jax.ShapeDtypeStruct((t // tm * SUBLANES, LANES), F32)],
        compiler_params=pltpu.CompilerParams(
            dimension_semantics=("arbitrary",), vmem_limit_bytes=VMEM_LIMIT),
        name="router",
    )(x, wr, br, lower)


def _local_slot(expert, rank, loff_ref, base, n_exp):
    off = jnp.zeros(expert.shape, F32)
    for e in range(n_exp):
        off = jnp.where(expert == float(e), loff_ref[base + e].astype(F32), off)
    return (off + rank).astype(jnp.int32)


def _dispatch_kernel(n8_ref, loff_ref, gdst_ref, gap_ref, x_ref, info_ref, xs_ref,
                     comp, zbuf, sems, *, tm, n_exp):
    i = pl.program_id(0)
    n_steps = pl.num_programs(0)
    slot = i % 2
    nc = comp.shape[1]

    def group_copy(step, e, s, off, rows):
        src = comp.at[s, pl.ds(pl.multiple_of(loff_ref[step * n_exp + e] + off, SUBLANES), rows)]
        dst = xs_ref.at[pl.ds(pl.multiple_of(gdst_ref[step * n_exp + e] + off, SUBLANES), rows)]
        return pltpu.make_async_copy(src, dst, sems.at[s])

    def start_groups(step, s):
        for e in range(n_exp):
            _for_row_chunks(n8_ref[step * n_exp + e],
                            lambda off, rows, e=e: group_copy(step, e, s, off, rows).start())

    def wait_groups(step, s):
        for e in range(n_exp):
            _for_row_chunks(n8_ref[step * n_exp + e],
                            lambda off, rows, e=e: group_copy(step, e, s, off, rows).wait())

    info_t = info_ref[...].T
    slot_row = lax.broadcasted_iota(jnp.int32, (nc, tm), 0)
    hit = None
    for k in range(TOP_K):
        s_k = _local_slot(info_t[k:k + 1, :], info_t[TOP_K + k:TOP_K + k + 1, :],
                          loff_ref, i * n_exp, n_exp)
        hit = (slot_row == s_k) if hit is None else hit | (slot_row == s_k)
    onehot = jnp.where(hit, 1.0, 0.0).astype(BF16)
    comp[slot] = _dot(onehot, x_ref[...].astype(BF16))

    @pl.when(i > 0)
    def _():
        wait_groups(i - 1, 1 - slot)

    start_groups(i, slot)

    @pl.when(i == n_steps - 1)
    def _():
        wait_groups(i, slot)
        zbuf[...] = jnp.zeros(zbuf.shape, F32)

        def gap_copy(e, off, rows):
            dst = xs_ref.at[pl.ds(pl.multiple_of(gap_ref[2 * e] + off, SUBLANES), rows)]
            return pltpu.make_async_copy(zbuf.at[pl.ds(0, rows)], dst, sems.at[0])

        for e in range(n_exp):
            _for_row_chunks(gap_ref[2 * e + 1],
                            lambda off, rows, e=e: gap_copy(e, off, rows).start())
        for e in range(n_exp):
            _for_row_chunks(gap_ref[2 * e + 1],
                            lambda off, rows, e=e: gap_copy(e, off, rows).wait())

        tail_rows = zbuf.shape[0]

        def tail_copy(j):
            start = pl.multiple_of(gap_ref[2 * n_exp] + j * tail_rows, SUBLANES)
            return pltpu.make_async_copy(zbuf, xs_ref.at[pl.ds(start, tail_rows)], sems.at[0])

        def tail(j, c):
            tail_copy(j).start()
            tail_copy(j).wait()
            return c

        lax.fori_loop(0, gap_ref[2 * n_exp + 1], tail, 0)


def _dispatch_call(n8, loff, gdst, gap, x, info, *, tm, tme, n_exp, n_rows):
    t, d = x.shape
    nc = TOP_K * tm + SUBLANES * n_exp
    return pl.pallas_call(
        functools.partial(_dispatch_kernel, tm=tm, n_exp=n_exp),
        grid_spec=pltpu.PrefetchScalarGridSpec(
            num_scalar_prefetch=4,
            grid=(t // tm,),
            in_specs=[pl.BlockSpec((tm, d), lambda i, *_: (i, 0)),
                      pl.BlockSpec((tm, LANES), lambda i, *_: (i, 0))],
            out_specs=pl.BlockSpec(memory_space=pl.ANY),
            scratch_shapes=[pltpu.VMEM((2, nc, d), F32),
                            pltpu.VMEM((tme, d), F32),
                            pltpu.SemaphoreType.DMA((2,))],
        ),
        out_shape=jax.ShapeDtypeStruct((n_rows, d), F32),
        compiler_params=pltpu.CompilerParams(
            dimension_semantics=("arbitrary",), vmem_limit_bytes=VMEM_LIMIT),
        name="dispatch",
    )(n8, loff, gdst, gap, x, info)


def _expert_kernel(tile_e_ref, tile_n_ref, xs_ref, wg_ref, wu_ref, wd_ref, ys_ref, *, fc):
    del tile_e_ref
    i = pl.program_id(0)

    @pl.when(tile_n_ref[i] > 0)
    def _():
        xb = xs_ref[...].astype(BF16)
        f = wg_ref.shape[2]
        acc = jnp.zeros(ys_ref.shape, F32)
        for c in range(f // fc):
            cols = slice(c * fc, (c + 1) * fc)
            g = _dot(xb, wg_ref[0, :, cols])
            u = _dot(xb, wu_ref[0, :, cols])
            h = (g * _sigmoid(g) * u).astype(BF16)
            acc = acc + _dot(h, wd_ref[0, cols, :])
        ys_ref[...] = acc

    @pl.when(tile_n_ref[i] == 0)
    def _():
        ys_ref[...] = jnp.zeros(ys_ref.shape, F32)


def _expert_call(tile_e, tile_n, xs, wg, wu, wd, *, tme, fc):
    n_rows, d = xs.shape
    f = wg.shape[2]
    assert n_rows % tme == 0 and f % fc == 0 and fc % LANES == 0

    def w_spec(shape):
        return pl.BlockSpec((1,) + shape[1:], lambda i, te, tn: (te[i], 0, 0))

    return pl.pallas_call(
        functools.partial(_expert_kernel, fc=fc),
        grid_spec=pltpu.PrefetchScalarGridSpec(
            num_scalar_prefetch=2,
            grid=(n_rows // tme,),
            in_specs=[pl.BlockSpec((tme, d), lambda i, te, tn: (i, 0)),
                      w_spec(wg.shape), w_spec(wu.shape), w_spec(wd.shape)],
            out_specs=pl.BlockSpec((tme, d), lambda i, te, tn: (i, 0)),
        ),
        out_shape=jax.ShapeDtypeStruct((n_rows, d), F32),
        compiler_params=pltpu.CompilerParams(
            dimension_semantics=("arbitrary",), vmem_limit_bytes=VMEM_LIMIT),
        name="experts",
    )(tile_e, tile_n, xs, wg, wu, wd)


def _combine_kernel(n8_ref, loff_ref, gdst_ref, x_ref, p_ref, info_ref, ys_ref, wpg_ref, bpg_ref,
                    wpp_ref, g2_ref, b2_ref, o_ref, ycomp, sems, *, alpha, tm, n_exp):
    i = pl.program_id(0)
    n_steps = pl.num_programs(0)
    slot = i % 2
    nc = ycomp.shape[1]

    def group_copy(step, e, s, off, rows):
        src = ys_ref.at[pl.ds(pl.multiple_of(gdst_ref[step * n_exp + e] + off, SUBLANES), rows)]
        dst = ycomp.at[s, pl.ds(pl.multiple_of(loff_ref[step * n_exp + e] + off, SUBLANES), rows)]
        return pltpu.make_async_copy(src, dst, sems.at[s])

    def start_groups(step, s):
        for e in range(n_exp):
            _for_row_chunks(n8_ref[step * n_exp + e],
                            lambda off, rows, e=e: group_copy(step, e, s, off, rows).start())

    def wait_groups(step, s):
        for e in range(n_exp):
            _for_row_chunks(n8_ref[step * n_exp + e],
                            lambda off, rows, e=e: group_copy(step, e, s, off, rows).wait())

    @pl.when(i == 0)
    def _():
        ycomp[...] = jnp.zeros(ycomp.shape, F32)
        start_groups(0, 0)

    @pl.when(i + 1 < n_steps)
    def _():
        start_groups(i + 1, 1 - slot)

    x = x_ref[...]
    acc = alpha * x + _ple(x.astype(BF16), p_ref[...], wpg_ref, bpg_ref, wpp_ref)

    wait_groups(i, slot)
    info = info_ref[...]
    yb = ycomp[slot].astype(BF16)
    slot_col = lax.broadcasted_iota(jnp.int32, (tm, nc), 1)
    for k in range(TOP_K):
        s_k = _local_slot(info[:, k:k + 1], info[:, TOP_K + k:TOP_K + k + 1],
                          loff_ref, i * n_exp, n_exp)
        pick = jnp.where(slot_col == s_k, 1.0, 0.0).astype(BF16)
        acc = acc + info[:, 4 + k:5 + k] * _dot(pick, yb)
    o_ref[...] = _layer_norm(acc, g2_ref[...], b2_ref[...])


def _combine_call(n8, loff, gdst, x, p, layer, info, ys, wpg, bpg, wpp, g2, b2, *, alpha, tm, n_exp):
    t, d = x.shape
    nc = TOP_K * tm + SUBLANES * n_exp
    consts = (wpg, bpg, wpp, g2, b2)

    def const_spec(shape):
        nd = len(shape)
        return pl.BlockSpec(shape, lambda i, *_: (0,) * nd, pipeline_mode=pl.Buffered(1))

    return pl.pallas_call(
        functools.partial(_combine_kernel, alpha=alpha, tm=tm, n_exp=n_exp),
        grid_spec=pltpu.PrefetchScalarGridSpec(
            num_scalar_prefetch=3,
            grid=(t // tm,),
            in_specs=[pl.BlockSpec((tm, d), lambda i, *_: (i, 0)),
                      pl.BlockSpec((None, tm, p.shape[2]), lambda i, *_: (layer, i, 0)),
                      pl.BlockSpec((tm, LANES), lambda i, *_: (i, 0)),
                      pl.BlockSpec(memory_space=pl.ANY)]
            + [const_spec(c.shape) for c in consts],
            out_specs=pl.BlockSpec((tm, d), lambda i, *_: (i, 0)),
            scratch_shapes=[pltpu.VMEM((2, nc, d), F32), pltpu.SemaphoreType.DMA((2,))],
        ),
        out_shape=jax.ShapeDtypeStruct((t, d), F32),
        compiler_params=pltpu.CompilerParams(
            dimension_semantics=("arbitrary",), vmem_limit_bytes=VMEM_LIMIT),
        name="combine",
    )(n8, loff, gdst, x, p, info, ys, *consts)


def _moe_layer(x, p, layer, w_router, b_router, we_gate, we_up, we_down, wpg, bpg, wpp, g2, b2,
               *, alpha, tm, tme, fc):
    t, d = x.shape
    n_exp = w_router.shape[1]
    n_tok_tiles = t // tm
    assert n_exp <= LANES and tm <= ROW_CHUNKS[0] and tme <= ROW_CHUNKS[0]
    wr = jnp.zeros((d, LANES), BF16).at[:, :n_exp].set(w_router.astype(BF16))
    br = jnp.full((1, LANES), NEG_BIG, F32).at[0, :n_exp].set(b_router)
    info, cnt = _router_call(x, wr, br, tm=tm)

    i32 = jnp.int32
    n = cnt[::SUBLANES, :n_exp].astype(i32)
    n8 = (n + SUBLANES - 1) // SUBLANES * SUBLANES
    loff = jnp.cumsum(n8, axis=1) - n8
    tot = jnp.sum(n8, axis=0)
    reg = (tot + tme - 1) // tme * tme
    reg_end = jnp.cumsum(reg)
    base = reg_end - reg
    gdst = base[None, :] + jnp.cumsum(n8, axis=0) - n8
    n_tiles = (t * TOP_K + n_tok_tiles * n_exp * (SUBLANES - 1)) // tme + n_exp + 1
    n_rows = n_tiles * tme
    tile_start = jnp.arange(n_tiles, dtype=i32) * tme
    tile_e = jnp.minimum(jnp.sum(tile_start[:, None] >= reg_end[None, :], axis=1), n_exp - 1).astype(i32)
    tile_n = jnp.clip(tot[tile_e] - (tile_start - base[tile_e]), 0, tme).astype(i32)
    gap = jnp.stack([base + tot, reg - tot], axis=1).reshape(-1)
    gap = jnp.concatenate([gap, jnp.stack([reg_end[-1], (n_rows - reg_end[-1]) // tme])]).astype(i32)
    n8f, lofff, gdstf = (a.reshape(-1).astype(i32) for a in (n8, loff, gdst))

    xs = _dispatch_call(n8f, lofff, gdstf, gap, x, info, tm=tm, tme=tme, n_exp=n_exp, n_rows=n_rows)
    ys = _expert_call(tile_e, tile_n, xs, we_gate.astype(BF16), we_up.astype(BF16),
                      we_down.astype(BF16), tme=tme, fc=fc)
    return _combine_call(n8f, lofff, gdstf, x, p, layer, info, ys, wpg, bpg, wpp, g2, b2,
                         alpha=alpha, tm=tm, n_exp=n_exp)


def _row(v):
    return v.reshape(1, -1)


def kernel(x, p, w_in, b_in, conf_conv_w, conf_conv_b, conf_ln_g, conf_ln_b, w_conf_out,
           sc_conv_w, w_sc_out, w_o, ln1_g, ln1_b, w_ff_gate, w_ff_up, w_ff_down, w_router,
           b_router, we_gate, we_up, we_down, w_ple_gate, b_ple_gate, w_ple_proj, ln2_g, ln2_b,
           *, ts=512, tm=512, tme=512, fc_moe=512):
    depth = w_in.shape[0]
    alpha = (2 * depth) ** 0.25
    bsz, seq, d = x.shape
    pt = p.reshape(depth, bsz * seq, -1)
    for i in range(depth):
        x = _mixer_call(
            x, w_in[i].astype(BF16), _row(b_in[i]), conf_conv_w[i], _row(conf_conv_b[i]),
            _row(conf_ln_g[i]), _row(conf_ln_b[i]), w_conf_out[i].astype(BF16), sc_conv_w[i],
            w_sc_out[i].astype(BF16), w_o[i].astype(BF16), _row(ln1_g[i]), _row(ln1_b[i]),
            alpha=alpha, ts=ts)
        xt = x.reshape(bsz * seq, d)
        ple_w = (w_ple_gate[i].astype(BF16), _row(b_ple_gate[i]), w_ple_proj[i].astype(BF16),
                 _row(ln2_g[i]), _row(ln2_b[i]))
        j = i // 2
        if i % 2 == 0:
            xt = _dense_ffn_call(xt, pt, i, w_ff_gate[j].astype(BF16), w_ff_up[j].astype(BF16),
                                 w_ff_down[j].astype(BF16), *ple_w, alpha=alpha, tm=tm)
        else:
            xt = _moe_layer(xt, pt, i, w_router[j], b_router[j], we_gate[j], we_up[j], we_down[j],
                            *ple_w, alpha=alpha, tm=tm, tme=tme, fc=fc_moe)
        x = xt.reshape(bsz, seq, d)
    return x
```

```python
import functools

import jax
import jax.numpy as jnp
from jax import lax
from jax.experimental import pallas as pl
from jax.experimental.pallas import tpu as pltpu

LN_EPS = 1e-5
TOP_K = 2
LANES = 128
SUBLANES = 8
CONV_HALO = 32
SC_HALO = 8
MXU_COLS = 256
CONV_ROWS = 64
NEG_BIG = -1e30
VMEM_LIMIT = 56 * 1024 * 1024

F32 = jnp.float32
BF16 = jnp.bfloat16


def _dot(a, b):
    return jnp.dot(a, b, preferred_element_type=F32)


def _sigmoid(x):
    return 1.0 / (1.0 + jnp.exp(-x))


def _layer_norm(x, g, b):
    mu = jnp.mean(x, axis=-1, keepdims=True)
    xc = x - mu
    var = jnp.mean(xc * xc, axis=-1, keepdims=True)
    return xc * lax.rsqrt(var + LN_EPS) * g + b


def _const_spec(shape):
    nd = len(shape)
    return pl.BlockSpec(shape, lambda *_: (0,) * nd, pipeline_mode=pl.Buffered(1))


def _causal_dwconv_chunk(src_ref, row0, halo, w_ref, taps, lanes, init):
    acc = init
    first = halo - (taps - 1)
    for r in range(SUBLANES):
        offs = [o for o in range(first, halo + 1) if o % SUBLANES == r]
        if not offs:
            continue
        base = offs[0]
        span = offs[-1] - base + CONV_ROWS
        blk = src_ref[row0 + base:row0 + base + span, lanes]
        part = None
        for o in offs:
            j = o - first
            term = w_ref[j:j + 1, lanes] * blk[o - base:o - base + CONV_ROWS, :]
            part = term if part is None else part + term
        acc = acc + part
    return acc


def _mixer_kernel(x_ref, w_in_ref, b_in_ref, ccw_ref, ccb_ref, clg_ref, clb_ref, wa_ref,
                  scw_ref, wb_ref, wo_ref, g1_ref, b1_ref, o_ref,
                  a_ext, u_ext, scb_buf, act_a, act_b, zg_buf, *, alpha, ts, dc, ds, kc, ks):
    s = pl.program_id(1)

    @pl.when(s == 0)
    def _():
        a_ext[0:CONV_HALO, :] = jnp.zeros((CONV_HALO, dc), F32)
        u_ext[0:SC_HALO, :] = jnp.zeros((SC_HALO, ds), F32)

    @pl.when(s > 0)
    def _():
        a_ext[0:CONV_HALO, :] = a_ext[ts:ts + CONV_HALO, :]
        u_ext[0:SC_HALO, :] = u_ext[ts:ts + SC_HALO, :]

    x = x_ref[0]
    xb = x.astype(BF16)

    c0 = 2 * dc
    c1 = c0 + 3 * ds
    zc = _dot(xb, w_in_ref[:, 0:c0]) + b_in_ref[:, 0:c0]
    a_ext[CONV_HALO:CONV_HALO + ts, :] = zc[:, :dc] * _sigmoid(zc[:, dc:])
    zs = _dot(xb, w_in_ref[:, c0:c1]) + b_in_ref[:, c0:c1]
    scb_buf[...] = zs[:, :ds]
    u_ext[SC_HALO:SC_HALO + ts, :] = zs[:, ds:2 * ds] * zs[:, 2 * ds:]

    d = x.shape[-1]
    n_piece = 2 * d // MXU_COLS
    units = (ts // CONV_ROWS) * (dc // LANES)
    zg_pieces = []

    def gate_piece(j):
        cols = slice(c1 + j * MXU_COLS, c1 + (j + 1) * MXU_COLS)
        z = _dot(xb, w_in_ref[:, cols]) + b_in_ref[:, cols]
        zg_buf[:, j * MXU_COLS:(j + 1) * MXU_COLS] = z
        zg_pieces.append(z[0:SUBLANES, 0:LANES])

    def after(piece):
        bits = pltpu.bitcast(piece, jnp.uint32)
        return pltpu.bitcast((bits >> 16) >> 16, F32)

    u = 0
    for ci in range(ts // CONV_ROWS):
        row0 = ci * CONV_ROWS
        conv = []
        for cg in range(dc // LANES):
            lanes = slice(cg * LANES, (cg + 1) * LANES)
            want = min(n_piece, -(-(u + 1) * n_piece // units))
            while len(zg_pieces) < want:
                gate_piece(len(zg_pieces))
            u += 1
            init = jnp.broadcast_to(ccb_ref[:, lanes], (CONV_ROWS, LANES))
            init = init + jnp.tile(after(zg_pieces[-1]), (CONV_ROWS // SUBLANES, 1))
            conv.append(_causal_dwconv_chunk(a_ext, row0, CONV_HALO, ccw_ref, kc, lanes, init))
        conv = jnp.concatenate(conv, axis=-1)
        ln = _layer_norm(conv, clg_ref[...], clb_ref[...])
        act_a[row0:row0 + CONV_ROWS, :] = (ln * _sigmoid(ln)).astype(BF16)
        sconv = []
        for cg in range(ds // LANES):
            lanes = slice(cg * LANES, (cg + 1) * LANES)
            init = jnp.zeros((CONV_ROWS, LANES), F32)
            sconv.append(_causal_dwconv_chunk(u_ext, row0, SC_HALO, scw_ref, ks, lanes, init))
        sconv = jnp.concatenate(sconv, axis=-1)
        act_b[row0:row0 + CONV_ROWS, :] = (scb_buf[row0:row0 + CONV_ROWS, :] * sconv).astype(BF16)
    zg = zg_buf[...]

    y_a = _dot(act_a[...], wa_ref[...])
    y_b = _dot(act_b[...], wb_ref[...])
    m = _sigmoid(zg[:, :d]) * y_a + _sigmoid(zg[:, d:]) * y_b
    mix = _dot(m.astype(BF16), wo_ref[...])
    o_ref[0] = _layer_norm(alpha * x + mix, g1_ref[...], b1_ref[...])


def _mixer_call(x, w_in, b_in, ccw, ccb, clg, clb, wa, scw, wb, wo, g1, b1, *, alpha, ts):
    bsz, seq, d = x.shape
    kc, dc = ccw.shape
    ks, ds = scw.shape
    assert seq % ts == 0 and ts % CONV_ROWS == 0 and ts >= CONV_HALO
    assert kc - 1 <= CONV_HALO and ks - 1 <= SC_HALO
    assert dc % LANES == 0 and ds % LANES == 0 and (2 * d) % MXU_COLS == 0
    kern = functools.partial(_mixer_kernel, alpha=alpha, ts=ts, dc=dc, ds=ds, kc=kc, ks=ks)
    consts = (w_in, b_in, ccw, ccb, clg, clb, wa, scw, wb, wo, g1, b1)
    return pl.pallas_call(
        kern,
        grid=(bsz, seq // ts),
        in_specs=[pl.BlockSpec((1, ts, d), lambda b, s: (b, s, 0))]
        + [_const_spec(c.shape) for c in consts],
        out_specs=pl.BlockSpec((1, ts, d), lambda b, s: (b, s, 0)),
        out_shape=jax.ShapeDtypeStruct((bsz, seq, d), F32),
        scratch_shapes=[
            pltpu.VMEM((ts + CONV_HALO, dc), F32),
            pltpu.VMEM((ts + SC_HALO, ds), F32),
            pltpu.VMEM((ts, ds), F32),
            pltpu.VMEM((ts, dc), BF16),
            pltpu.VMEM((ts, ds), BF16),
            pltpu.VMEM((ts, 2 * d), F32),
        ],
        compiler_params=pltpu.CompilerParams(
            dimension_semantics=("arbitrary", "arbitrary"), vmem_limit_bytes=VMEM_LIMIT),
        name="mixer",
    )(x, *consts)


def _ple(xb, p, wpg_ref, bpg_ref, wpp_ref):
    gate = _sigmoid(_dot(xb, wpg_ref[...]) + bpg_ref[...])
    return gate * _dot(p.astype(BF16), wpp_ref[...])


def _dense_ffn_kernel(x_ref, p_ref, wg_ref, wu_ref, wd_ref, wpg_ref, bpg_ref, wpp_ref,
                      g2_ref, b2_ref, o_ref, *, alpha):
    x = x_ref[...]
    xb = x.astype(BF16)
    acc = alpha * x + _ple(xb, p_ref[...], wpg_ref, bpg_ref, wpp_ref)
    g = _dot(xb, wg_ref[...])
    u = _dot(xb, wu_ref[...])
    h = (g * _sigmoid(g) * u).astype(BF16)
    acc = acc + _dot(h, wd_ref[...])
    o_ref[...] = _layer_norm(acc, g2_ref[...], b2_ref[...])


def _dense_ffn_call(x, p, layer, wg, wu, wd, wpg, bpg, wpp, g2, b2, *, alpha, tm):
    t, d = x.shape
    assert t % tm == 0
    consts = (wg, wu, wd, wpg, bpg, wpp, g2, b2)
    return pl.pallas_call(
        functools.partial(_dense_ffn_kernel, alpha=alpha),
        grid=(t // tm,),
        in_specs=[pl.BlockSpec((tm, d), lambda i: (i, 0)),
                  pl.BlockSpec((None, tm, p.shape[2]), lambda i: (layer, i, 0))]
        + [_const_spec(c.shape) for c in consts],
        out_specs=pl.BlockSpec((tm, d), lambda i: (i, 0)),
        out_shape=jax.ShapeDtypeStruct((t, d), F32),
        compiler_params=pltpu.CompilerParams(
            dimension_semantics=("arbitrary",), vmem_limit_bytes=VMEM_LIMIT),
        name="dense_ffn",
    )(x, p, *consts)


ROW_CHUNKS = (512, 256, 128, 64, 32, 16, 8)


def _for_row_chunks(n, fn):
    off = 0
    for rows in ROW_CHUNKS:
        take = (n & rows) != 0
        pl.when(take)(functools.partial(fn, off, rows))
        off = off + jnp.where(take, rows, 0)


TOTAL_CHUNKS = (1024,) + ROW_CHUNKS


def _wait_rows(n, wait_fn):
    for rows in TOTAL_CHUNKS:
        pl.when((n & rows) != 0)(functools.partial(wait_fn, rows))


def _router_kernel(x_ref, wr_ref, br_ref, lower_ref, info_ref, cnt_ref):
    logits = _dot(x_ref[...].astype(BF16), wr_ref[...]) + br_ref[...]
    lane = lax.broadcasted_iota(jnp.int32, logits.shape, 1)
    m1 = jnp.max(logits, axis=-1, keepdims=True)
    i1 = jnp.min(jnp.where(logits == m1, lane, LANES), axis=-1, keepdims=True)
    rest = jnp.where(lane == i1, 2.0 * NEG_BIG, logits)
    m2 = jnp.max(rest, axis=-1, keepdims=True)
    i2 = jnp.min(jnp.where(rest == m2, lane, LANES), axis=-1, keepdims=True)
    e21 = jnp.exp(m2 - m1)
    w1 = 1.0 / (1.0 + e21)
    w2 = e21 / (1.0 + e21)

    hit1 = lane == i1
    hit2 = lane == i2
    onehot = jnp.where(hit1 | hit2, 1.0, 0.0)
    ranks = _dot(lower_ref[...], onehot.astype(BF16))
    r1 = jnp.sum(jnp.where(hit1, ranks, 0.0), axis=-1, keepdims=True)
    r2 = jnp.sum(jnp.where(hit2, ranks, 0.0), axis=-1, keepdims=True)

    cols = (i1.astype(F32), i2.astype(F32), r1, r2, w1, w2)
    info = jnp.zeros(logits.shape, F32)
    for k, v in enumerate(cols):
        info = jnp.where(lane == k, v, info)
    info_ref[...] = info
    cnt_ref[...] = jnp.broadcast_to(jnp.sum(onehot, axis=0, keepdims=True), cnt_ref.shape)


def _router_call(x, wr, br, *, tm):
    t, d = x.shape
    assert t % tm == 0 and tm < 2 ** 24
    lower = jnp.tril(jnp.ones((tm, tm), BF16), -1)
    return pl.pallas_call(
        _router_kernel,
        grid=(t // tm,),
        in_specs=[pl.BlockSpec((tm, d), lambda i: (i, 0)),
                  _const_spec(wr.shape), _const_spec(br.shape), _const_spec(lower.shape)],
        out_specs=[pl.BlockSpec((tm, LANES), lambda i: (i, 0)),
                   pl.BlockSpec((SUBLANES, LANES), lambda i: (i, 0))],
        out_shape=[jax.ShapeDtypeStruct((t, LANES), F32),
                   jax.ShapeDtypeStruct((t // tm * SUBLANES, LANES), F32)],
        compiler_params=pltpu.CompilerParams(
            dimension_semantics=("arbitrary",), vmem_limit_bytes=VMEM_LIMIT),
        name="router",
    )(x, wr, br, lower)


def _local_slot(expert, rank, loff_ref, base, n_exp):
    off = jnp.zeros(expert.shape, F32)
    for e in range(n_exp):
        off = jnp.where(expert == float(e), loff_ref[base + e].astype(F32), off)
    return (off + rank).astype(jnp.int32)


def _dispatch_kernel(n8_ref, loff_ref, gdst_ref, gap_ref, x_ref, info_ref, xs_ref,
                     comp, zbuf, sems, *, tm, n_exp):
    i = pl.program_id(0)
    n_steps = pl.num_programs(0)
    slot = i % 2
    nc = comp.shape[1]

    def group_copy(step, e, s, off, rows):
        src = comp.at[s, pl.ds(pl.multiple_of(loff_ref[step * n_exp + e] + off, SUBLANES), rows)]
        dst = xs_ref.at[pl.ds(pl.multiple_of(gdst_ref[step * n_exp + e] + off, SUBLANES), rows)]
        return pltpu.make_async_copy(src, dst, sems.at[s])

    def start_groups(step, s):
        for e in range(n_exp):
            _for_row_chunks(n8_ref[step * n_exp + e],
                            lambda off, rows, e=e: group_copy(step, e, s, off, rows).start())

    def wait_groups(step, s):
        total = n8_ref[step * n_exp]
        for e in range(1, n_exp):
            total = total + n8_ref[step * n_exp + e]
        _wait_rows(total, lambda rows: pltpu.make_async_copy(
            comp.at[s, pl.ds(0, rows)], xs_ref.at[pl.ds(0, rows)], sems.at[s]).wait())

    info_t = info_ref[...].T
    slot_row = lax.broadcasted_iota(jnp.int32, (nc, tm), 0)
    hit = None
    for k in range(TOP_K):
        s_k = _local_slot(info_t[k:k + 1, :], info_t[TOP_K + k:TOP_K + k + 1, :],
                          loff_ref, i * n_exp, n_exp)
        hit = (slot_row == s_k) if hit is None else hit | (slot_row == s_k)
    onehot = jnp.where(hit, 1.0, 0.0).astype(BF16)
    comp[slot] = _dot(onehot, x_ref[...].astype(BF16))

    @pl.when(i > 0)
    def _():
        wait_groups(i - 1, 1 - slot)

    start_groups(i, slot)

    @pl.when(i == n_steps - 1)
    def _():
        wait_groups(i, slot)
        zbuf[...] = jnp.zeros(zbuf.shape, F32)

        def gap_copy(e, off, rows):
            dst = xs_ref.at[pl.ds(pl.multiple_of(gap_ref[2 * e] + off, SUBLANES), rows)]
            return pltpu.make_async_copy(zbuf.at[pl.ds(0, rows)], dst, sems.at[0])

        for e in range(n_exp):
            _for_row_chunks(gap_ref[2 * e + 1],
                            lambda off, rows, e=e: gap_copy(e, off, rows).start())
        for e in range(n_exp):
            _for_row_chunks(gap_ref[2 * e + 1],
                            lambda off, rows, e=e: gap_copy(e, off, rows).wait())

        tail_rows = zbuf.shape[0]

        def tail_copy(j):
            start = pl.multiple_of(gap_ref[2 * n_exp] + j * tail_rows, SUBLANES)
            return pltpu.make_async_copy(zbuf, xs_ref.at[pl.ds(start, tail_rows)], sems.at[0])

        def tail(j, c):
            tail_copy(j).start()
            tail_copy(j).wait()
            return c

        lax.fori_loop(0, gap_ref[2 * n_exp + 1], tail, 0)


def _dispatch_call(n8, loff, gdst, gap, x, info, *, tm, tme, n_exp, n_rows):
    t, d = x.shape
    nc = TOP_K * tm + SUBLANES * n_exp
    return pl.pallas_call(
        functools.partial(_dispatch_kernel, tm=tm, n_exp=n_exp),
        grid_spec=pltpu.PrefetchScalarGridSpec(
            num_scalar_prefetch=4,
            grid=(t // tm,),
            in_specs=[pl.BlockSpec((tm, d), lambda i, *_: (i, 0)),
                      pl.BlockSpec((tm, LANES), lambda i, *_: (i, 0))],
            out_specs=pl.BlockSpec(memory_space=pl.ANY),
            scratch_shapes=[pltpu.VMEM((2, nc, d), F32),
                            pltpu.VMEM((tme, d), F32),
                            pltpu.SemaphoreType.DMA((2,))],
        ),
        out_shape=jax.ShapeDtypeStruct((n_rows, d), F32),
        compiler_params=pltpu.CompilerParams(
            dimension_semantics=("arbitrary",), vmem_limit_bytes=VMEM_LIMIT),
        name="dispatch",
    )(n8, loff, gdst, gap, x, info)


def _expert_kernel(tile_e_ref, tile_n_ref, xs_ref, wg_ref, wu_ref, wd_ref, ys_ref, *, fc):
    del tile_e_ref
    i = pl.program_id(0)

    @pl.when(tile_n_ref[i] > 0)
    def _():
        xb = xs_ref[...].astype(BF16)
        f = wg_ref.shape[2]
        acc = jnp.zeros(ys_ref.shape, F32)
        for c in range(f // fc):
            cols = slice(c * fc, (c + 1) * fc)
            g = _dot(xb, wg_ref[0, :, cols])
            u = _dot(xb, wu_ref[0, :, cols])
            h = (g * _sigmoid(g) * u).astype(BF16)
            acc = acc + _dot(h, wd_ref[0, cols, :])
        ys_ref[...] = acc

    @pl.when(tile_n_ref[i] == 0)
    def _():
        ys_ref[...] = jnp.zeros(ys_ref.shape, F32)


def _expert_call(tile_e, tile_n, xs, wg, wu, wd, *, tme, fc):
    n_rows, d = xs.shape
    f = wg.shape[2]
    assert n_rows % tme == 0 and f % fc == 0 and fc % LANES == 0

    def w_spec(shape):
        return pl.BlockSpec((1,) + shape[1:], lambda i, te, tn: (te[i], 0, 0))

    return pl.pallas_call(
        functools.partial(_expert_kernel, fc=fc),
        grid_spec=pltpu.PrefetchScalarGridSpec(
            num_scalar_prefetch=2,
            grid=(n_rows // tme,),
            in_specs=[pl.BlockSpec((tme, d), lambda i, te, tn: (i, 0)),
                      w_spec(wg.shape), w_spec(wu.shape), w_spec(wd.shape)],
            out_specs=pl.BlockSpec((tme, d), lambda i, te, tn: (i, 0)),
        ),
        out_shape=jax.ShapeDtypeStruct((n_rows, d), F32),
        compiler_params=pltpu.CompilerParams(
            dimension_semantics=("arbitrary",), vmem_limit_bytes=VMEM_LIMIT),
        name="experts",
    )(tile_e, tile_n, xs, wg, wu, wd)


def _combine_kernel(n8_ref, loff_ref, gdst_ref, x_ref, p_ref, info_ref, ys_ref, wpg_ref, bpg_ref,
                    wpp_ref, g2_ref, b2_ref, o_ref, ycomp, sems, *, alpha, tm, n_exp):
    i = pl.program_id(0)
    n_steps = pl.num_programs(0)
    slot = i % 2
    nc = ycomp.shape[1]

    def group_copy(step, e, s, off, rows):
        src = ys_ref.at[pl.ds(pl.multiple_of(gdst_ref[step * n_exp + e] + off, SUBLANES), rows)]
        dst = ycomp.at[s, pl.ds(pl.multiple_of(loff_ref[step * n_exp + e] + off, SUBLANES), rows)]
        return pltpu.make_async_copy(src, dst, sems.at[s])

    def start_groups(step, s):
        for e in range(n_exp):
            _for_row_chunks(n8_ref[step * n_exp + e],
                            lambda off, rows, e=e: group_copy(step, e, s, off, rows).start())

    def wait_groups(step, s):
        total = n8_ref[step * n_exp]
        for e in range(1, n_exp):
            total = total + n8_ref[step * n_exp + e]
        _wait_rows(total, lambda rows: pltpu.make_async_copy(
            ys_ref.at[pl.ds(0, rows)], ycomp.at[s, pl.ds(0, rows)], sems.at[s]).wait())

    @pl.when(i == 0)
    def _():
        ycomp[...] = jnp.zeros(ycomp.shape, F32)
        start_groups(0, 0)

    @pl.when(i + 1 < n_steps)
    def _():
        start_groups(i + 1, 1 - slot)

    x = x_ref[...]
    acc = alpha * x + _ple(x.astype(BF16), p_ref[...], wpg_ref, bpg_ref, wpp_ref)

    wait_groups(i, slot)
    info = info_ref[...]
    yb = ycomp[slot].astype(BF16)
    slot_col = lax.broadcasted_iota(jnp.int32, (tm, nc), 1)
    pick = jnp.zeros((tm, nc), F32)
    for k in range(TOP_K):
        s_k = _local_slot(info[:, k:k + 1], info[:, TOP_K + k:TOP_K + k + 1],
                          loff_ref, i * n_exp, n_exp)
        pick = jnp.where(slot_col == s_k, info[:, 4 + k:5 + k], pick)
    acc = acc + _dot(pick.astype(BF16), yb)
    o_ref[...] = _layer_norm(acc, g2_ref[...], b2_ref[...])


def _combine_call(n8, loff, gdst, x, p, layer, info, ys, wpg, bpg, wpp, g2, b2, *, alpha, tm, n_exp):
    t, d = x.shape
    nc = TOP_K * tm + SUBLANES * n_exp
    consts = (wpg, bpg, wpp, g2, b2)

    def const_spec(shape):
        nd = len(shape)
        return pl.BlockSpec(shape, lambda i, *_: (0,) * nd, pipeline_mode=pl.Buffered(1))

    return pl.pallas_call(
        functools.partial(_combine_kernel, alpha=alpha, tm=tm, n_exp=n_exp),
        grid_spec=pltpu.PrefetchScalarGridSpec(
            num_scalar_prefetch=3,
            grid=(t // tm,),
            in_specs=[pl.BlockSpec((tm, d), lambda i, *_: (i, 0)),
                      pl.BlockSpec((None, tm, p.shape[2]), lambda i, *_: (layer, i, 0)),
                      pl.BlockSpec((tm, LANES), lambda i, *_: (i, 0)),
                      pl.BlockSpec(memory_space=pl.ANY)]
            + [const_spec(c.shape) for c in consts],
            out_specs=pl.BlockSpec((tm, d), lambda i, *_: (i, 0)),
            scratch_shapes=[pltpu.VMEM((2, nc, d), F32), pltpu.SemaphoreType.DMA((2,))],
        ),
        out_shape=jax.ShapeDtypeStruct((t, d), F32),
        compiler_params=pltpu.CompilerParams(
            dimension_semantics=("arbitrary",), vmem_limit_bytes=VMEM_LIMIT),
        name="combine",
    )(n8, loff, gdst, x, p, info, ys, *consts)


def _moe_layer(x, p, layer, w_router, b_router, we_gate, we_up, we_down, wpg, bpg, wpp, g2, b2,
               *, alpha, tm, tme, fc):
    t, d = x.shape
    n_exp = w_router.shape[1]
    n_tok_tiles = t // tm
    assert n_exp <= LANES and tm <= ROW_CHUNKS[0] and tme <= ROW_CHUNKS[0]
    wr = jnp.zeros((d, LANES), BF16).at[:, :n_exp].set(w_router.astype(BF16))
    br = jnp.full((1, LANES), NEG_BIG, F32).at[0, :n_exp].set(b_router)
    info, cnt = _router_call(x, wr, br, tm=tm)

    i32 = jnp.int32
    n = cnt[::SUBLANES, :n_exp].astype(i32)
    n8 = (n + SUBLANES - 1) // SUBLANES * SUBLANES
    loff = jnp.cumsum(n8, axis=1) - n8
    tot = jnp.sum(n8, axis=0)
    reg = (tot + tme - 1) // tme * tme
    reg_end = jnp.cumsum(reg)
    base = reg_end - reg
    gdst = base[None, :] + jnp.cumsum(n8, axis=0) - n8
    n_tiles = (t * TOP_K + n_tok_tiles * n_exp * (SUBLANES - 1)) // tme + n_exp + 1
    n_rows = n_tiles * tme
    tile_start = jnp.arange(n_tiles, dtype=i32) * tme
    tile_e = jnp.minimum(jnp.sum(tile_start[:, None] >= reg_end[None, :], axis=1), n_exp - 1).astype(i32)
    tile_n = jnp.clip(tot[tile_e] - (tile_start - base[tile_e]), 0, tme).astype(i32)
    gap = jnp.stack([base + tot, reg - tot], axis=1).reshape(-1)
    gap = jnp.concatenate([gap, jnp.stack([reg_end[-1], (n_rows - reg_end[-1]) // tme])]).astype(i32)
    n8f, lofff, gdstf = (a.reshape(-1).astype(i32) for a in (n8, loff, gdst))

    xs = _dispatch_call(n8f, lofff, gdstf, gap, x, info, tm=tm, tme=tme, n_exp=n_exp, n_rows=n_rows)
    ys = _expert_call(tile_e, tile_n, xs, we_gate.astype(BF16), we_up.astype(BF16),
                      we_down.astype(BF16), tme=tme, fc=fc)
    return _combine_call(n8f, lofff, gdstf, x, p, layer, info, ys, wpg, bpg, wpp, g2, b2,
                         alpha=alpha, tm=tm, n_exp=n_exp)


def _row(v):
    return v.reshape(1, -1)


def kernel(x, p, w_in, b_in, conf_conv_w, conf_conv_b, conf_ln_g, conf_ln_b, w_conf_out,
           sc_conv_w, w_sc_out, w_o, ln1_g, ln1_b, w_ff_gate, w_ff_up, w_ff_down, w_router,
           b_router, we_gate, we_up, we_down, w_ple_gate, b_ple_gate, w_ple_proj, ln2_g, ln2_b,
           *, ts=512, tm=512, tme=512, fc_moe=512):
    depth = w_in.shape[0]
    alpha = (2 * depth) ** 0.25
    bsz, seq, d = x.shape
    pt = p.reshape(depth, bsz * seq, -1)
    for i in range(depth):
        x = _mixer_call(
            x, w_in[i].astype(BF16), _row(b_in[i]), conf_conv_w[i], _row(conf_conv_b[i]),
            _row(conf_ln_g[i]), _row(conf_ln_b[i]), w_conf_out[i].astype(BF16), sc_conv_w[i],
            w_sc_out[i].astype(BF16), w_o[i].astype(BF16), _row(ln1_g[i]), _row(ln1_b[i]),
            alpha=alpha, ts=ts)
        xt = x.reshape(bsz * seq, d)
        ple_w = (w_ple_gate[i].astype(BF16), _row(b_ple_gate[i]), w_ple_proj[i].astype(BF16),
                 _row(ln2_g[i]), _row(ln2_b[i]))
        j = i // 2
        if i % 2 == 0:
            xt = _dense_ffn_call(xt, pt, i, w_ff_gate[j].astype(BF16), w_ff_up[j].astype(BF16),
                                 w_ff_down[j].astype(BF16), *ple_w, alpha=alpha, tm=tm)
        else:
            xt = _moe_layer(xt, pt, i, w_router[j], b_router[j], we_gate[j], we_up[j], we_down[j],
                            *ple_w, alpha=alpha, tm=tm, tme=tme, fc=fc_moe)
        x = xt.reshape(bsz, seq, d)
    return x
```

```python
import functools

import jax
import jax.numpy as jnp
from jax import lax
from jax.experimental import pallas as pl
from jax.experimental.pallas import tpu as pltpu

LN_EPS = 1e-5
TOP_K = 2
LANES = 128
SUBLANES = 8
CONV_HALO = 32
SC_HALO = 8
MXU_COLS = 256
CONV_ROWS = 64
NEG_BIG = -1e30
VMEM_LIMIT = 56 * 1024 * 1024

F32 = jnp.float32
BF16 = jnp.bfloat16


def _dot(a, b):
    return jnp.dot(a, b, preferred_element_type=F32)


def _sigmoid(x):
    return 1.0 / (1.0 + jnp.exp(-x))


def _layer_norm(x, g, b):
    mu = jnp.mean(x, axis=-1, keepdims=True)
    xc = x - mu
    var = jnp.mean(xc * xc, axis=-1, keepdims=True)
    return xc * lax.rsqrt(var + LN_EPS) * g + b


def _const_spec(shape):
    nd = len(shape)
    return pl.BlockSpec(shape, lambda *_: (0,) * nd, pipeline_mode=pl.Buffered(1))


def _side_cast_specs(side, n_steps, step_index):
    rows, cols = side.shape
    assert rows % (n_steps * 2 * SUBLANES) == 0
    block = (rows // n_steps, cols)
    spec = pl.BlockSpec(block, lambda *idx: (step_index(*idx), 0))
    return spec, spec, jax.ShapeDtypeStruct(side.shape, BF16)


def _causal_dwconv_chunk(src_ref, row0, halo, w_ref, taps, lanes, init):
    acc = init
    first = halo - (taps - 1)
    for r in range(SUBLANES):
        offs = [o for o in range(first, halo + 1) if o % SUBLANES == r]
        if not offs:
            continue
        base = offs[0]
        span = offs[-1] - base + CONV_ROWS
        blk = src_ref[row0 + base:row0 + base + span, lanes]
        part = None
        for o in offs:
            j = o - first
            term = w_ref[j:j + 1, lanes] * blk[o - base:o - base + CONV_ROWS, :]
            part = term if part is None else part + term
        acc = acc + part
    return acc


def _mixer_kernel(x_ref, w_in_ref, b_in_ref, ccw_ref, ccb_ref, clg_ref, clb_ref, wa_ref,
                  scw_ref, wb_ref, wo_ref, g1_ref, b1_ref, side_ref, o_ref, side_out_ref,
                  a_ext, u_ext, scb_buf, act_a, act_b, zg_buf, *, alpha, ts, dc, ds, kc, ks):
    s = pl.program_id(1)
    side_out_ref[...] = side_ref[...].astype(BF16)

    @pl.when(s == 0)
    def _():
        a_ext[0:CONV_HALO, :] = jnp.zeros((CONV_HALO, dc), F32)
        u_ext[0:SC_HALO, :] = jnp.zeros((SC_HALO, ds), F32)

    @pl.when(s > 0)
    def _():
        a_ext[0:CONV_HALO, :] = a_ext[ts:ts + CONV_HALO, :]
        u_ext[0:SC_HALO, :] = u_ext[ts:ts + SC_HALO, :]

    x = x_ref[0]
    xb = x.astype(BF16)

    c0 = 2 * dc
    c1 = c0 + 3 * ds
    zc = _dot(xb, w_in_ref[:, 0:c0]) + b_in_ref[:, 0:c0]
    a_ext[CONV_HALO:CONV_HALO + ts, :] = zc[:, :dc] * _sigmoid(zc[:, dc:])
    zs = _dot(xb, w_in_ref[:, c0:c1]) + b_in_ref[:, c0:c1]
    scb_buf[...] = zs[:, :ds]
    u_ext[SC_HALO:SC_HALO + ts, :] = zs[:, ds:2 * ds] * zs[:, 2 * ds:]

    d = x.shape[-1]
    n_piece = 2 * d // MXU_COLS
    units = (ts // CONV_ROWS) * (dc // LANES)
    zg_pieces = []

    def gate_piece(j):
        cols = slice(c1 + j * MXU_COLS, c1 + (j + 1) * MXU_COLS)
        z = _dot(xb, w_in_ref[:, cols]) + b_in_ref[:, cols]
        zg_buf[:, j * MXU_COLS:(j + 1) * MXU_COLS] = z
        zg_pieces.append(z[0:SUBLANES, 0:LANES])

    def after(piece):
        bits = pltpu.bitcast(piece, jnp.uint32)
        return pltpu.bitcast((bits >> 16) >> 16, F32)

    u = 0
    for ci in range(ts // CONV_ROWS):
        row0 = ci * CONV_ROWS
        conv = []
        for cg in range(dc // LANES):
            lanes = slice(cg * LANES, (cg + 1) * LANES)
            want = min(n_piece, -(-(u + 1) * n_piece // units))
            while len(zg_pieces) < want:
                gate_piece(len(zg_pieces))
            u += 1
            init = jnp.broadcast_to(ccb_ref[:, lanes], (CONV_ROWS, LANES))
            init = init + jnp.tile(after(zg_pieces[-1]), (CONV_ROWS // SUBLANES, 1))
            conv.append(_causal_dwconv_chunk(a_ext, row0, CONV_HALO, ccw_ref, kc, lanes, init))
        conv = jnp.concatenate(conv, axis=-1)
        ln = _layer_norm(conv, clg_ref[...], clb_ref[...])
        act_a[row0:row0 + CONV_ROWS, :] = (ln * _sigmoid(ln)).astype(BF16)
        sconv = []
        for cg in range(ds // LANES):
            lanes = slice(cg * LANES, (cg + 1) * LANES)
            init = jnp.zeros((CONV_ROWS, LANES), F32)
            sconv.append(_causal_dwconv_chunk(u_ext, row0, SC_HALO, scw_ref, ks, lanes, init))
        sconv = jnp.concatenate(sconv, axis=-1)
        act_b[row0:row0 + CONV_ROWS, :] = (scb_buf[row0:row0 + CONV_ROWS, :] * sconv).astype(BF16)
    zg = zg_buf[...]

    y_a = _dot(act_a[...], wa_ref[...])
    y_b = _dot(act_b[...], wb_ref[...])
    m = _sigmoid(zg[:, :d]) * y_a + _sigmoid(zg[:, d:]) * y_b
    mix = _dot(m.astype(BF16), wo_ref[...])
    o_ref[0] = _layer_norm(alpha * x + mix, g1_ref[...], b1_ref[...])


def _mixer_call(x, w_in, b_in, ccw, ccb, clg, clb, wa, scw, wb, wo, g1, b1, side, *, alpha, ts):
    bsz, seq, d = x.shape
    kc, dc = ccw.shape
    ks, ds = scw.shape
    assert seq % ts == 0 and ts % CONV_ROWS == 0 and ts >= CONV_HALO
    assert kc - 1 <= CONV_HALO and ks - 1 <= SC_HALO
    assert dc % LANES == 0 and ds % LANES == 0 and (2 * d) % MXU_COLS == 0
    kern = functools.partial(_mixer_kernel, alpha=alpha, ts=ts, dc=dc, ds=ds, kc=kc, ks=ks)
    consts = (w_in, b_in, ccw, ccb, clg, clb, wa, scw, wb, wo, g1, b1)
    tiles = seq // ts
    side_in, side_out, side_shape = _side_cast_specs(side, bsz * tiles, lambda b, s: b * tiles + s)
    return pl.pallas_call(
        kern,
        grid=(bsz, tiles),
        in_specs=[pl.BlockSpec((1, ts, d), lambda b, s: (b, s, 0))]
        + [_const_spec(c.shape) for c in consts] + [side_in],
        out_specs=[pl.BlockSpec((1, ts, d), lambda b, s: (b, s, 0)), side_out],
        out_shape=[jax.ShapeDtypeStruct((bsz, seq, d), F32), side_shape],
        scratch_shapes=[
            pltpu.VMEM((ts + CONV_HALO, dc), F32),
            pltpu.VMEM((ts + SC_HALO, ds), F32),
            pltpu.VMEM((ts, ds), F32),
            pltpu.VMEM((ts, dc), BF16),
            pltpu.VMEM((ts, ds), BF16),
            pltpu.VMEM((ts, 2 * d), F32),
        ],
        compiler_params=pltpu.CompilerParams(
            dimension_semantics=("arbitrary", "arbitrary"), vmem_limit_bytes=VMEM_LIMIT),
        name="mixer",
    )(x, *consts, side)


def _ple(xb, p, wpg_ref, bpg_ref, wpp_ref):
    gate = _sigmoid(_dot(xb, wpg_ref[...]) + bpg_ref[...])
    return gate * _dot(p.astype(BF16), wpp_ref[...])


def _dense_ffn_kernel(x_ref, p_ref, wg_ref, wu_ref, wd_ref, wpg_ref, bpg_ref, wpp_ref,
                      g2_ref, b2_ref, side_ref, o_ref, side_out_ref, *, alpha):
    side_out_ref[...] = side_ref[...].astype(BF16)
    x = x_ref[...]
    xb = x.astype(BF16)
    acc = alpha * x + _ple(xb, p_ref[...], wpg_ref, bpg_ref, wpp_ref)
    g = _dot(xb, wg_ref[...])
    u = _dot(xb, wu_ref[...])
    h = (g * _sigmoid(g) * u).astype(BF16)
    acc = acc + _dot(h, wd_ref[...])
    o_ref[...] = _layer_norm(acc, g2_ref[...], b2_ref[...])


def _dense_ffn_call(x, p, layer, wg, wu, wd, wpg, bpg, wpp, g2, b2, side, *, alpha, tm):
    t, d = x.shape
    assert t % tm == 0
    consts = (wg, wu, wd, wpg, bpg, wpp, g2, b2)
    side_in, side_out, side_shape = _side_cast_specs(side, t // tm, lambda i: i)
    return pl.pallas_call(
        functools.partial(_dense_ffn_kernel, alpha=alpha),
        grid=(t // tm,),
        in_specs=[pl.BlockSpec((tm, d), lambda i: (i, 0)),
                  pl.BlockSpec((None, tm, p.shape[2]), lambda i: (layer, i, 0))]
        + [_const_spec(c.shape) for c in consts] + [side_in],
        out_specs=[pl.BlockSpec((tm, d), lambda i: (i, 0)), side_out],
        out_shape=[jax.ShapeDtypeStruct((t, d), F32), side_shape],
        compiler_params=pltpu.CompilerParams(
            dimension_semantics=("arbitrary",), vmem_limit_bytes=VMEM_LIMIT),
        name="dense_ffn",
    )(x, p, *consts, side)


ROW_CHUNKS = (512, 256, 128, 64, 32, 16, 8)


def _for_row_chunks(n, fn):
    off = 0
    for rows in ROW_CHUNKS:
        take = (n & rows) != 0
        pl.when(take)(functools.partial(fn, off, rows))
        off = off + jnp.where(take, rows, 0)


TOTAL_CHUNKS = (1024,) + ROW_CHUNKS


def _wait_rows(n, wait_fn):
    for rows in TOTAL_CHUNKS:
        pl.when((n & rows) != 0)(functools.partial(wait_fn, rows))


def _router_kernel(x_ref, wr_ref, br_ref, lower_ref, info_ref, cnt_ref):
    logits = _dot(x_ref[...].astype(BF16), wr_ref[...]) + br_ref[...]
    lane = lax.broadcasted_iota(jnp.int32, logits.shape, 1)
    m1 = jnp.max(logits, axis=-1, keepdims=True)
    i1 = jnp.min(jnp.where(logits == m1, lane, LANES), axis=-1, keepdims=True)
    rest = jnp.where(lane == i1, 2.0 * NEG_BIG, logits)
    m2 = jnp.max(rest, axis=-1, keepdims=True)
    i2 = jnp.min(jnp.where(rest == m2, lane, LANES), axis=-1, keepdims=True)
    e21 = jnp.exp(m2 - m1)
    w1 = 1.0 / (1.0 + e21)
    w2 = e21 / (1.0 + e21)

    hit1 = lane == i1
    hit2 = lane == i2
    onehot = jnp.where(hit1 | hit2, 1.0, 0.0)
    ranks = _dot(lower_ref[...], onehot.astype(BF16))
    r1 = jnp.sum(jnp.where(hit1, ranks, 0.0), axis=-1, keepdims=True)
    r2 = jnp.sum(jnp.where(hit2, ranks, 0.0), axis=-1, keepdims=True)

    cols = (i1.astype(F32), i2.astype(F32), r1, r2, w1, w2)
    info = jnp.zeros(logits.shape, F32)
    for k, v in enumerate(cols):
        info = jnp.where(lane == k, v, info)
    info_ref[...] = info
    cnt_ref[...] = jnp.broadcast_to(jnp.sum(onehot, axis=0, keepdims=True), cnt_ref.shape)


def _router_call(x, wr, br, *, tm):
    t, d = x.shape
    assert t % tm == 0 and tm < 2 ** 24
    lower = jnp.tril(jnp.ones((tm, tm), BF16), -1)
    return pl.pallas_call(
        _router_kernel,
        grid=(t // tm,),
        in_specs=[pl.BlockSpec((tm, d), lambda i: (i, 0)),
                  _const_spec(wr.shape), _const_spec(br.shape), _const_spec(lower.shape)],
        out_specs=[pl.BlockSpec((tm, LANES), lambda i: (i, 0)),
                   pl.BlockSpec((SUBLANES, LANES), lambda i: (i, 0))],
        out_shape=[jax.ShapeDtypeStruct((t, LANES), F32),
                   jax.ShapeDtypeStruct((t // tm * SUBLANES, LANES), F32)],
        compiler_params=pltpu.CompilerParams(
            dimension_semantics=("arbitrary",), vmem_limit_bytes=VMEM_LIMIT),
        name="router",
    )(x, wr, br, lower)


def _local_slot(expert, rank, loff_ref, base, n_exp):
    off = jnp.zeros(expert.shape, F32)
    for e in range(n_exp):
        off = jnp.where(expert == float(e), loff_ref[base + e].astype(F32), off)
    return (off + rank).astype(jnp.int32)


def _dispatch_kernel(n8_ref, loff_ref, gdst_ref, gap_ref, x_ref, info_ref, xs_ref,
                     comp, zbuf, sems, *, tm, n_exp):
    i = pl.program_id(0)
    n_steps = pl.num_programs(0)
    slot = i % 2
    nc = comp.shape[1]

    def group_copy(step, e, s, off, rows):
        src = comp.at[s, pl.ds(pl.multiple_of(loff_ref[step * n_exp + e] + off, SUBLANES), rows)]
        dst = xs_ref.at[pl.ds(pl.multiple_of(gdst_ref[step * n_exp + e] + off, SUBLANES), rows)]
        return pltpu.make_async_copy(src, dst, sems.at[s])

    def start_groups(step, s):
        for e in range(n_exp):
            _for_row_chunks(n8_ref[step * n_exp + e],
                            lambda off, rows, e=e: group_copy(step, e, s, off, rows).start())

    def wait_groups(step, s):
        total = n8_ref[step * n_exp]
        for e in range(1, n_exp):
            total = total + n8_ref[step * n_exp + e]
        _wait_rows(total, lambda rows: pltpu.make_async_copy(
            comp.at[s, pl.ds(0, rows)], xs_ref.at[pl.ds(0, rows)], sems.at[s]).wait())

    info_t = info_ref[...].T
    slot_row = lax.broadcasted_iota(jnp.int32, (nc, tm), 0)
    hit = None
    for k in range(TOP_K):
        s_k = _local_slot(info_t[k:k + 1, :], info_t[TOP_K + k:TOP_K + k + 1, :],
                          loff_ref, i * n_exp, n_exp)
        hit = (slot_row == s_k) if hit is None else hit | (slot_row == s_k)
    onehot = jnp.where(hit, 1.0, 0.0).astype(BF16)
    comp[slot] = _dot(onehot, x_ref[...].astype(BF16))

    @pl.when(i > 0)
    def _():
        wait_groups(i - 1, 1 - slot)

    start_groups(i, slot)

    @pl.when(i == n_steps - 1)
    def _():
        wait_groups(i, slot)
        zbuf[...] = jnp.zeros(zbuf.shape, F32)

        def gap_copy(e, off, rows):
            dst = xs_ref.at[pl.ds(pl.multiple_of(gap_ref[2 * e] + off, SUBLANES), rows)]
            return pltpu.make_async_copy(zbuf.at[pl.ds(0, rows)], dst, sems.at[0])

        for e in range(n_exp):
            _for_row_chunks(gap_ref[2 * e + 1],
                            lambda off, rows, e=e: gap_copy(e, off, rows).start())
        for e in range(n_exp):
            _for_row_chunks(gap_ref[2 * e + 1],
                            lambda off, rows, e=e: gap_copy(e, off, rows).wait())

        tail_rows = zbuf.shape[0]

        def tail_copy(j):
            start = pl.multiple_of(gap_ref[2 * n_exp] + j * tail_rows, SUBLANES)
            return pltpu.make_async_copy(zbuf, xs_ref.at[pl.ds(start, tail_rows)], sems.at[0])

        def tail(j, c):
            tail_copy(j).start()
            tail_copy(j).wait()
            return c

        lax.fori_loop(0, gap_ref[2 * n_exp + 1], tail, 0)


def _dispatch_call(n8, loff, gdst, gap, x, info, *, tm, tme, n_exp, n_rows):
    t, d = x.shape
    nc = TOP_K * tm + SUBLANES * n_exp
    return pl.pallas_call(
        functools.partial(_dispatch_kernel, tm=tm, n_exp=n_exp),
        grid_spec=pltpu.PrefetchScalarGridSpec(
            num_scalar_prefetch=4,
            grid=(t // tm,),
            in_specs=[pl.BlockSpec((tm, d), lambda i, *_: (i, 0)),
                      pl.BlockSpec((tm, LANES), lambda i, *_: (i, 0))],
            out_specs=pl.BlockSpec(memory_space=pl.ANY),
            scratch_shapes=[pltpu.VMEM((2, nc, d), F32),
                            pltpu.VMEM((tme, d), F32),
                            pltpu.SemaphoreType.DMA((2,))],
        ),
        out_shape=jax.ShapeDtypeStruct((n_rows, d), F32),
        compiler_params=pltpu.CompilerParams(
            dimension_semantics=("arbitrary",), vmem_limit_bytes=VMEM_LIMIT),
        name="dispatch",
    )(n8, loff, gdst, gap, x, info)


def _expert_kernel(tile_e_ref, tile_n_ref, xs_ref, wg_ref, wu_ref, wd_ref, ys_ref, *, fc):
    del tile_e_ref
    i = pl.program_id(0)

    @pl.when(tile_n_ref[i] > 0)
    def _():
        xb = xs_ref[...].astype(BF16)
        f = wg_ref.shape[2]
        acc = jnp.zeros(ys_ref.shape, F32)
        for c in range(f // fc):
            cols = slice(c * fc, (c + 1) * fc)
            g = _dot(xb, wg_ref[0, :, cols])
            u = _dot(xb, wu_ref[0, :, cols])
            h = (g * _sigmoid(g) * u).astype(BF16)
            acc = acc + _dot(h, wd_ref[0, cols, :])
        ys_ref[...] = acc

    @pl.when(tile_n_ref[i] == 0)
    def _():
        ys_ref[...] = jnp.zeros(ys_ref.shape, F32)


def _expert_call(tile_e, tile_n, xs, wg, wu, wd, *, tme, fc):
    n_rows, d = xs.shape
    f = wg.shape[2]
    assert n_rows % tme == 0 and f % fc == 0 and fc % LANES == 0

    def w_spec(shape):
        return pl.BlockSpec((1,) + shape[1:], lambda i, te, tn: (te[i], 0, 0))

    return pl.pallas_call(
        functools.partial(_expert_kernel, fc=fc),
        grid_spec=pltpu.PrefetchScalarGridSpec(
            num_scalar_prefetch=2,
            grid=(n_rows // tme,),
            in_specs=[pl.BlockSpec((tme, d), lambda i, te, tn: (i, 0)),
                      w_spec(wg.shape), w_spec(wu.shape), w_spec(wd.shape)],
            out_specs=pl.BlockSpec((tme, d), lambda i, te, tn: (i, 0)),
        ),
        out_shape=jax.ShapeDtypeStruct((n_rows, d), F32),
        compiler_params=pltpu.CompilerParams(
            dimension_semantics=("arbitrary",), vmem_limit_bytes=VMEM_LIMIT),
        name="experts",
    )(tile_e, tile_n, xs, wg, wu, wd)


def _combine_kernel(n8_ref, loff_ref, gdst_ref, x_ref, p_ref, info_ref, ys_ref, wpg_ref, bpg_ref,
                    wpp_ref, g2_ref, b2_ref, o_ref, ycomp, sems, *, alpha, tm, n_exp):
    i = pl.program_id(0)
    n_steps = pl.num_programs(0)
    slot = i % 2
    nc = ycomp.shape[1]

    def group_copy(step, e, s, off, rows):
        src = ys_ref.at[pl.ds(pl.multiple_of(gdst_ref[step * n_exp + e] + off, SUBLANES), rows)]
        dst = ycomp.at[s, pl.ds(pl.multiple_of(loff_ref[step * n_exp + e] + off, SUBLANES), rows)]
        return pltpu.make_async_copy(src, dst, sems.at[s])

    def start_groups(step, s):
        for e in range(n_exp):
            _for_row_chunks(n8_ref[step * n_exp + e],
                            lambda off, rows, e=e: group_copy(step, e, s, off, rows).start())

    def wait_groups(step, s):
        total = n8_ref[step * n_exp]
        for e in range(1, n_exp):
            total = total + n8_ref[step * n_exp + e]
        _wait_rows(total, lambda rows: pltpu.make_async_copy(
            ys_ref.at[pl.ds(0, rows)], ycomp.at[s, pl.ds(0, rows)], sems.at[s]).wait())

    @pl.when(i == 0)
    def _():
        ycomp[...] = jnp.zeros(ycomp.shape, F32)
        start_groups(0, 0)

    @pl.when(i + 1 < n_steps)
    def _():
        start_groups(i + 1, 1 - slot)

    x = x_ref[...]
    acc = alpha * x + _ple(x.astype(BF16), p_ref[...], wpg_ref, bpg_ref, wpp_ref)

    wait_groups(i, slot)
    info = info_ref[...]
    yb = ycomp[slot].astype(BF16)
    slot_col = lax.broadcasted_iota(jnp.int32, (tm, nc), 1)
    pick = jnp.zeros((tm, nc), F32)
    for k in range(TOP_K):
        s_k = _local_slot(info[:, k:k + 1], info[:, TOP_K + k:TOP_K + k + 1],
                          loff_ref, i * n_exp, n_exp)
        pick = jnp.where(slot_col == s_k, info[:, 4 + k:5 + k], pick)
    acc = acc + _dot(pick.astype(BF16), yb)
    o_ref[...] = _layer_norm(acc, g2_ref[...], b2_ref[...])


def _combine_call(n8, loff, gdst, x, p, layer, info, ys, wpg, bpg, wpp, g2, b2, *, alpha, tm, n_exp):
    t, d = x.shape
    nc = TOP_K * tm + SUBLANES * n_exp
    consts = (wpg, bpg, wpp, g2, b2)

    def const_spec(shape):
        nd = len(shape)
        return pl.BlockSpec(shape, lambda i, *_: (0,) * nd, pipeline_mode=pl.Buffered(1))

    return pl.pallas_call(
        functools.partial(_combine_kernel, alpha=alpha, tm=tm, n_exp=n_exp),
        grid_spec=pltpu.PrefetchScalarGridSpec(
            num_scalar_prefetch=3,
            grid=(t // tm,),
            in_specs=[pl.BlockSpec((tm, d), lambda i, *_: (i, 0)),
                      pl.BlockSpec((None, tm, p.shape[2]), lambda i, *_: (layer, i, 0)),
                      pl.BlockSpec((tm, LANES), lambda i, *_: (i, 0)),
                      pl.BlockSpec(memory_space=pl.ANY)]
            + [const_spec(c.shape) for c in consts],
            out_specs=pl.BlockSpec((tm, d), lambda i, *_: (i, 0)),
            scratch_shapes=[pltpu.VMEM((2, nc, d), F32), pltpu.SemaphoreType.DMA((2,))],
        ),
        out_shape=jax.ShapeDtypeStruct((t, d), F32),
        compiler_params=pltpu.CompilerParams(
            dimension_semantics=("arbitrary",), vmem_limit_bytes=VMEM_LIMIT),
        name="combine",
    )(n8, loff, gdst, x, p, info, ys, *consts)


def _moe_layer(x, p, layer, w_router, b_router, we_gate, we_up, we_down, wpg, bpg, wpp, g2, b2,
               *, alpha, tm, tme, fc):
    t, d = x.shape
    n_exp = w_router.shape[1]
    n_tok_tiles = t // tm
    assert n_exp <= LANES and tm <= ROW_CHUNKS[0] and tme <= ROW_CHUNKS[0]
    wr = jnp.zeros((d, LANES), BF16).at[:, :n_exp].set(w_router.astype(BF16))
    br = jnp.full((1, LANES), NEG_BIG, F32).at[0, :n_exp].set(b_router)
    info, cnt = _router_call(x, wr, br, tm=tm)

    i32 = jnp.int32
    n = cnt[::SUBLANES, :n_exp].astype(i32)
    n8 = (n + SUBLANES - 1) // SUBLANES * SUBLANES
    loff = jnp.cumsum(n8, axis=1) - n8
    tot = jnp.sum(n8, axis=0)
    reg = (tot + tme - 1) // tme * tme
    reg_end = jnp.cumsum(reg)
    base = reg_end - reg
    gdst = base[None, :] + jnp.cumsum(n8, axis=0) - n8
    n_tiles = (t * TOP_K + n_tok_tiles * n_exp * (SUBLANES - 1)) // tme + n_exp + 1
    n_rows = n_tiles * tme
    tile_start = jnp.arange(n_tiles, dtype=i32) * tme
    tile_e = jnp.minimum(jnp.sum(tile_start[:, None] >= reg_end[None, :], axis=1), n_exp - 1).astype(i32)
    tile_n = jnp.clip(tot[tile_e] - (tile_start - base[tile_e]), 0, tme).astype(i32)
    gap = jnp.stack([base + tot, reg - tot], axis=1).reshape(-1)
    gap = jnp.concatenate([gap, jnp.stack([reg_end[-1], (n_rows - reg_end[-1]) // tme])]).astype(i32)
    n8f, lofff, gdstf = (a.reshape(-1).astype(i32) for a in (n8, loff, gdst))

    xs = _dispatch_call(n8f, lofff, gdstf, gap, x, info, tm=tm, tme=tme, n_exp=n_exp, n_rows=n_rows)
    ys = _expert_call(tile_e, tile_n, xs, we_gate, we_up, we_down, tme=tme, fc=fc)
    return _combine_call(n8f, lofff, gdstf, x, p, layer, info, ys, wpg, bpg, wpp, g2, b2,
                         alpha=alpha, tm=tm, n_exp=n_exp)


def _row(v):
    return v.reshape(1, -1)


def kernel(x, p, w_in, b_in, conf_conv_w, conf_conv_b, conf_ln_g, conf_ln_b, w_conf_out,
           sc_conv_w, w_sc_out, w_o, ln1_g, ln1_b, w_ff_gate, w_ff_up, w_ff_down, w_router,
           b_router, we_gate, we_up, we_down, w_ple_gate, b_ple_gate, w_ple_proj, ln2_g, ln2_b,
           *, ts=512, tm=512, tme=512, fc_moe=512):
    depth = w_in.shape[0]
    alpha = (2 * depth) ** 0.25
    bsz, seq, d = x.shape
    pt = p.reshape(depth, bsz * seq, -1)

    expert_w = (("gate", we_gate), ("up", we_up), ("down", we_down))
    queue = [((i // 2, name), w[i // 2], i) for i in range(depth) if i % 2 == 1 for name, w in expert_w]
    cast = {}
    idle = jnp.zeros((bsz * seq // min(ts, tm) * 2 * SUBLANES, LANES), F32)

    def side_job(layer, is_mixer):
        for k, (key, w, needed_in) in enumerate(queue):
            if layer < needed_in or (layer == needed_in and is_mixer):
                del queue[k]
                return key, w
        return None, idle

    def run(call, *args, layer, is_mixer, **kw):
        key, w = side_job(layer, is_mixer)
        out, w_bf16 = call(*args, w.reshape(-1, w.shape[-1]), **kw)
        if key is not None:
            cast[key] = w_bf16.reshape(w.shape)
        return out

    for i in range(depth):
        x = run(_mixer_call,
                x, w_in[i].astype(BF16), _row(b_in[i]), conf_conv_w[i], _row(conf_conv_b[i]),
                _row(conf_ln_g[i]), _row(conf_ln_b[i]), w_conf_out[i].astype(BF16), sc_conv_w[i],
                w_sc_out[i].astype(BF16), w_o[i].astype(BF16), _row(ln1_g[i]), _row(ln1_b[i]),
                layer=i, is_mixer=True, alpha=alpha, ts=ts)
        xt = x.reshape(bsz * seq, d)
        ple_w = (w_ple_gate[i].astype(BF16), _row(b_ple_gate[i]), w_ple_proj[i].astype(BF16),
                 _row(ln2_g[i]), _row(ln2_b[i]))
        j = i // 2
        if i % 2 == 0:
            xt = run(_dense_ffn_call,
                     xt, pt, i, w_ff_gate[j].astype(BF16), w_ff_up[j].astype(BF16),
                     w_ff_down[j].astype(BF16), *ple_w, layer=i, is_mixer=False, alpha=alpha, tm=tm)
        else:
            queue[:] = [q for q in queue if q[2] != i]
            ew = [cast.pop((j, name)) if (j, name) in cast else w[j].astype(BF16)
                  for name, w in expert_w]
            xt = _moe_layer(xt, pt, i, w_router[j], b_router[j], *ew,
                            *ple_w, alpha=alpha, tm=tm, tme=tme, fc=fc_moe)
        x = xt.reshape(bsz, seq, d)
    return x
```

```python
import functools

import jax
import jax.numpy as jnp
from jax import lax
from jax.experimental import pallas as pl
from jax.experimental.pallas import tpu as pltpu

LN_EPS = 1e-5
TOP_K = 2
LANES = 128
SUBLANES = 8
CONV_HALO = 32
SC_HALO = 8
MXU_COLS = 256
CONV_ROWS = 64
NEG_BIG = -1e30
VMEM_LIMIT = 56 * 1024 * 1024

F32 = jnp.float32
BF16 = jnp.bfloat16


def _dot(a, b):
    return jnp.dot(a, b, preferred_element_type=F32)


def _sigmoid(x):
    return 1.0 / (1.0 + jnp.exp(-x))


def _layer_norm(x, g, b):
    mu = jnp.mean(x, axis=-1, keepdims=True)
    xc = x - mu
    var = jnp.mean(xc * xc, axis=-1, keepdims=True)
    return xc * lax.rsqrt(var + LN_EPS) * g + b


def _const_spec(shape):
    nd = len(shape)
    return pl.BlockSpec(shape, lambda *_: (0,) * nd, pipeline_mode=pl.Buffered(1))


def _side_cast_specs(side, n_steps, step_index):
    rows, cols = side.shape
    assert rows % (n_steps * 2 * SUBLANES) == 0
    block = (rows // n_steps, cols)
    spec = pl.BlockSpec(block, lambda *idx: (step_index(*idx), 0))
    return spec, spec, jax.ShapeDtypeStruct(side.shape, BF16)


def _causal_dwconv_chunk(src_ref, row0, halo, w_ref, taps, lanes, init):
    acc = init
    first = halo - (taps - 1)
    for r in range(SUBLANES):
        offs = [o for o in range(first, halo + 1) if o % SUBLANES == r]
        if not offs:
            continue
        base = offs[0]
        span = offs[-1] - base + CONV_ROWS
        blk = src_ref[row0 + base:row0 + base + span, lanes]
        part = None
        for o in offs:
            j = o - first
            term = w_ref[j:j + 1, lanes] * blk[o - base:o - base + CONV_ROWS, :]
            part = term if part is None else part + term
        acc = acc + part
    return acc


def _mixer_kernel(x_ref, w_in_ref, b_in_ref, ccw_ref, ccb_ref, clg_ref, clb_ref, wa_ref,
                  scw_ref, wb_ref, wo_ref, g1_ref, b1_ref, side_ref, o_ref, side_out_ref,
                  a_ext, u_ext, scb_buf, act_a, act_b, zg_buf, *, alpha, ts, dc, ds, kc, ks):
    s = pl.program_id(1)
    side_out_ref[...] = side_ref[...].astype(BF16)

    @pl.when(s == 0)
    def _():
        a_ext[0:CONV_HALO, :] = jnp.zeros((CONV_HALO, dc), F32)
        u_ext[0:SC_HALO, :] = jnp.zeros((SC_HALO, ds), F32)

    @pl.when(s > 0)
    def _():
        a_ext[0:CONV_HALO, :] = a_ext[ts:ts + CONV_HALO, :]
        u_ext[0:SC_HALO, :] = u_ext[ts:ts + SC_HALO, :]

    x = x_ref[0]
    xb = x.astype(BF16)

    c0 = 2 * dc
    c1 = c0 + 3 * ds
    zc = _dot(xb, w_in_ref[:, 0:c0]) + b_in_ref[:, 0:c0]
    a_ext[CONV_HALO:CONV_HALO + ts, :] = zc[:, :dc] * _sigmoid(zc[:, dc:])
    zs = _dot(xb, w_in_ref[:, c0:c1]) + b_in_ref[:, c0:c1]
    scb_buf[...] = zs[:, :ds]
    u_ext[SC_HALO:SC_HALO + ts, :] = zs[:, ds:2 * ds] * zs[:, 2 * ds:]

    d = x.shape[-1]
    n_piece = 2 * d // MXU_COLS
    units = (ts // CONV_ROWS) * (dc // LANES)
    zg_pieces = []

    def gate_piece(j):
        cols = slice(c1 + j * MXU_COLS, c1 + (j + 1) * MXU_COLS)
        z = _dot(xb, w_in_ref[:, cols]) + b_in_ref[:, cols]
        zg_buf[:, j * MXU_COLS:(j + 1) * MXU_COLS] = z
        zg_pieces.append(z[0:SUBLANES, 0:LANES])

    def after(piece):
        bits = pltpu.bitcast(piece, jnp.uint32)
        return pltpu.bitcast((bits >> 16) >> 16, F32)

    u = 0
    for ci in range(ts // CONV_ROWS):
        row0 = ci * CONV_ROWS
        conv = []
        for cg in range(dc // LANES):
            lanes = slice(cg * LANES, (cg + 1) * LANES)
            want = min(n_piece, -(-(u + 1) * n_piece // units))
            while len(zg_pieces) < want:
                gate_piece(len(zg_pieces))
            u += 1
            init = jnp.broadcast_to(ccb_ref[:, lanes], (CONV_ROWS, LANES))
            init = init + jnp.tile(after(zg_pieces[-1]), (CONV_ROWS // SUBLANES, 1))
            conv.append(_causal_dwconv_chunk(a_ext, row0, CONV_HALO, ccw_ref, kc, lanes, init))
        conv = jnp.concatenate(conv, axis=-1)
        ln = _layer_norm(conv, clg_ref[...], clb_ref[...])
        act_a[row0:row0 + CONV_ROWS, :] = (ln * _sigmoid(ln)).astype(BF16)
        sconv = []
        for cg in range(ds // LANES):
            lanes = slice(cg * LANES, (cg + 1) * LANES)
            init = jnp.zeros((CONV_ROWS, LANES), F32)
            sconv.append(_causal_dwconv_chunk(u_ext, row0, SC_HALO, scw_ref, ks, lanes, init))
        sconv = jnp.concatenate(sconv, axis=-1)
        act_b[row0:row0 + CONV_ROWS, :] = (scb_buf[row0:row0 + CONV_ROWS, :] * sconv).astype(BF16)
    zg = zg_buf[...]

    y_a = _dot(act_a[...], wa_ref[...])
    y_b = _dot(act_b[...], wb_ref[...])
    m = _sigmoid(zg[:, :d]) * y_a + _sigmoid(zg[:, d:]) * y_b
    mix = _dot(m.astype(BF16), wo_ref[...])
    o_ref[0] = _layer_norm(alpha * x + mix, g1_ref[...], b1_ref[...])


def _mixer_call(x, w_in, b_in, ccw, ccb, clg, clb, wa, scw, wb, wo, g1, b1, side, *, alpha, ts):
    bsz, seq, d = x.shape
    kc, dc = ccw.shape
    ks, ds = scw.shape
    assert seq % ts == 0 and ts % CONV_ROWS == 0 and ts >= CONV_HALO
    assert kc - 1 <= CONV_HALO and ks - 1 <= SC_HALO
    assert dc % LANES == 0 and ds % LANES == 0 and (2 * d) % MXU_COLS == 0
    kern = functools.partial(_mixer_kernel, alpha=alpha, ts=ts, dc=dc, ds=ds, kc=kc, ks=ks)
    consts = (w_in, b_in, ccw, ccb, clg, clb, wa, scw, wb, wo, g1, b1)
    tiles = seq // ts
    side_in, side_out, side_shape = _side_cast_specs(side, bsz * tiles, lambda b, s: b * tiles + s)
    return pl.pallas_call(
        kern,
        grid=(bsz, tiles),
        in_specs=[pl.BlockSpec((1, ts, d), lambda b, s: (b, s, 0))]
        + [_const_spec(c.shape) for c in consts] + [side_in],
        out_specs=[pl.BlockSpec((1, ts, d), lambda b, s: (b, s, 0)), side_out],
        out_shape=[jax.ShapeDtypeStruct((bsz, seq, d), F32), side_shape],
        scratch_shapes=[
            pltpu.VMEM((ts + CONV_HALO, dc), F32),
            pltpu.VMEM((ts + SC_HALO, ds), F32),
            pltpu.VMEM((ts, ds), F32),
            pltpu.VMEM((ts, dc), BF16),
            pltpu.VMEM((ts, ds), BF16),
            pltpu.VMEM((ts, 2 * d), F32),
        ],
        compiler_params=pltpu.CompilerParams(
            dimension_semantics=("arbitrary", "arbitrary"), vmem_limit_bytes=VMEM_LIMIT),
        name="mixer",
    )(x, *consts, side)


def _ple(xb, p, wpg_ref, bpg_ref, wpp_ref):
    gate = _sigmoid(_dot(xb, wpg_ref[...]) + bpg_ref[...])
    return gate * _dot(p.astype(BF16), wpp_ref[...])


def _dense_ffn_kernel(x_ref, p_ref, wg_ref, wu_ref, wd_ref, wpg_ref, bpg_ref, wpp_ref,
                      g2_ref, b2_ref, side_ref, o_ref, side_out_ref, *, alpha):
    side_out_ref[...] = side_ref[...].astype(BF16)
    x = x_ref[...]
    xb = x.astype(BF16)
    acc = alpha * x + _ple(xb, p_ref[...], wpg_ref, bpg_ref, wpp_ref)
    g = _dot(xb, wg_ref[...])
    u = _dot(xb, wu_ref[...])
    h = (g * _sigmoid(g) * u).astype(BF16)
    acc = acc + _dot(h, wd_ref[...])
    o_ref[...] = _layer_norm(acc, g2_ref[...], b2_ref[...])


def _dense_ffn_call(x, p, layer, wg, wu, wd, wpg, bpg, wpp, g2, b2, side, *, alpha, tm):
    t, d = x.shape
    assert t % tm == 0
    consts = (wg, wu, wd, wpg, bpg, wpp, g2, b2)
    side_in, side_out, side_shape = _side_cast_specs(side, t // tm, lambda i: i)
    return pl.pallas_call(
        functools.partial(_dense_ffn_kernel, alpha=alpha),
        grid=(t // tm,),
        in_specs=[pl.BlockSpec((tm, d), lambda i: (i, 0)),
                  pl.BlockSpec((None, tm, p.shape[2]), lambda i: (layer, i, 0))]
        + [_const_spec(c.shape) for c in consts] + [side_in],
        out_specs=[pl.BlockSpec((tm, d), lambda i: (i, 0)), side_out],
        out_shape=[jax.ShapeDtypeStruct((t, d), F32), side_shape],
        compiler_params=pltpu.CompilerParams(
            dimension_semantics=("arbitrary",), vmem_limit_bytes=VMEM_LIMIT),
        name="dense_ffn",
    )(x, p, *consts, side)


ROW_CHUNKS = (512, 256, 128, 64, 32, 16, 8)


def _for_row_chunks(n, fn):
    off = 0
    for rows in ROW_CHUNKS:
        take = (n & rows) != 0
        pl.when(take)(functools.partial(fn, off, rows))
        off = off + jnp.where(take, rows, 0)


TOTAL_CHUNKS = (1024,) + ROW_CHUNKS


def _wait_rows(n, wait_fn):
    for rows in TOTAL_CHUNKS:
        pl.when((n & rows) != 0)(functools.partial(wait_fn, rows))


def _router_kernel(x_ref, wr_ref, br_ref, upper_ref, info_ref, cnt_ref, *, n_exp):
    logits = _dot(x_ref[...].astype(BF16), wr_ref[...]) + br_ref[...]
    lt = logits.T
    tm = lt.shape[1]
    row = [lt[e:e + 1, :] for e in range(n_exp)]
    zero = jnp.zeros((1, tm), F32)
    m1, i1, m2, i2 = row[0], zero, jnp.full((1, tm), 2.0 * NEG_BIG, F32), zero
    for e in range(1, n_exp):
        first = row[e] > m1
        second = row[e] > m2
        m2 = jnp.where(first, m1, jnp.where(second, row[e], m2))
        i2 = jnp.where(first, i1, jnp.where(second, float(e), i2))
        m1 = jnp.where(first, row[e], m1)
        i1 = jnp.where(first, float(e), i1)
    e21 = jnp.exp(m2 - m1)
    w1 = 1.0 / (1.0 + e21)
    w2 = e21 / (1.0 + e21)

    expert = lax.broadcasted_iota(jnp.int32, (SUBLANES, tm), 0).astype(F32)
    hit1 = expert == i1
    hit2 = expert == i2
    onehot = jnp.where(hit1 | hit2, 1.0, 0.0)
    ranks = _dot(onehot.astype(BF16), upper_ref[...])
    r1 = jnp.sum(jnp.where(hit1, ranks, 0.0), axis=0, keepdims=True)
    r2 = jnp.sum(jnp.where(hit2, ranks, 0.0), axis=0, keepdims=True)
    info_ref[...] = jnp.concatenate([i1, i2, r1, r2, w1, w2, zero, zero], axis=0)
    counts = jnp.sum(onehot, axis=1, keepdims=True)
    cnt_ref[...] = jnp.broadcast_to(counts, cnt_ref.shape)


def _router_call(x, wr, br, *, tm, n_exp):
    t, d = x.shape
    assert t % tm == 0 and tm < 2 ** 24 and n_exp <= SUBLANES
    upper = jnp.triu(jnp.ones((tm, tm), BF16), 1)
    return pl.pallas_call(
        functools.partial(_router_kernel, n_exp=n_exp),
        grid=(t // tm,),
        in_specs=[pl.BlockSpec((tm, d), lambda i: (i, 0)),
                  _const_spec(wr.shape), _const_spec(br.shape), _const_spec(upper.shape)],
        out_specs=[pl.BlockSpec((SUBLANES, tm), lambda i: (0, i)),
                   pl.BlockSpec((SUBLANES, LANES), lambda i: (i, 0))],
        out_shape=[jax.ShapeDtypeStruct((SUBLANES, t), F32),
                   jax.ShapeDtypeStruct((t // tm * SUBLANES, LANES), F32)],
        compiler_params=pltpu.CompilerParams(
            dimension_semantics=("arbitrary",), vmem_limit_bytes=VMEM_LIMIT),
        name="router",
    )(x, wr, br, upper)


def _local_slot(expert, rank, loff_ref, base, n_exp):
    off = jnp.zeros(expert.shape, F32)
    for e in range(n_exp):
        off = jnp.where(expert == float(e), loff_ref[base + e].astype(F32), off)
    return (off + rank).astype(jnp.int32)


def _dispatch_kernel(n8_ref, loff_ref, gdst_ref, gap_ref, x_ref, info_ref, xs_ref,
                     comp, zbuf, sems, *, tm, n_exp):
    i = pl.program_id(0)
    n_steps = pl.num_programs(0)
    slot = i % 2
    nc = comp.shape[1]

    def group_copy(step, e, s, off, rows):
        src = comp.at[s, pl.ds(pl.multiple_of(loff_ref[step * n_exp + e] + off, SUBLANES), rows)]
        dst = xs_ref.at[pl.ds(pl.multiple_of(gdst_ref[step * n_exp + e] + off, SUBLANES), rows)]
        return pltpu.make_async_copy(src, dst, sems.at[s])

    def start_groups(step, s):
        for e in range(n_exp):
            _for_row_chunks(n8_ref[step * n_exp + e],
                            lambda off, rows, e=e: group_copy(step, e, s, off, rows).start())

    def wait_groups(step, s):
        total = n8_ref[step * n_exp]
        for e in range(1, n_exp):
            total = total + n8_ref[step * n_exp + e]
        _wait_rows(total, lambda rows: pltpu.make_async_copy(
            comp.at[s, pl.ds(0, rows)], xs_ref.at[pl.ds(0, rows)], sems.at[s]).wait())

    info_t = info_ref[...]
    slot_row = lax.broadcasted_iota(jnp.int32, (nc, tm), 0)
    hit = None
    for k in range(TOP_K):
        s_k = _local_slot(info_t[k:k + 1, :], info_t[TOP_K + k:TOP_K + k + 1, :],
                          loff_ref, i * n_exp, n_exp)
        hit = (slot_row == s_k) if hit is None else hit | (slot_row == s_k)
    onehot = jnp.where(hit, 1.0, 0.0).astype(BF16)
    comp[slot] = _dot(onehot, x_ref[...].astype(BF16))

    @pl.when(i > 0)
    def _():
        wait_groups(i - 1, 1 - slot)

    start_groups(i, slot)

    @pl.when(i == n_steps - 1)
    def _():
        wait_groups(i, slot)
        zbuf[...] = jnp.zeros(zbuf.shape, F32)

        def gap_copy(e, off, rows):
            dst = xs_ref.at[pl.ds(pl.multiple_of(gap_ref[2 * e] + off, SUBLANES), rows)]
            return pltpu.make_async_copy(zbuf.at[pl.ds(0, rows)], dst, sems.at[0])

        for e in range(n_exp):
            _for_row_chunks(gap_ref[2 * e + 1],
                            lambda off, rows, e=e: gap_copy(e, off, rows).start())
        for e in range(n_exp):
            _for_row_chunks(gap_ref[2 * e + 1],
                            lambda off, rows, e=e: gap_copy(e, off, rows).wait())

        tail_rows = zbuf.shape[0]

        def tail_copy(j):
            start = pl.multiple_of(gap_ref[2 * n_exp] + j * tail_rows, SUBLANES)
            return pltpu.make_async_copy(zbuf, xs_ref.at[pl.ds(start, tail_rows)], sems.at[0])

        def tail(j, c):
            tail_copy(j).start()
            tail_copy(j).wait()
            return c

        lax.fori_loop(0, gap_ref[2 * n_exp + 1], tail, 0)


def _dispatch_call(n8, loff, gdst, gap, x, info, *, tm, tme, n_exp, n_rows):
    t, d = x.shape
    nc = TOP_K * tm + SUBLANES * n_exp
    return pl.pallas_call(
        functools.partial(_dispatch_kernel, tm=tm, n_exp=n_exp),
        grid_spec=pltpu.PrefetchScalarGridSpec(
            num_scalar_prefetch=4,
            grid=(t // tm,),
            in_specs=[pl.BlockSpec((tm, d), lambda i, *_: (i, 0)),
                      pl.BlockSpec((SUBLANES, tm), lambda i, *_: (0, i))],
            out_specs=pl.BlockSpec(memory_space=pl.ANY),
            scratch_shapes=[pltpu.VMEM((2, nc, d), F32),
                            pltpu.VMEM((tme, d), F32),
                            pltpu.SemaphoreType.DMA((2,))],
        ),
        out_shape=jax.ShapeDtypeStruct((n_rows, d), F32),
        compiler_params=pltpu.CompilerParams(
            dimension_semantics=("arbitrary",), vmem_limit_bytes=VMEM_LIMIT),
        name="dispatch",
    )(n8, loff, gdst, gap, x, info)


def _expert_kernel(tile_e_ref, tile_n_ref, xs_ref, wg_ref, wu_ref, wd_ref, ys_ref, *, fc):
    del tile_e_ref
    i = pl.program_id(0)

    @pl.when(tile_n_ref[i] > 0)
    def _():
        xb = xs_ref[...].astype(BF16)
        f = wg_ref.shape[2]
        acc = jnp.zeros(ys_ref.shape, F32)
        for c in range(f // fc):
            cols = slice(c * fc, (c + 1) * fc)
            g = _dot(xb, wg_ref[0, :, cols])
            u = _dot(xb, wu_ref[0, :, cols])
            h = (g * _sigmoid(g) * u).astype(BF16)
            acc = acc + _dot(h, wd_ref[0, cols, :])
        ys_ref[...] = acc

    @pl.when(tile_n_ref[i] == 0)
    def _():
        ys_ref[...] = jnp.zeros(ys_ref.shape, F32)


def _expert_call(tile_e, tile_n, xs, wg, wu, wd, *, tme, fc):
    n_rows, d = xs.shape
    f = wg.shape[2]
    assert n_rows % tme == 0 and f % fc == 0 and fc % LANES == 0

    def w_spec(shape):
        return pl.BlockSpec((1,) + shape[1:], lambda i, te, tn: (te[i], 0, 0))

    return pl.pallas_call(
        functools.partial(_expert_kernel, fc=fc),
        grid_spec=pltpu.PrefetchScalarGridSpec(
            num_scalar_prefetch=2,
            grid=(n_rows // tme,),
            in_specs=[pl.BlockSpec((tme, d), lambda i, te, tn: (i, 0)),
                      w_spec(wg.shape), w_spec(wu.shape), w_spec(wd.shape)],
            out_specs=pl.BlockSpec((tme, d), lambda i, te, tn: (i, 0)),
        ),
        out_shape=jax.ShapeDtypeStruct((n_rows, d), F32),
        compiler_params=pltpu.CompilerParams(
            dimension_semantics=("arbitrary",), vmem_limit_bytes=VMEM_LIMIT),
        name="experts",
    )(tile_e, tile_n, xs, wg, wu, wd)


def _combine_kernel(n8_ref, loff_ref, gdst_ref, x_ref, p_ref, info_ref, ys_ref, wpg_ref, bpg_ref,
                    wpp_ref, g2_ref, b2_ref, o_ref, ycomp, sems, *, alpha, tm, n_exp):
    i = pl.program_id(0)
    n_steps = pl.num_programs(0)
    slot = i % 2
    nc = ycomp.shape[1]

    def group_copy(step, e, s, off, rows):
        src = ys_ref.at[pl.ds(pl.multiple_of(gdst_ref[step * n_exp + e] + off, SUBLANES), rows)]
        dst = ycomp.at[s, pl.ds(pl.multiple_of(loff_ref[step * n_exp + e] + off, SUBLANES), rows)]
        return pltpu.make_async_copy(src, dst, sems.at[s])

    def start_groups(step, s):
        for e in range(n_exp):
            _for_row_chunks(n8_ref[step * n_exp + e],
                            lambda off, rows, e=e: group_copy(step, e, s, off, rows).start())

    def wait_groups(step, s):
        total = n8_ref[step * n_exp]
        for e in range(1, n_exp):
            total = total + n8_ref[step * n_exp + e]
        _wait_rows(total, lambda rows: pltpu.make_async_copy(
            ys_ref.at[pl.ds(0, rows)], ycomp.at[s, pl.ds(0, rows)], sems.at[s]).wait())

    @pl.when(i == 0)
    def _():
        ycomp[...] = jnp.zeros(ycomp.shape, F32)
        start_groups(0, 0)

    @pl.when(i + 1 < n_steps)
    def _():
        start_groups(i + 1, 1 - slot)

    x = x_ref[...]
    acc = alpha * x + _ple(x.astype(BF16), p_ref[...], wpg_ref, bpg_ref, wpp_ref)

    wait_groups(i, slot)
    info_t = info_ref[...]
    info = jnp.concatenate([info_t, jnp.zeros((LANES - SUBLANES, tm), F32)], axis=0).T
    yb = ycomp[slot].astype(BF16)
    slot_col = lax.broadcasted_iota(jnp.int32, (tm, nc), 1)
    pick = jnp.zeros((tm, nc), F32)
    for k in range(TOP_K):
        s_k = _local_slot(info[:, k:k + 1], info[:, TOP_K + k:TOP_K + k + 1],
                          loff_ref, i * n_exp, n_exp)
        pick = jnp.where(slot_col == s_k, info[:, 4 + k:5 + k], pick)
    acc = acc + _dot(pick.astype(BF16), yb)
    o_ref[...] = _layer_norm(acc, g2_ref[...], b2_ref[...])


def _combine_call(n8, loff, gdst, x, p, layer, info, ys, wpg, bpg, wpp, g2, b2, *, alpha, tm, n_exp):
    t, d = x.shape
    nc = TOP_K * tm + SUBLANES * n_exp
    consts = (wpg, bpg, wpp, g2, b2)

    def const_spec(shape):
        nd = len(shape)
        return pl.BlockSpec(shape, lambda i, *_: (0,) * nd, pipeline_mode=pl.Buffered(1))

    return pl.pallas_call(
        functools.partial(_combine_kernel, alpha=alpha, tm=tm, n_exp=n_exp),
        grid_spec=pltpu.PrefetchScalarGridSpec(
            num_scalar_prefetch=3,
            grid=(t // tm,),
            in_specs=[pl.BlockSpec((tm, d), lambda i, *_: (i, 0)),
                      pl.BlockSpec((None, tm, p.shape[2]), lambda i, *_: (layer, i, 0)),
                      pl.BlockSpec((SUBLANES, tm), lambda i, *_: (0, i)),
                      pl.BlockSpec(memory_space=pl.ANY)]
            + [const_spec(c.shape) for c in consts],
            out_specs=pl.BlockSpec((tm, d), lambda i, *_: (i, 0)),
            scratch_shapes=[pltpu.VMEM((2, nc, d), F32), pltpu.SemaphoreType.DMA((2,))],
        ),
        out_shape=jax.ShapeDtypeStruct((t, d), F32),
        compiler_params=pltpu.CompilerParams(
            dimension_semantics=("arbitrary",), vmem_limit_bytes=VMEM_LIMIT),
        name="combine",
    )(n8, loff, gdst, x, p, info, ys, *consts)


def _moe_layer(x, p, layer, w_router, b_router, we_gate, we_up, we_down, wpg, bpg, wpp, g2, b2,
               *, alpha, tm, tme, fc):
    t, d = x.shape
    n_exp = w_router.shape[1]
    n_tok_tiles = t // tm
    assert tm <= ROW_CHUNKS[0] and tme <= ROW_CHUNKS[0] and tm % LANES == 0
    wr = jnp.zeros((d, LANES), BF16).at[:, :n_exp].set(w_router.astype(BF16))
    br = jnp.full((1, LANES), NEG_BIG, F32).at[0, :n_exp].set(b_router)
    info, cnt = _router_call(x, wr, br, tm=tm, n_exp=n_exp)

    i32 = jnp.int32
    n = cnt[:, 0].reshape(n_tok_tiles, SUBLANES)[:, :n_exp].astype(i32)
    n8 = (n + SUBLANES - 1) // SUBLANES * SUBLANES
    loff = jnp.cumsum(n8, axis=1) - n8
    tot = jnp.sum(n8, axis=0)
    reg = (tot + tme - 1) // tme * tme
    reg_end = jnp.cumsum(reg)
    base = reg_end - reg
    gdst = base[None, :] + jnp.cumsum(n8, axis=0) - n8
    n_tiles = (t * TOP_K + n_tok_tiles * n_exp * (SUBLANES - 1)) // tme + n_exp + 1
    n_rows = n_tiles * tme
    tile_start = jnp.arange(n_tiles, dtype=i32) * tme
    tile_e = jnp.minimum(jnp.sum(tile_start[:, None] >= reg_end[None, :], axis=1), n_exp - 1).astype(i32)
    tile_n = jnp.clip(tot[tile_e] - (tile_start - base[tile_e]), 0, tme).astype(i32)
    gap = jnp.stack([base + tot, reg - tot], axis=1).reshape(-1)
    gap = jnp.concatenate([gap, jnp.stack([reg_end[-1], (n_rows - reg_end[-1]) // tme])]).astype(i32)
    n8f, lofff, gdstf = (a.reshape(-1).astype(i32) for a in (n8, loff, gdst))

    xs = _dispatch_call(n8f, lofff, gdstf, gap, x, info, tm=tm, tme=tme, n_exp=n_exp, n_rows=n_rows)
    ys = _expert_call(tile_e, tile_n, xs, we_gate, we_up, we_down, tme=tme, fc=fc)
    return _combine_call(n8f, lofff, gdstf, x, p, layer, info, ys, wpg, bpg, wpp, g2, b2,
                         alpha=alpha, tm=tm, n_exp=n_exp)


def _row(v):
    return v.reshape(1, -1)


def kernel(x, p, w_in, b_in, conf_conv_w, conf_conv_b, conf_ln_g, conf_ln_b, w_conf_out,
           sc_conv_w, w_sc_out, w_o, ln1_g, ln1_b, w_ff_gate, w_ff_up, w_ff_down, w_router,
           b_router, we_gate, we_up, we_down, w_ple_gate, b_ple_gate, w_ple_proj, ln2_g, ln2_b,
           *, ts=512, tm=512, tme=512, fc_moe=512):
    depth = w_in.shape[0]
    alpha = (2 * depth) ** 0.25
    bsz, seq, d = x.shape
    pt = p.reshape(depth, bsz * seq, -1)

    expert_w = (("gate", we_gate), ("up", we_up), ("down", we_down))
    queue = [((i // 2, name), w[i // 2], i) for i in range(depth) if i % 2 == 1 for name, w in expert_w]
    cast = {}
    idle = jnp.zeros((bsz * seq // min(ts, tm) * 2 * SUBLANES, LANES), F32)

    def side_job(layer, is_mixer):
        for k, (key, w, needed_in) in enumerate(queue):
            if layer < needed_in or (layer == needed_in and is_mixer):
                del queue[k]
                return key, w
        return None, idle

    def run(call, *args, layer, is_mixer, **kw):
        key, w = side_job(layer, is_mixer)
        out, w_bf16 = call(*args, w.reshape(-1, w.shape[-1]), **kw)
        if key is not None:
            cast[key] = w_bf16.reshape(w.shape)
        return out

    for i in range(depth):
        x = run(_mixer_call,
                x, w_in[i].astype(BF16), _row(b_in[i]), conf_conv_w[i], _row(conf_conv_b[i]),
                _row(conf_ln_g[i]), _row(conf_ln_b[i]), w_conf_out[i].astype(BF16), sc_conv_w[i],
                w_sc_out[i].astype(BF16), w_o[i].astype(BF16), _row(ln1_g[i]), _row(ln1_b[i]),
                layer=i, is_mixer=True, alpha=alpha, ts=ts)
        xt = x.reshape(bsz * seq, d)
        ple_w = (w_ple_gate[i].astype(BF16), _row(b_ple_gate[i]), w_ple_proj[i].astype(BF16),
                 _row(ln2_g[i]), _row(ln2_b[i]))
        j = i // 2
        if i % 2 == 0:
            xt = run(_dense_ffn_call,
                     xt, pt, i, w_ff_gate[j].astype(BF16), w_ff_up[j].astype(BF16),
                     w_ff_down[j].astype(BF16), *ple_w, layer=i, is_mixer=False, alpha=alpha, tm=tm)
        else:
            queue[:] = [q for q in queue if q[2] != i]
            ew = [cast.pop((j, name)) if (j, name) in cast else w[j].astype(BF16)
                  for name, w in expert_w]
            xt = _moe_layer(xt, pt, i, w_router[j], b_router[j], *ew,
                            *ple_w, alpha=alpha, tm=tm, tme=tme, fc=fc_moe)
        x = xt.reshape(bsz, seq, d)
    return x
```

```python
import functools
from typing import NamedTuple

import jax
import jax.numpy as jnp
from jax import lax
from jax.experimental import pallas as pl
from jax.experimental.pallas import tpu as pltpu

LN_EPS = 1e-5
TOP_K = 2
LANES = 128
SUBLANES = 8
CONV_HALO = 32
SC_HALO = 8
MXU_COLS = 256
CONV_ROWS = 64
NEG_BIG = -1e30
VMEM_LIMIT = 56 * 1024 * 1024

F32 = jnp.float32
BF16 = jnp.bfloat16


class Tiles(NamedTuple):
    seq_rows: int = 512
    token_rows: int = 512
    expert_rows: int = 512
    expert_cols: int = 512


TILES = Tiles()


def _dot(a, b):
    return jnp.dot(a, b, preferred_element_type=F32)


def _sigmoid(x):
    return 1.0 / (1.0 + jnp.exp(-x))


def _layer_norm(x, g, b):
    mu = jnp.mean(x, axis=-1, keepdims=True)
    xc = x - mu
    var = jnp.mean(xc * xc, axis=-1, keepdims=True)
    return xc * lax.rsqrt(var + LN_EPS) * g + b


def _const_spec(shape):
    nd = len(shape)
    return pl.BlockSpec(shape, lambda *_: (0,) * nd, pipeline_mode=pl.Buffered(1))


def _side_cast_specs(side, n_steps, step_index):
    rows, cols = side.shape
    assert rows % (n_steps * 2 * SUBLANES) == 0
    block = (rows // n_steps, cols)
    spec = pl.BlockSpec(block, lambda *idx: (step_index(*idx), 0))
    return spec, spec, jax.ShapeDtypeStruct(side.shape, BF16)


def _causal_dwconv_chunk(src_ref, row0, halo, w_ref, taps, lanes, init):
    acc = init
    first = halo - (taps - 1)
    for r in range(SUBLANES):
        offs = [o for o in range(first, halo + 1) if o % SUBLANES == r]
        if not offs:
            continue
        base = offs[0]
        span = offs[-1] - base + CONV_ROWS
        blk = src_ref[row0 + base:row0 + base + span, lanes]
        part = None
        for o in offs:
            j = o - first
            term = w_ref[j:j + 1, lanes] * blk[o - base:o - base + CONV_ROWS, :]
            part = term if part is None else part + term
        acc = acc + part
    return acc


def _mixer_kernel(x_ref, w_in_ref, b_in_ref, ccw_ref, ccb_ref, clg_ref, clb_ref, wa_ref,
                  scw_ref, wb_ref, wo_ref, g1_ref, b1_ref, side_ref, o_ref, side_out_ref,
                  a_ext, u_ext, scb_buf, act_a, act_b, zg_buf, *, alpha, ts, dc, ds, kc, ks):
    s = pl.program_id(1)
    side_out_ref[...] = side_ref[...].astype(BF16)

    @pl.when(s == 0)
    def _():
        a_ext[0:CONV_HALO, :] = jnp.zeros((CONV_HALO, dc), F32)
        u_ext[0:SC_HALO, :] = jnp.zeros((SC_HALO, ds), F32)

    @pl.when(s > 0)
    def _():
        a_ext[0:CONV_HALO, :] = a_ext[ts:ts + CONV_HALO, :]
        u_ext[0:SC_HALO, :] = u_ext[ts:ts + SC_HALO, :]

    x = x_ref[0]
    xb = x.astype(BF16)

    c0 = 2 * dc
    c1 = c0 + 3 * ds
    zc = _dot(xb, w_in_ref[:, 0:c0]) + b_in_ref[:, 0:c0]
    a_ext[CONV_HALO:CONV_HALO + ts, :] = zc[:, :dc] * _sigmoid(zc[:, dc:])
    zs = _dot(xb, w_in_ref[:, c0:c1]) + b_in_ref[:, c0:c1]
    scb_buf[...] = zs[:, :ds]
    u_ext[SC_HALO:SC_HALO + ts, :] = zs[:, ds:2 * ds] * zs[:, 2 * ds:]

    d = x.shape[-1]
    n_piece = 2 * d // MXU_COLS
    units = (ts // CONV_ROWS) * (dc // LANES)
    zg_pieces = []

    def gate_piece(j):
        cols = slice(c1 + j * MXU_COLS, c1 + (j + 1) * MXU_COLS)
        z = _dot(xb, w_in_ref[:, cols]) + b_in_ref[:, cols]
        zg_buf[:, j * MXU_COLS:(j + 1) * MXU_COLS] = z
        zg_pieces.append(z[0:SUBLANES, 0:LANES])

    def after(piece):
        bits = pltpu.bitcast(piece, jnp.uint32)
        return pltpu.bitcast((bits >> 16) >> 16, F32)

    u = 0
    for ci in range(ts // CONV_ROWS):
        row0 = ci * CONV_ROWS
        conv = []
        for cg in range(dc // LANES):
            lanes = slice(cg * LANES, (cg + 1) * LANES)
            want = min(n_piece, -(-(u + 1) * n_piece // units))
            while len(zg_pieces) < want:
                gate_piece(len(zg_pieces))
            u += 1
            init = jnp.broadcast_to(ccb_ref[:, lanes], (CONV_ROWS, LANES))
            init = init + jnp.tile(after(zg_pieces[-1]), (CONV_ROWS // SUBLANES, 1))
            conv.append(_causal_dwconv_chunk(a_ext, row0, CONV_HALO, ccw_ref, kc, lanes, init))
        conv = jnp.concatenate(conv, axis=-1)
        ln = _layer_norm(conv, clg_ref[...], clb_ref[...])
        act_a[row0:row0 + CONV_ROWS, :] = (ln * _sigmoid(ln)).astype(BF16)
        sconv = []
        for cg in range(ds // LANES):
            lanes = slice(cg * LANES, (cg + 1) * LANES)
            init = jnp.zeros((CONV_ROWS, LANES), F32)
            sconv.append(_causal_dwconv_chunk(u_ext, row0, SC_HALO, scw_ref, ks, lanes, init))
        sconv = jnp.concatenate(sconv, axis=-1)
        act_b[row0:row0 + CONV_ROWS, :] = (scb_buf[row0:row0 + CONV_ROWS, :] * sconv).astype(BF16)
    zg = zg_buf[...]

    y_a = _dot(act_a[...], wa_ref[...])
    y_b = _dot(act_b[...], wb_ref[...])
    m = _sigmoid(zg[:, :d]) * y_a + _sigmoid(zg[:, d:]) * y_b
    mix = _dot(m.astype(BF16), wo_ref[...])
    o_ref[0] = _layer_norm(alpha * x + mix, g1_ref[...], b1_ref[...])


def _mixer_call(x, w_in, b_in, ccw, ccb, clg, clb, wa, scw, wb, wo, g1, b1, side, *, alpha, ts):
    bsz, seq, d = x.shape
    kc, dc = ccw.shape
    ks, ds = scw.shape
    assert seq % ts == 0 and ts % CONV_ROWS == 0 and ts >= CONV_HALO
    assert kc - 1 <= CONV_HALO and ks - 1 <= SC_HALO
    assert dc % LANES == 0 and ds % LANES == 0 and (2 * d) % MXU_COLS == 0
    kern = functools.partial(_mixer_kernel, alpha=alpha, ts=ts, dc=dc, ds=ds, kc=kc, ks=ks)
    consts = (w_in, b_in, ccw, ccb, clg, clb, wa, scw, wb, wo, g1, b1)
    tiles = seq // ts
    side_in, side_out, side_shape = _side_cast_specs(side, bsz * tiles, lambda b, s: b * tiles + s)
    return pl.pallas_call(
        kern,
        grid=(bsz, tiles),
        in_specs=[pl.BlockSpec((1, ts, d), lambda b, s: (b, s, 0))]
        + [_const_spec(c.shape) for c in consts] + [side_in],
        out_specs=[pl.BlockSpec((1, ts, d), lambda b, s: (b, s, 0)), side_out],
        out_shape=[jax.ShapeDtypeStruct((bsz, seq, d), F32), side_shape],
        scratch_shapes=[
            pltpu.VMEM((ts + CONV_HALO, dc), F32),
            pltpu.VMEM((ts + SC_HALO, ds), F32),
            pltpu.VMEM((ts, ds), F32),
            pltpu.VMEM((ts, dc), BF16),
            pltpu.VMEM((ts, ds), BF16),
            pltpu.VMEM((ts, 2 * d), F32),
        ],
        compiler_params=pltpu.CompilerParams(
            dimension_semantics=("arbitrary", "arbitrary"), vmem_limit_bytes=VMEM_LIMIT),
        name="mixer",
    )(x, *consts, side)


def _ple(xb, p, wpg_ref, bpg_ref, wpp_ref):
    gate = _sigmoid(_dot(xb, wpg_ref[...]) + bpg_ref[...])
    return gate * _dot(p.astype(BF16), wpp_ref[...])


def _dense_ffn_kernel(x_ref, p_ref, wg_ref, wu_ref, wd_ref, wpg_ref, bpg_ref, wpp_ref,
                      g2_ref, b2_ref, side_ref, o_ref, side_out_ref, *, alpha):
    side_out_ref[...] = side_ref[...].astype(BF16)
    x = x_ref[...]
    xb = x.astype(BF16)
    acc = alpha * x + _ple(xb, p_ref[...], wpg_ref, bpg_ref, wpp_ref)
    g = _dot(xb, wg_ref[...])
    u = _dot(xb, wu_ref[...])
    h = (g * _sigmoid(g) * u).astype(BF16)
    acc = acc + _dot(h, wd_ref[...])
    o_ref[...] = _layer_norm(acc, g2_ref[...], b2_ref[...])


def _dense_ffn_call(x, p, layer, wg, wu, wd, wpg, bpg, wpp, g2, b2, side, *, alpha, tm):
    t, d = x.shape
    assert t % tm == 0
    consts = (wg, wu, wd, wpg, bpg, wpp, g2, b2)
    side_in, side_out, side_shape = _side_cast_specs(side, t // tm, lambda i: i)
    return pl.pallas_call(
        functools.partial(_dense_ffn_kernel, alpha=alpha),
        grid=(t // tm,),
        in_specs=[pl.BlockSpec((tm, d), lambda i: (i, 0)),
                  pl.BlockSpec((None, tm, p.shape[2]), lambda i: (layer, i, 0))]
        + [_const_spec(c.shape) for c in consts] + [side_in],
        out_specs=[pl.BlockSpec((tm, d), lambda i: (i, 0)), side_out],
        out_shape=[jax.ShapeDtypeStruct((t, d), F32), side_shape],
        compiler_params=pltpu.CompilerParams(
            dimension_semantics=("arbitrary",), vmem_limit_bytes=VMEM_LIMIT),
        name="dense_ffn",
    )(x, p, *consts, side)


ROW_CHUNKS = (512, 256, 128, 64, 32, 16, 8)


def _for_row_chunks(n, fn):
    off = 0
    for rows in ROW_CHUNKS:
        take = (n & rows) != 0
        pl.when(take)(functools.partial(fn, off, rows))
        off = off + jnp.where(take, rows, 0)


TOTAL_CHUNKS = (1024,) + ROW_CHUNKS


def _wait_rows(n, wait_fn):
    for rows in TOTAL_CHUNKS:
        pl.when((n & rows) != 0)(functools.partial(wait_fn, rows))


def _router_kernel(x_ref, wr_ref, br_ref, upper_ref, info_ref, cnt_ref, *, n_exp):
    logits = _dot(x_ref[...].astype(BF16), wr_ref[...]) + br_ref[...]
    lt = logits.T
    tm = lt.shape[1]
    row = [lt[e:e + 1, :] for e in range(n_exp)]
    zero = jnp.zeros((1, tm), F32)
    m1, i1, m2, i2 = row[0], zero, jnp.full((1, tm), 2.0 * NEG_BIG, F32), zero
    for e in range(1, n_exp):
        first = row[e] > m1
        second = row[e] > m2
        m2 = jnp.where(first, m1, jnp.where(second, row[e], m2))
        i2 = jnp.where(first, i1, jnp.where(second, float(e), i2))
        m1 = jnp.where(first, row[e], m1)
        i1 = jnp.where(first, float(e), i1)
    e21 = jnp.exp(m2 - m1)
    w1 = 1.0 / (1.0 + e21)
    w2 = e21 / (1.0 + e21)

    expert = lax.broadcasted_iota(jnp.int32, (SUBLANES, tm), 0).astype(F32)
    hit1 = expert == i1
    hit2 = expert == i2
    onehot = jnp.where(hit1 | hit2, 1.0, 0.0)
    ranks = _dot(onehot.astype(BF16), upper_ref[...])
    r1 = jnp.sum(jnp.where(hit1, ranks, 0.0), axis=0, keepdims=True)
    r2 = jnp.sum(jnp.where(hit2, ranks, 0.0), axis=0, keepdims=True)
    info_ref[...] = jnp.concatenate([i1, i2, r1, r2, w1, w2, zero, zero], axis=0)
    counts = jnp.sum(onehot, axis=1, keepdims=True)
    cnt_ref[...] = jnp.broadcast_to(counts, cnt_ref.shape)


def _router_call(x, wr, br, *, tm, n_exp):
    t, d = x.shape
    assert t % tm == 0 and tm < 2 ** 24 and n_exp <= SUBLANES
    upper = jnp.triu(jnp.ones((tm, tm), BF16), 1)
    return pl.pallas_call(
        functools.partial(_router_kernel, n_exp=n_exp),
        grid=(t // tm,),
        in_specs=[pl.BlockSpec((tm, d), lambda i: (i, 0)),
                  _const_spec(wr.shape), _const_spec(br.shape), _const_spec(upper.shape)],
        out_specs=[pl.BlockSpec((SUBLANES, tm), lambda i: (0, i)),
                   pl.BlockSpec((SUBLANES, LANES), lambda i: (i, 0))],
        out_shape=[jax.ShapeDtypeStruct((SUBLANES, t), F32),
                   jax.ShapeDtypeStruct((t // tm * SUBLANES, LANES), F32)],
        compiler_params=pltpu.CompilerParams(
            dimension_semantics=("arbitrary",), vmem_limit_bytes=VMEM_LIMIT),
        name="router",
    )(x, wr, br, upper)


def _local_slot(expert, rank, loff_ref, base, n_exp):
    off = jnp.zeros(expert.shape, F32)
    for e in range(n_exp):
        off = jnp.where(expert == float(e), loff_ref[base + e].astype(F32), off)
    return (off + rank).astype(jnp.int32)


def _dispatch_kernel(n8_ref, loff_ref, gdst_ref, gap_ref, x_ref, info_ref, xs_ref,
                     comp, zbuf, sems, *, tm, n_exp):
    i = pl.program_id(0)
    n_steps = pl.num_programs(0)
    slot = i % 2
    nc = comp.shape[1]

    def group_copy(step, e, s, off, rows):
        src = comp.at[s, pl.ds(pl.multiple_of(loff_ref[step * n_exp + e] + off, SUBLANES), rows)]
        dst = xs_ref.at[pl.ds(pl.multiple_of(gdst_ref[step * n_exp + e] + off, SUBLANES), rows)]
        return pltpu.make_async_copy(src, dst, sems.at[s])

    def start_groups(step, s):
        for e in range(n_exp):
            _for_row_chunks(n8_ref[step * n_exp + e],
                            lambda off, rows, e=e: group_copy(step, e, s, off, rows).start())

    def wait_groups(step, s):
        total = n8_ref[step * n_exp]
        for e in range(1, n_exp):
            total = total + n8_ref[step * n_exp + e]
        _wait_rows(total, lambda rows: pltpu.make_async_copy(
            comp.at[s, pl.ds(0, rows)], xs_ref.at[pl.ds(0, rows)], sems.at[s]).wait())

    info_t = info_ref[...]
    slot_row = lax.broadcasted_iota(jnp.int32, (nc, tm), 0)
    hit = None
    for k in range(TOP_K):
        s_k = _local_slot(info_t[k:k + 1, :], info_t[TOP_K + k:TOP_K + k + 1, :],
                          loff_ref, i * n_exp, n_exp)
        hit = (slot_row == s_k) if hit is None else hit | (slot_row == s_k)
    onehot = jnp.where(hit, 1.0, 0.0).astype(BF16)
    comp[slot] = _dot(onehot, x_ref[...].astype(BF16))

    @pl.when(i > 0)
    def _():
        wait_groups(i - 1, 1 - slot)

    start_groups(i, slot)

    @pl.when(i == n_steps - 1)
    def _():
        wait_groups(i, slot)
        zbuf[...] = jnp.zeros(zbuf.shape, F32)

        def gap_copy(e, off, rows):
            dst = xs_ref.at[pl.ds(pl.multiple_of(gap_ref[2 * e] + off, SUBLANES), rows)]
            return pltpu.make_async_copy(zbuf.at[pl.ds(0, rows)], dst, sems.at[0])

        for e in range(n_exp):
            _for_row_chunks(gap_ref[2 * e + 1],
                            lambda off, rows, e=e: gap_copy(e, off, rows).start())
        for e in range(n_exp):
            _for_row_chunks(gap_ref[2 * e + 1],
                            lambda off, rows, e=e: gap_copy(e, off, rows).wait())

        tail_rows = zbuf.shape[0]

        def tail_copy(j):
            start = pl.multiple_of(gap_ref[2 * n_exp] + j * tail_rows, SUBLANES)
            return pltpu.make_async_copy(zbuf, xs_ref.at[pl.ds(start, tail_rows)], sems.at[0])

        def tail(j, c):
            tail_copy(j).start()
            tail_copy(j).wait()
            return c

        lax.fori_loop(0, gap_ref[2 * n_exp + 1], tail, 0)


def _dispatch_call(n8, loff, gdst, gap, x, info, *, tm, tme, n_exp, n_rows):
    t, d = x.shape
    nc = TOP_K * tm + SUBLANES * n_exp
    return pl.pallas_call(
        functools.partial(_dispatch_kernel, tm=tm, n_exp=n_exp),
        grid_spec=pltpu.PrefetchScalarGridSpec(
            num_scalar_prefetch=4,
            grid=(t // tm,),
            in_specs=[pl.BlockSpec((tm, d), lambda i, *_: (i, 0)),
                      pl.BlockSpec((SUBLANES, tm), lambda i, *_: (0, i))],
            out_specs=pl.BlockSpec(memory_space=pl.ANY),
            scratch_shapes=[pltpu.VMEM((2, nc, d), F32),
                            pltpu.VMEM((tme, d), F32),
                            pltpu.SemaphoreType.DMA((2,))],
        ),
        out_shape=jax.ShapeDtypeStruct((n_rows, d), F32),
        compiler_params=pltpu.CompilerParams(
            dimension_semantics=("arbitrary",), vmem_limit_bytes=VMEM_LIMIT),
        name="dispatch",
    )(n8, loff, gdst, gap, x, info)


def _expert_kernel(tile_e_ref, tile_n_ref, xs_ref, wg_ref, wu_ref, wd_ref, ys_ref, *, fc):
    del tile_e_ref
    i = pl.program_id(0)

    @pl.when(tile_n_ref[i] > 0)
    def _():
        xb = xs_ref[...].astype(BF16)
        f = wg_ref.shape[2]
        acc = jnp.zeros(ys_ref.shape, F32)
        for c in range(f // fc):
            cols = slice(c * fc, (c + 1) * fc)
            g = _dot(xb, wg_ref[0, :, cols])
            u = _dot(xb, wu_ref[0, :, cols])
            h = (g * _sigmoid(g) * u).astype(BF16)
            acc = acc + _dot(h, wd_ref[0, cols, :])
        ys_ref[...] = acc

    @pl.when(tile_n_ref[i] == 0)
    def _():
        ys_ref[...] = jnp.zeros(ys_ref.shape, F32)


def _expert_call(tile_e, tile_n, xs, wg, wu, wd, *, tme, fc):
    n_rows, d = xs.shape
    f = wg.shape[2]
    assert n_rows % tme == 0 and f % fc == 0 and fc % LANES == 0

    def w_spec(shape):
        return pl.BlockSpec((1,) + shape[1:], lambda i, te, tn: (te[i], 0, 0))

    return pl.pallas_call(
        functools.partial(_expert_kernel, fc=fc),
        grid_spec=pltpu.PrefetchScalarGridSpec(
            num_scalar_prefetch=2,
            grid=(n_rows // tme,),
            in_specs=[pl.BlockSpec((tme, d), lambda i, te, tn: (i, 0)),
                      w_spec(wg.shape), w_spec(wu.shape), w_spec(wd.shape)],
            out_specs=pl.BlockSpec((tme, d), lambda i, te, tn: (i, 0)),
        ),
        out_shape=jax.ShapeDtypeStruct((n_rows, d), F32),
        compiler_params=pltpu.CompilerParams(
            dimension_semantics=("arbitrary",), vmem_limit_bytes=VMEM_LIMIT),
        name="experts",
    )(tile_e, tile_n, xs, wg, wu, wd)


def _combine_kernel(n8_ref, loff_ref, gdst_ref, x_ref, p_ref, info_ref, ys_ref, wpg_ref, bpg_ref,
                    wpp_ref, g2_ref, b2_ref, o_ref, ycomp, sems, *, alpha, tm, n_exp):
    i = pl.program_id(0)
    n_steps = pl.num_programs(0)
    slot = i % 2
    nc = ycomp.shape[1]

    def group_copy(step, e, s, off, rows):
        src = ys_ref.at[pl.ds(pl.multiple_of(gdst_ref[step * n_exp + e] + off, SUBLANES), rows)]
        dst = ycomp.at[s, pl.ds(pl.multiple_of(loff_ref[step * n_exp + e] + off, SUBLANES), rows)]
        return pltpu.make_async_copy(src, dst, sems.at[s])

    def start_groups(step, s):
        for e in range(n_exp):
            _for_row_chunks(n8_ref[step * n_exp + e],
                            lambda off, rows, e=e: group_copy(step, e, s, off, rows).start())

    def wait_groups(step, s):
        total = n8_ref[step * n_exp]
        for e in range(1, n_exp):
            total = total + n8_ref[step * n_exp + e]
        _wait_rows(total, lambda rows: pltpu.make_async_copy(
            ys_ref.at[pl.ds(0, rows)], ycomp.at[s, pl.ds(0, rows)], sems.at[s]).wait())

    @pl.when(i == 0)
    def _():
        ycomp[...] = jnp.zeros(ycomp.shape, F32)
        start_groups(0, 0)

    @pl.when(i + 1 < n_steps)
    def _():
        start_groups(i + 1, 1 - slot)

    x = x_ref[...]
    acc = alpha * x + _ple(x.astype(BF16), p_ref[...], wpg_ref, bpg_ref, wpp_ref)

    info_t = info_ref[...]
    info = jnp.concatenate([info_t, jnp.zeros((LANES - SUBLANES, tm), F32)], axis=0).T
    slot_col = lax.broadcasted_iota(jnp.int32, (tm, nc), 1)
    pick = jnp.zeros((tm, nc), F32)
    for k in range(TOP_K):
        s_k = _local_slot(info[:, k:k + 1], info[:, TOP_K + k:TOP_K + k + 1],
                          loff_ref, i * n_exp, n_exp)
        pick = jnp.where(slot_col == s_k, info[:, 4 + k:5 + k], pick)
    pick = pick.astype(BF16)

    wait_groups(i, slot)
    acc = acc + _dot(pick, ycomp[slot].astype(BF16))
    o_ref[...] = _layer_norm(acc, g2_ref[...], b2_ref[...])


def _combine_call(n8, loff, gdst, x, p, layer, info, ys, wpg, bpg, wpp, g2, b2, *, alpha, tm, n_exp):
    t, d = x.shape
    nc = TOP_K * tm + SUBLANES * n_exp
    consts = (wpg, bpg, wpp, g2, b2)

    def const_spec(shape):
        nd = len(shape)
        return pl.BlockSpec(shape, lambda i, *_: (0,) * nd, pipeline_mode=pl.Buffered(1))

    return pl.pallas_call(
        functools.partial(_combine_kernel, alpha=alpha, tm=tm, n_exp=n_exp),
        grid_spec=pltpu.PrefetchScalarGridSpec(
            num_scalar_prefetch=3,
            grid=(t // tm,),
            in_specs=[pl.BlockSpec((tm, d), lambda i, *_: (i, 0)),
                      pl.BlockSpec((None, tm, p.shape[2]), lambda i, *_: (layer, i, 0)),
                      pl.BlockSpec((SUBLANES, tm), lambda i, *_: (0, i)),
                      pl.BlockSpec(memory_space=pl.ANY)]
            + [const_spec(c.shape) for c in consts],
            out_specs=pl.BlockSpec((tm, d), lambda i, *_: (i, 0)),
            scratch_shapes=[pltpu.VMEM((2, nc, d), F32), pltpu.SemaphoreType.DMA((2,))],
        ),
        out_shape=jax.ShapeDtypeStruct((t, d), F32),
        compiler_params=pltpu.CompilerParams(
            dimension_semantics=("arbitrary",), vmem_limit_bytes=VMEM_LIMIT),
        name="combine",
    )(n8, loff, gdst, x, p, info, ys, *consts)


def _moe_layer(x, p, layer, w_router, b_router, we_gate, we_up, we_down, wpg, bpg, wpp, g2, b2,
               *, alpha, tm, tme, fc):
    t, d = x.shape
    n_exp = w_router.shape[1]
    n_tok_tiles = t // tm
    assert tm <= ROW_CHUNKS[0] and tme <= ROW_CHUNKS[0] and tm % LANES == 0
    wr = jnp.zeros((d, LANES), BF16).at[:, :n_exp].set(w_router.astype(BF16))
    br = jnp.full((1, LANES), NEG_BIG, F32).at[0, :n_exp].set(b_router)
    info, cnt = _router_call(x, wr, br, tm=tm, n_exp=n_exp)

    i32 = jnp.int32
    n = cnt[:, 0].reshape(n_tok_tiles, SUBLANES)[:, :n_exp].astype(i32)
    n8 = (n + SUBLANES - 1) // SUBLANES * SUBLANES
    loff = jnp.cumsum(n8, axis=1) - n8
    tot = jnp.sum(n8, axis=0)
    reg = (tot + tme - 1) // tme * tme
    reg_end = jnp.cumsum(reg)
    base = reg_end - reg
    gdst = base[None, :] + jnp.cumsum(n8, axis=0) - n8
    n_tiles = (t * TOP_K + n_tok_tiles * n_exp * (SUBLANES - 1)) // tme + n_exp + 1
    n_rows = n_tiles * tme
    tile_start = jnp.arange(n_tiles, dtype=i32) * tme
    tile_e = jnp.minimum(jnp.sum(tile_start[:, None] >= reg_end[None, :], axis=1), n_exp - 1).astype(i32)
    tile_n = jnp.clip(tot[tile_e] - (tile_start - base[tile_e]), 0, tme).astype(i32)
    gap = jnp.stack([base + tot, reg - tot], axis=1).reshape(-1)
    gap = jnp.concatenate([gap, jnp.stack([reg_end[-1], (n_rows - reg_end[-1]) // tme])]).astype(i32)
    n8f, lofff, gdstf = (a.reshape(-1).astype(i32) for a in (n8, loff, gdst))

    xs = _dispatch_call(n8f, lofff, gdstf, gap, x, info, tm=tm, tme=tme, n_exp=n_exp, n_rows=n_rows)
    ys = _expert_call(tile_e, tile_n, xs, we_gate, we_up, we_down, tme=tme, fc=fc)
    return _combine_call(n8f, lofff, gdstf, x, p, layer, info, ys, wpg, bpg, wpp, g2, b2,
                         alpha=alpha, tm=tm, n_exp=n_exp)


def _row(v):
    return v.reshape(1, -1)


def kernel(x, p, w_in, b_in, conf_conv_w, conf_conv_b, conf_ln_g, conf_ln_b, w_conf_out,
           sc_conv_w, w_sc_out, w_o, ln1_g, ln1_b, w_ff_gate, w_ff_up, w_ff_down, w_router,
           b_router, we_gate, we_up, we_down, w_ple_gate, b_ple_gate, w_ple_proj, ln2_g, ln2_b,
           ):
    ts, tm, tme, fc_moe = TILES
    depth = w_in.shape[0]
    alpha = (2 * depth) ** 0.25
    bsz, seq, d = x.shape
    pt = p.reshape(depth, bsz * seq, -1)

    expert_w = (("gate", we_gate), ("up", we_up), ("down", we_down))
    queue = [((i // 2, name), w[i // 2], i) for i in range(depth) if i % 2 == 1 for name, w in expert_w]
    cast = {}
    idle = jnp.zeros((bsz * seq // min(ts, tm) * 2 * SUBLANES, LANES), F32)

    def side_job(layer, is_mixer):
        for k, (key, w, needed_in) in enumerate(queue):
            if layer < needed_in or (layer == needed_in and is_mixer):
                del queue[k]
                return key, w
        return None, idle

    def run(call, *args, layer, is_mixer, **kw):
        key, w = side_job(layer, is_mixer)
        out, w_bf16 = call(*args, w.reshape(-1, w.shape[-1]), **kw)
        if key is not None:
            cast[key] = w_bf16.reshape(w.shape)
        return out

    for i in range(depth):
        x = run(_mixer_call,
                x, w_in[i].astype(BF16), _row(b_in[i]), conf_conv_w[i], _row(conf_conv_b[i]),
                _row(conf_ln_g[i]), _row(conf_ln_b[i]), w_conf_out[i].astype(BF16), sc_conv_w[i],
                w_sc_out[i].astype(BF16), w_o[i].astype(BF16), _row(ln1_g[i]), _row(ln1_b[i]),
                layer=i, is_mixer=True, alpha=alpha, ts=ts)
        xt = x.reshape(bsz * seq, d)
        ple_w = (w_ple_gate[i].astype(BF16), _row(b_ple_gate[i]), w_ple_proj[i].astype(BF16),
                 _row(ln2_g[i]), _row(ln2_b[i]))
        j = i // 2
        if i % 2 == 0:
            xt = run(_dense_ffn_call,
                     xt, pt, i, w_ff_gate[j].astype(BF16), w_ff_up[j].astype(BF16),
                     w_ff_down[j].astype(BF16), *ple_w, layer=i, is_mixer=False, alpha=alpha, tm=tm)
        else:
            queue[:] = [q for q in queue if q[2] != i]
            ew = [cast.pop((j, name)) if (j, name) in cast else w[j].astype(BF16)
                  for name, w in expert_w]
            xt = _moe_layer(xt, pt, i, w_router[j], b_router[j], *ew,
                            *ple_w, alpha=alpha, tm=tm, tme=tme, fc=fc_moe)
        x = xt.reshape(bsz, seq, d)
    return x
```

```python
import functools
from typing import NamedTuple

import jax
import jax.numpy as jnp
from jax import lax
from jax.experimental import pallas as pl
from jax.experimental.pallas import tpu as pltpu

LN_EPS = 1e-5
TOP_K = 2
LANES = 128
SUBLANES = 8
CONV_HALO = 32
SC_HALO = 8
MXU_COLS = 256
CONV_ROWS = 64
NEG_BIG = -1e30
VMEM_LIMIT = 56 * 1024 * 1024

F32 = jnp.float32
BF16 = jnp.bfloat16


class Tiles(NamedTuple):
    seq_rows: int = 512
    token_rows: int = 512
    expert_rows: int = 512
    expert_cols: int = 512


TILES = Tiles()


def _dot(a, b):
    return jnp.dot(a, b, preferred_element_type=F32)


def _sigmoid(x):
    return 1.0 / (1.0 + jnp.exp(-x))


def _layer_norm(x, g, b):
    mu = jnp.mean(x, axis=-1, keepdims=True)
    xc = x - mu
    var = jnp.mean(xc * xc, axis=-1, keepdims=True)
    return xc * lax.rsqrt(var + LN_EPS) * g + b


def _const_spec(shape):
    nd = len(shape)
    return pl.BlockSpec(shape, lambda *_: (0,) * nd, pipeline_mode=pl.Buffered(1))


def _route(x_tile, wr_ref, br_ref, upper_ref, n_exp):
    logits = _dot(x_tile.astype(BF16), wr_ref[...]) + br_ref[...]
    lt = logits.T
    tm = lt.shape[1]
    row = [lt[e:e + 1, :] for e in range(n_exp)]
    zero = jnp.zeros((1, tm), F32)
    m1, i1, m2, i2 = row[0], zero, jnp.full((1, tm), 2.0 * NEG_BIG, F32), zero
    for e in range(1, n_exp):
        first = row[e] > m1
        second = row[e] > m2
        m2 = jnp.where(first, m1, jnp.where(second, row[e], m2))
        i2 = jnp.where(first, i1, jnp.where(second, float(e), i2))
        m1 = jnp.where(first, row[e], m1)
        i1 = jnp.where(first, float(e), i1)
    e21 = jnp.exp(m2 - m1)
    w1 = 1.0 / (1.0 + e21)
    w2 = e21 / (1.0 + e21)

    expert = lax.broadcasted_iota(jnp.int32, (SUBLANES, tm), 0).astype(F32)
    hit1 = expert == i1
    hit2 = expert == i2
    onehot = jnp.where(hit1 | hit2, 1.0, 0.0)
    ranks = _dot(onehot.astype(BF16), upper_ref[...])
    r1 = jnp.sum(jnp.where(hit1, ranks, 0.0), axis=0, keepdims=True)
    r2 = jnp.sum(jnp.where(hit2, ranks, 0.0), axis=0, keepdims=True)
    info = jnp.concatenate([i1, i2, r1, r2, w1, w2, zero, zero], axis=0)
    return info, jnp.sum(onehot, axis=1, keepdims=True)


def _side_cast_specs(side, n_steps, step_index):
    rows, cols = side.shape
    assert rows % (n_steps * 2 * SUBLANES) == 0
    block = (rows // n_steps, cols)
    spec = pl.BlockSpec(block, lambda *idx: (step_index(*idx), 0))
    return spec, spec, jax.ShapeDtypeStruct(side.shape, BF16)


def _causal_dwconv_chunk(src_ref, row0, halo, w_ref, taps, lanes, init):
    acc = init
    first = halo - (taps - 1)
    for r in range(SUBLANES):
        offs = [o for o in range(first, halo + 1) if o % SUBLANES == r]
        if not offs:
            continue
        base = offs[0]
        span = offs[-1] - base + CONV_ROWS
        blk = src_ref[row0 + base:row0 + base + span, lanes]
        part = None
        for o in offs:
            j = o - first
            term = w_ref[j:j + 1, lanes] * blk[o - base:o - base + CONV_ROWS, :]
            part = term if part is None else part + term
        acc = acc + part
    return acc


def _mixer_kernel(x_ref, w_in_ref, b_in_ref, ccw_ref, ccb_ref, clg_ref, clb_ref, wa_ref,
                  scw_ref, wb_ref, wo_ref, g1_ref, b1_ref, wr_ref, br_ref, upper_ref, side_ref,
                  o_ref, side_out_ref, info_ref, cnt_ref,
                  a_ext, u_ext, scb_buf, act_a, act_b, zg_buf, *, alpha, ts, dc, ds, kc, ks, n_exp):
    s = pl.program_id(1)
    side_out_ref[...] = side_ref[...].astype(BF16)

    @pl.when(s == 0)
    def _():
        a_ext[0:CONV_HALO, :] = jnp.zeros((CONV_HALO, dc), F32)
        u_ext[0:SC_HALO, :] = jnp.zeros((SC_HALO, ds), F32)

    @pl.when(s > 0)
    def _():
        a_ext[0:CONV_HALO, :] = a_ext[ts:ts + CONV_HALO, :]
        u_ext[0:SC_HALO, :] = u_ext[ts:ts + SC_HALO, :]

    x = x_ref[0]
    xb = x.astype(BF16)

    c0 = 2 * dc
    c1 = c0 + 3 * ds
    zc = _dot(xb, w_in_ref[:, 0:c0]) + b_in_ref[:, 0:c0]
    a_ext[CONV_HALO:CONV_HALO + ts, :] = zc[:, :dc] * _sigmoid(zc[:, dc:])
    zs = _dot(xb, w_in_ref[:, c0:c1]) + b_in_ref[:, c0:c1]
    scb_buf[...] = zs[:, :ds]
    u_ext[SC_HALO:SC_HALO + ts, :] = zs[:, ds:2 * ds] * zs[:, 2 * ds:]

    d = x.shape[-1]
    n_piece = 2 * d // MXU_COLS
    units = (ts // CONV_ROWS) * (dc // LANES)
    zg_pieces = []

    def gate_piece(j):
        cols = slice(c1 + j * MXU_COLS, c1 + (j + 1) * MXU_COLS)
        z = _dot(xb, w_in_ref[:, cols]) + b_in_ref[:, cols]
        zg_buf[:, j * MXU_COLS:(j + 1) * MXU_COLS] = z
        zg_pieces.append(z[0:SUBLANES, 0:LANES])

    def after(piece):
        bits = pltpu.bitcast(piece, jnp.uint32)
        return pltpu.bitcast((bits >> 16) >> 16, F32)

    u = 0
    for ci in range(ts // CONV_ROWS):
        row0 = ci * CONV_ROWS
        conv = []
        for cg in range(dc // LANES):
            lanes = slice(cg * LANES, (cg + 1) * LANES)
            want = min(n_piece, -(-(u + 1) * n_piece // units))
            while len(zg_pieces) < want:
                gate_piece(len(zg_pieces))
            u += 1
            init = jnp.broadcast_to(ccb_ref[:, lanes], (CONV_ROWS, LANES))
            init = init + jnp.tile(after(zg_pieces[-1]), (CONV_ROWS // SUBLANES, 1))
            conv.append(_causal_dwconv_chunk(a_ext, row0, CONV_HALO, ccw_ref, kc, lanes, init))
        conv = jnp.concatenate(conv, axis=-1)
        ln = _layer_norm(conv, clg_ref[...], clb_ref[...])
        act_a[row0:row0 + CONV_ROWS, :] = (ln * _sigmoid(ln)).astype(BF16)
        sconv = []
        for cg in range(ds // LANES):
            lanes = slice(cg * LANES, (cg + 1) * LANES)
            init = jnp.zeros((CONV_ROWS, LANES), F32)
            sconv.append(_causal_dwconv_chunk(u_ext, row0, SC_HALO, scw_ref, ks, lanes, init))
        sconv = jnp.concatenate(sconv, axis=-1)
        act_b[row0:row0 + CONV_ROWS, :] = (scb_buf[row0:row0 + CONV_ROWS, :] * sconv).astype(BF16)
    zg = zg_buf[...]

    y_a = _dot(act_a[...], wa_ref[...])
    y_b = _dot(act_b[...], wb_ref[...])
    m = _sigmoid(zg[:, :d]) * y_a + _sigmoid(zg[:, d:]) * y_b
    mix = _dot(m.astype(BF16), wo_ref[...])
    out = _layer_norm(alpha * x + mix, g1_ref[...], b1_ref[...])
    o_ref[0] = out
    if n_exp:
        info, counts = _route(out, wr_ref, br_ref, upper_ref, n_exp)
        info_ref[...] = info
        cnt_ref[...] = jnp.broadcast_to(counts, cnt_ref.shape)
    else:
        info_ref[...] = jnp.zeros(info_ref.shape, F32)
        cnt_ref[...] = jnp.zeros(cnt_ref.shape, F32)


def _mixer_call(x, w_in, b_in, ccw, ccb, clg, clb, wa, scw, wb, wo, g1, b1, w_router, b_router,
                side, *, alpha, ts):
    bsz, seq, d = x.shape
    kc, dc = ccw.shape
    ks, ds = scw.shape
    assert seq % ts == 0 and ts % CONV_ROWS == 0 and ts >= CONV_HALO
    assert kc - 1 <= CONV_HALO and ks - 1 <= SC_HALO
    assert dc % LANES == 0 and ds % LANES == 0 and (2 * d) % MXU_COLS == 0
    n_exp = 0 if w_router is None else w_router.shape[1]
    assert n_exp <= SUBLANES and ts % LANES == 0 and ts < 2 ** 24
    wr = jnp.zeros((d, LANES), BF16)
    br = jnp.full((1, LANES), NEG_BIG, F32)
    if n_exp:
        wr = wr.at[:, :n_exp].set(w_router.astype(BF16))
        br = br.at[0, :n_exp].set(b_router)
    upper = jnp.triu(jnp.ones((ts, ts), BF16), 1)
    kern = functools.partial(_mixer_kernel, alpha=alpha, ts=ts, dc=dc, ds=ds, kc=kc, ks=ks, n_exp=n_exp)
    consts = (w_in, b_in, ccw, ccb, clg, clb, wa, scw, wb, wo, g1, b1, wr, br, upper)
    tiles = seq // ts
    side_in, side_out, side_shape = _side_cast_specs(side, bsz * tiles, lambda b, s: b * tiles + s)
    return pl.pallas_call(
        kern,
        grid=(bsz, tiles),
        in_specs=[pl.BlockSpec((1, ts, d), lambda b, s: (b, s, 0))]
        + [_const_spec(c.shape) for c in consts] + [side_in],
        out_specs=[pl.BlockSpec((1, ts, d), lambda b, s: (b, s, 0)), side_out,
                   pl.BlockSpec((SUBLANES, ts), lambda b, s: (0, b * tiles + s)),
                   pl.BlockSpec((SUBLANES, LANES), lambda b, s: (b * tiles + s, 0))],
        out_shape=[jax.ShapeDtypeStruct((bsz, seq, d), F32), side_shape,
                   jax.ShapeDtypeStruct((SUBLANES, bsz * seq), F32),
                   jax.ShapeDtypeStruct((bsz * tiles * SUBLANES, LANES), F32)],
        scratch_shapes=[
            pltpu.VMEM((ts + CONV_HALO, dc), F32),
            pltpu.VMEM((ts + SC_HALO, ds), F32),
            pltpu.VMEM((ts, ds), F32),
            pltpu.VMEM((ts, dc), BF16),
            pltpu.VMEM((ts, ds), BF16),
            pltpu.VMEM((ts, 2 * d), F32),
        ],
        compiler_params=pltpu.CompilerParams(
            dimension_semantics=("arbitrary", "arbitrary"), vmem_limit_bytes=VMEM_LIMIT),
        name="mixer",
    )(x, *consts, side)


def _ple(xb, p, wpg_ref, bpg_ref, wpp_ref):
    gate = _sigmoid(_dot(xb, wpg_ref[...]) + bpg_ref[...])
    return gate * _dot(p.astype(BF16), wpp_ref[...])


def _dense_ffn_kernel(x_ref, p_ref, wg_ref, wu_ref, wd_ref, wpg_ref, bpg_ref, wpp_ref,
                      g2_ref, b2_ref, side_ref, o_ref, side_out_ref, *, alpha):
    side_out_ref[...] = side_ref[...].astype(BF16)
    x = x_ref[...]
    xb = x.astype(BF16)
    acc = alpha * x + _ple(xb, p_ref[...], wpg_ref, bpg_ref, wpp_ref)
    g = _dot(xb, wg_ref[...])
    u = _dot(xb, wu_ref[...])
    h = (g * _sigmoid(g) * u).astype(BF16)
    acc = acc + _dot(h, wd_ref[...])
    o_ref[...] = _layer_norm(acc, g2_ref[...], b2_ref[...])


def _dense_ffn_call(x, p, layer, wg, wu, wd, wpg, bpg, wpp, g2, b2, side, *, alpha, tm):
    t, d = x.shape
    assert t % tm == 0
    consts = (wg, wu, wd, wpg, bpg, wpp, g2, b2)
    side_in, side_out, side_shape = _side_cast_specs(side, t // tm, lambda i: i)
    return pl.pallas_call(
        functools.partial(_dense_ffn_kernel, alpha=alpha),
        grid=(t // tm,),
        in_specs=[pl.BlockSpec((tm, d), lambda i: (i, 0)),
                  pl.BlockSpec((None, tm, p.shape[2]), lambda i: (layer, i, 0))]
        + [_const_spec(c.shape) for c in consts] + [side_in],
        out_specs=[pl.BlockSpec((tm, d), lambda i: (i, 0)), side_out],
        out_shape=[jax.ShapeDtypeStruct((t, d), F32), side_shape],
        compiler_params=pltpu.CompilerParams(
            dimension_semantics=("arbitrary",), vmem_limit_bytes=VMEM_LIMIT),
        name="dense_ffn",
    )(x, p, *consts, side)


ROW_CHUNKS = (512, 256, 128, 64, 32, 16, 8)


def _for_row_chunks(n, fn):
    off = 0
    for rows in ROW_CHUNKS:
        take = (n & rows) != 0
        pl.when(take)(functools.partial(fn, off, rows))
        off = off + jnp.where(take, rows, 0)


TOTAL_CHUNKS = (1024,) + ROW_CHUNKS


def _wait_rows(n, wait_fn):
    for rows in TOTAL_CHUNKS:
        pl.when((n & rows) != 0)(functools.partial(wait_fn, rows))


def _local_slot(expert, rank, loff_ref, base, n_exp):
    off = jnp.zeros(expert.shape, F32)
    for e in range(n_exp):
        off = jnp.where(expert == float(e), loff_ref[base + e].astype(F32), off)
    return (off + rank).astype(jnp.int32)


def _dispatch_kernel(n8_ref, loff_ref, gdst_ref, gap_ref, x_ref, info_ref, xs_ref,
                     comp, zbuf, sems, *, tm, n_exp):
    i = pl.program_id(0)
    n_steps = pl.num_programs(0)
    slot = i % 2
    nc = comp.shape[1]

    def group_copy(step, e, s, off, rows):
        src = comp.at[s, pl.ds(pl.multiple_of(loff_ref[step * n_exp + e] + off, SUBLANES), rows)]
        dst = xs_ref.at[pl.ds(pl.multiple_of(gdst_ref[step * n_exp + e] + off, SUBLANES), rows)]
        return pltpu.make_async_copy(src, dst, sems.at[s])

    def start_groups(step, s):
        for e in range(n_exp):
            _for_row_chunks(n8_ref[step * n_exp + e],
                            lambda off, rows, e=e: group_copy(step, e, s, off, rows).start())

    def wait_groups(step, s):
        total = n8_ref[step * n_exp]
        for e in range(1, n_exp):
            total = total + n8_ref[step * n_exp + e]
        _wait_rows(total, lambda rows: pltpu.make_async_copy(
            comp.at[s, pl.ds(0, rows)], xs_ref.at[pl.ds(0, rows)], sems.at[s]).wait())

    info_t = info_ref[...]
    slot_row = lax.broadcasted_iota(jnp.int32, (nc, tm), 0)
    hit = None
    for k in range(TOP_K):
        s_k = _local_slot(info_t[k:k + 1, :], info_t[TOP_K + k:TOP_K + k + 1, :],
                          loff_ref, i * n_exp, n_exp)
        hit = (slot_row == s_k) if hit is None else hit | (slot_row == s_k)
    onehot = jnp.where(hit, 1.0, 0.0).astype(BF16)
    comp[slot] = _dot(onehot, x_ref[...].astype(BF16))

    @pl.when(i > 0)
    def _():
        wait_groups(i - 1, 1 - slot)

    start_groups(i, slot)

    @pl.when(i == n_steps - 1)
    def _():
        wait_groups(i, slot)
        zbuf[...] = jnp.zeros(zbuf.shape, F32)

        def gap_copy(e, off, rows):
            dst = xs_ref.at[pl.ds(pl.multiple_of(gap_ref[2 * e] + off, SUBLANES), rows)]
            return pltpu.make_async_copy(zbuf.at[pl.ds(0, rows)], dst, sems.at[0])

        for e in range(n_exp):
            _for_row_chunks(gap_ref[2 * e + 1],
                            lambda off, rows, e=e: gap_copy(e, off, rows).start())
        for e in range(n_exp):
            _for_row_chunks(gap_ref[2 * e + 1],
                            lambda off, rows, e=e: gap_copy(e, off, rows).wait())

        tail_rows = zbuf.shape[0]

        def tail_copy(j):
            start = pl.multiple_of(gap_ref[2 * n_exp] + j * tail_rows, SUBLANES)
            return pltpu.make_async_copy(zbuf, xs_ref.at[pl.ds(start, tail_rows)], sems.at[0])

        def tail(j, c):
            tail_copy(j).start()
            tail_copy(j).wait()
            return c

        lax.fori_loop(0, gap_ref[2 * n_exp + 1], tail, 0)


def _dispatch_call(n8, loff, gdst, gap, x, info, *, tm, tme, n_exp, n_rows):
    t, d = x.shape
    nc = TOP_K * tm + SUBLANES * n_exp
    return pl.pallas_call(
        functools.partial(_dispatch_kernel, tm=tm, n_exp=n_exp),
        grid_spec=pltpu.PrefetchScalarGridSpec(
            num_scalar_prefetch=4,
            grid=(t // tm,),
            in_specs=[pl.BlockSpec((tm, d), lambda i, *_: (i, 0)),
                      pl.BlockSpec((SUBLANES, tm), lambda i, *_: (0, i))],
            out_specs=pl.BlockSpec(memory_space=pl.ANY),
            scratch_shapes=[pltpu.VMEM((2, nc, d), F32),
                            pltpu.VMEM((tme, d), F32),
                            pltpu.SemaphoreType.DMA((2,))],
        ),
        out_shape=jax.ShapeDtypeStruct((n_rows, d), F32),
        compiler_params=pltpu.CompilerParams(
            dimension_semantics=("arbitrary",), vmem_limit_bytes=VMEM_LIMIT),
        name="dispatch",
    )(n8, loff, gdst, gap, x, info)


def _expert_kernel(tile_e_ref, tile_n_ref, xs_ref, wg_ref, wu_ref, wd_ref, ys_ref, *, fc):
    del tile_e_ref
    i = pl.program_id(0)

    @pl.when(tile_n_ref[i] > 0)
    def _():
        xb = xs_ref[...].astype(BF16)
        f = wg_ref.shape[2]
        acc = jnp.zeros(ys_ref.shape, F32)
        for c in range(f // fc):
            cols = slice(c * fc, (c + 1) * fc)
            g = _dot(xb, wg_ref[0, :, cols])
            u = _dot(xb, wu_ref[0, :, cols])
            h = (g * _sigmoid(g) * u).astype(BF16)
            acc = acc + _dot(h, wd_ref[0, cols, :])
        ys_ref[...] = acc

    @pl.when(tile_n_ref[i] == 0)
    def _():
        ys_ref[...] = jnp.zeros(ys_ref.shape, F32)


def _expert_call(tile_e, tile_n, xs, wg, wu, wd, *, tme, fc):
    n_rows, d = xs.shape
    f = wg.shape[2]
    assert n_rows % tme == 0 and f % fc == 0 and fc % LANES == 0

    def w_spec(shape):
        return pl.BlockSpec((1,) + shape[1:], lambda i, te, tn: (te[i], 0, 0))

    return pl.pallas_call(
        functools.partial(_expert_kernel, fc=fc),
        grid_spec=pltpu.PrefetchScalarGridSpec(
            num_scalar_prefetch=2,
            grid=(n_rows // tme,),
            in_specs=[pl.BlockSpec((tme, d), lambda i, te, tn: (i, 0)),
                      w_spec(wg.shape), w_spec(wu.shape), w_spec(wd.shape)],
            out_specs=pl.BlockSpec((tme, d), lambda i, te, tn: (i, 0)),
        ),
        out_shape=jax.ShapeDtypeStruct((n_rows, d), F32),
        compiler_params=pltpu.CompilerParams(
            dimension_semantics=("arbitrary",), vmem_limit_bytes=VMEM_LIMIT),
        name="experts",
    )(tile_e, tile_n, xs, wg, wu, wd)


def _combine_kernel(n8_ref, loff_ref, gdst_ref, x_ref, p_ref, info_ref, ys_ref, wpg_ref, bpg_ref,
                    wpp_ref, g2_ref, b2_ref, o_ref, ycomp, sems, *, alpha, tm, n_exp):
    i = pl.program_id(0)
    n_steps = pl.num_programs(0)
    slot = i % 2
    nc = ycomp.shape[1]

    def group_copy(step, e, s, off, rows):
        src = ys_ref.at[pl.ds(pl.multiple_of(gdst_ref[step * n_exp + e] + off, SUBLANES), rows)]
        dst = ycomp.at[s, pl.ds(pl.multiple_of(loff_ref[step * n_exp + e] + off, SUBLANES), rows)]
        return pltpu.make_async_copy(src, dst, sems.at[s])

    def start_groups(step, s):
        for e in range(n_exp):
            _for_row_chunks(n8_ref[step * n_exp + e],
                            lambda off, rows, e=e: group_copy(step, e, s, off, rows).start())

    def wait_groups(step, s):
        total = n8_ref[step * n_exp]
        for e in range(1, n_exp):
            total = total + n8_ref[step * n_exp + e]
        _wait_rows(total, lambda rows: pltpu.make_async_copy(
            ys_ref.at[pl.ds(0, rows)], ycomp.at[s, pl.ds(0, rows)], sems.at[s]).wait())

    @pl.when(i == 0)
    def _():
        ycomp[...] = jnp.zeros(ycomp.shape, F32)
        start_groups(0, 0)

    @pl.when(i + 1 < n_steps)
    def _():
        start_groups(i + 1, 1 - slot)

    x = x_ref[...]
    acc = alpha * x + _ple(x.astype(BF16), p_ref[...], wpg_ref, bpg_ref, wpp_ref)

    info_t = info_ref[...]
    info = jnp.concatenate([info_t, jnp.zeros((LANES - SUBLANES, tm), F32)], axis=0).T
    slot_col = lax.broadcasted_iota(jnp.int32, (tm, nc), 1)
    pick = jnp.zeros((tm, nc), F32)
    for k in range(TOP_K):
        s_k = _local_slot(info[:, k:k + 1], info[:, TOP_K + k:TOP_K + k + 1],
                          loff_ref, i * n_exp, n_exp)
        pick = jnp.where(slot_col == s_k, info[:, 4 + k:5 + k], pick)
    pick = pick.astype(BF16)

    wait_groups(i, slot)
    acc = acc + _dot(pick, ycomp[slot].astype(BF16))
    o_ref[...] = _layer_norm(acc, g2_ref[...], b2_ref[...])


def _combine_call(n8, loff, gdst, x, p, layer, info, ys, wpg, bpg, wpp, g2, b2, *, alpha, tm, n_exp):
    t, d = x.shape
    nc = TOP_K * tm + SUBLANES * n_exp
    consts = (wpg, bpg, wpp, g2, b2)

    def const_spec(shape):
        nd = len(shape)
        return pl.BlockSpec(shape, lambda i, *_: (0,) * nd, pipeline_mode=pl.Buffered(1))

    return pl.pallas_call(
        functools.partial(_combine_kernel, alpha=alpha, tm=tm, n_exp=n_exp),
        grid_spec=pltpu.PrefetchScalarGridSpec(
            num_scalar_prefetch=3,
            grid=(t // tm,),
            in_specs=[pl.BlockSpec((tm, d), lambda i, *_: (i, 0)),
                      pl.BlockSpec((None, tm, p.shape[2]), lambda i, *_: (layer, i, 0)),
                      pl.BlockSpec((SUBLANES, tm), lambda i, *_: (0, i)),
                      pl.BlockSpec(memory_space=pl.ANY)]
            + [const_spec(c.shape) for c in consts],
            out_specs=pl.BlockSpec((tm, d), lambda i, *_: (i, 0)),
            scratch_shapes=[pltpu.VMEM((2, nc, d), F32), pltpu.SemaphoreType.DMA((2,))],
        ),
        out_shape=jax.ShapeDtypeStruct((t, d), F32),
        compiler_params=pltpu.CompilerParams(
            dimension_semantics=("arbitrary",), vmem_limit_bytes=VMEM_LIMIT),
        name="combine",
    )(n8, loff, gdst, x, p, info, ys, *consts)


def _moe_layer(x, p, layer, info, cnt, n_exp, we_gate, we_up, we_down, wpg, bpg, wpp, g2, b2,
               *, alpha, tm, tme, fc):
    t, d = x.shape
    n_tok_tiles = t // tm
    assert tm <= ROW_CHUNKS[0] and tme <= ROW_CHUNKS[0]

    i32 = jnp.int32
    n = cnt[:, 0].reshape(n_tok_tiles, SUBLANES)[:, :n_exp].astype(i32)
    n8 = (n + SUBLANES - 1) // SUBLANES * SUBLANES
    loff = jnp.cumsum(n8, axis=1) - n8
    tot = jnp.sum(n8, axis=0)
    reg = (tot + tme - 1) // tme * tme
    reg_end = jnp.cumsum(reg)
    base = reg_end - reg
    gdst = base[None, :] + jnp.cumsum(n8, axis=0) - n8
    n_tiles = (t * TOP_K + n_tok_tiles * n_exp * (SUBLANES - 1)) // tme + n_exp + 1
    n_rows = n_tiles * tme
    tile_start = jnp.arange(n_tiles, dtype=i32) * tme
    tile_e = jnp.minimum(jnp.sum(tile_start[:, None] >= reg_end[None, :], axis=1), n_exp - 1).astype(i32)
    tile_n = jnp.clip(tot[tile_e] - (tile_start - base[tile_e]), 0, tme).astype(i32)
    gap = jnp.stack([base + tot, reg - tot], axis=1).reshape(-1)
    gap = jnp.concatenate([gap, jnp.stack([reg_end[-1], (n_rows - reg_end[-1]) // tme])]).astype(i32)
    n8f, lofff, gdstf = (a.reshape(-1).astype(i32) for a in (n8, loff, gdst))

    xs = _dispatch_call(n8f, lofff, gdstf, gap, x, info, tm=tm, tme=tme, n_exp=n_exp, n_rows=n_rows)
    ys = _expert_call(tile_e, tile_n, xs, we_gate, we_up, we_down, tme=tme, fc=fc)
    return _combine_call(n8f, lofff, gdstf, x, p, layer, info, ys, wpg, bpg, wpp, g2, b2,
                         alpha=alpha, tm=tm, n_exp=n_exp)


def _row(v):
    return v.reshape(1, -1)


def kernel(x, p, w_in, b_in, conf_conv_w, conf_conv_b, conf_ln_g, conf_ln_b, w_conf_out,
           sc_conv_w, w_sc_out, w_o, ln1_g, ln1_b, w_ff_gate, w_ff_up, w_ff_down, w_router,
           b_router, we_gate, we_up, we_down, w_ple_gate, b_ple_gate, w_ple_proj, ln2_g, ln2_b,
           ):
    ts, tm, tme, fc_moe = TILES
    depth = w_in.shape[0]
    alpha = (2 * depth) ** 0.25
    bsz, seq, d = x.shape
    pt = p.reshape(depth, bsz * seq, -1)

    expert_w = (("gate", we_gate), ("up", we_up), ("down", we_down))
    queue = [((i // 2, name), w[i // 2], i) for i in range(depth) if i % 2 == 1 for name, w in expert_w]
    cast = {}
    idle = jnp.zeros((bsz * seq // min(ts, tm) * 2 * SUBLANES, LANES), F32)

    def side_job(layer, is_mixer):
        for k, (key, w, needed_in) in enumerate(queue):
            if layer < needed_in or (layer == needed_in and is_mixer):
                del queue[k]
                return key, w
        return None, idle

    def run(call, *args, layer, is_mixer, **kw):
        key, w = side_job(layer, is_mixer)
        out, w_bf16, *rest = call(*args, w.reshape(-1, w.shape[-1]), **kw)
        if key is not None:
            cast[key] = w_bf16.reshape(w.shape)
        return (out, *rest) if rest else out

    assert ts == tm
    for i in range(depth):
        moe = i % 2 == 1
        x, info, cnt = run(
            _mixer_call,
            x, w_in[i].astype(BF16), _row(b_in[i]), conf_conv_w[i], _row(conf_conv_b[i]),
            _row(conf_ln_g[i]), _row(conf_ln_b[i]), w_conf_out[i].astype(BF16), sc_conv_w[i],
            w_sc_out[i].astype(BF16), w_o[i].astype(BF16), _row(ln1_g[i]), _row(ln1_b[i]),
            w_router[i // 2] if moe else None, b_router[i // 2] if moe else None,
            layer=i, is_mixer=True, alpha=alpha, ts=ts)
        xt = x.reshape(bsz * seq, d)
        ple_w = (w_ple_gate[i].astype(BF16), _row(b_ple_gate[i]), w_ple_proj[i].astype(BF16),
                 _row(ln2_g[i]), _row(ln2_b[i]))
        j = i // 2
        if i % 2 == 0:
            xt = run(_dense_ffn_call,
                     xt, pt, i, w_ff_gate[j].astype(BF16), w_ff_up[j].astype(BF16),
                     w_ff_down[j].astype(BF16), *ple_w, layer=i, is_mixer=False, alpha=alpha, tm=tm)
        else:
            queue[:] = [q for q in queue if q[2] != i]
            ew = [cast.pop((j, name)) if (j, name) in cast else w[j].astype(BF16)
                  for name, w in expert_w]
            xt = _moe_layer(xt, pt, i, info, cnt, w_router.shape[2], *ew,
                            *ple_w, alpha=alpha, tm=tm, tme=tme, fc=fc_moe)
        x = xt.reshape(bsz, seq, d)
    return x
```

```python
import functools
from typing import NamedTuple

import jax
import jax.numpy as jnp
from jax import lax
from jax.experimental import pallas as pl
from jax.experimental.pallas import tpu as pltpu

LN_EPS = 1e-5
TOP_K = 2
LANES = 128
SUBLANES = 8
CONV_HALO = 32
SC_HALO = 8
MXU_COLS = 256
CONV_ROWS = 64
NEG_BIG = -1e30
VMEM_LIMIT = 56 * 1024 * 1024

F32 = jnp.float32
BF16 = jnp.bfloat16


class Tiles(NamedTuple):
    seq_rows: int = 512
    token_rows: int = 512
    expert_rows: int = 512
    expert_cols: int = 512


TILES = Tiles()


def _dot(a, b):
    return jnp.dot(a, b, preferred_element_type=F32)


def _sigmoid(x):
    return 1.0 / (1.0 + jnp.exp(-x))


def _layer_norm(x, g, b):
    mu = jnp.mean(x, axis=-1, keepdims=True)
    xc = x - mu
    var = jnp.mean(xc * xc, axis=-1, keepdims=True)
    return xc * lax.rsqrt(var + LN_EPS) * g + b


def _const_spec(shape):
    nd = len(shape)
    return pl.BlockSpec(shape, lambda *_: (0,) * nd, pipeline_mode=pl.Buffered(1))


def _route(x_tile, wr_ref, br_ref, upper_ref, n_exp):
    logits = _dot(x_tile.astype(BF16), wr_ref[...]) + br_ref[...]
    lt = logits.T
    tm = lt.shape[1]
    row = [lt[e:e + 1, :] for e in range(n_exp)]
    zero = jnp.zeros((1, tm), F32)
    m1, i1, m2, i2 = row[0], zero, jnp.full((1, tm), 2.0 * NEG_BIG, F32), zero
    for e in range(1, n_exp):
        first = row[e] > m1
        second = row[e] > m2
        m2 = jnp.where(first, m1, jnp.where(second, row[e], m2))
        i2 = jnp.where(first, i1, jnp.where(second, float(e), i2))
        m1 = jnp.where(first, row[e], m1)
        i1 = jnp.where(first, float(e), i1)
    e21 = jnp.exp(m2 - m1)
    w1 = 1.0 / (1.0 + e21)
    w2 = e21 / (1.0 + e21)

    expert = lax.broadcasted_iota(jnp.int32, (SUBLANES, tm), 0).astype(F32)
    hit1 = expert == i1
    hit2 = expert == i2
    onehot = jnp.where(hit1 | hit2, 1.0, 0.0)
    ranks = _dot(onehot.astype(BF16), upper_ref[...])
    r1 = jnp.sum(jnp.where(hit1, ranks, 0.0), axis=0, keepdims=True)
    r2 = jnp.sum(jnp.where(hit2, ranks, 0.0), axis=0, keepdims=True)
    info = jnp.concatenate([i1, i2, r1, r2, w1, w2, zero, zero], axis=0)
    return info, jnp.sum(onehot, axis=1, keepdims=True)


def _side_cast_specs(side, n_steps, step_index):
    rows, cols = side.shape
    assert rows % (n_steps * 2 * SUBLANES) == 0
    block = (rows // n_steps, cols)
    spec = pl.BlockSpec(block, lambda *idx: (step_index(*idx), 0))
    return spec, spec, jax.ShapeDtypeStruct(side.shape, BF16)


def _causal_dwconv_chunk(src_ref, row0, halo, w_ref, taps, lanes, init):
    acc = init
    first = halo - (taps - 1)
    for r in range(SUBLANES):
        offs = [o for o in range(first, halo + 1) if o % SUBLANES == r]
        if not offs:
            continue
        base = offs[0]
        span = offs[-1] - base + CONV_ROWS
        blk = src_ref[row0 + base:row0 + base + span, lanes]
        part = None
        for o in offs:
            j = o - first
            term = w_ref[j:j + 1, lanes] * blk[o - base:o - base + CONV_ROWS, :]
            part = term if part is None else part + term
        acc = acc + part
    return acc


def _mixer_kernel(x_ref, w_in_ref, b_in_ref, ccw_ref, ccb_ref, clg_ref, clb_ref, wa_ref,
                  scw_ref, wb_ref, wo_ref, g1_ref, b1_ref, wr_ref, br_ref, upper_ref, side_ref,
                  o_ref, side_out_ref, info_ref, cnt_ref,
                  a_ext, u_ext, scb_buf, act_a, act_b, zg_buf, *, alpha, ts, dc, ds, kc, ks, n_exp):
    s = pl.program_id(1)
    side_out_ref[...] = side_ref[...].astype(BF16)

    @pl.when(s == 0)
    def _():
        a_ext[0:CONV_HALO, :] = jnp.zeros((CONV_HALO, dc), F32)
        u_ext[0:SC_HALO, :] = jnp.zeros((SC_HALO, ds), F32)

    @pl.when(s > 0)
    def _():
        a_ext[0:CONV_HALO, :] = a_ext[ts:ts + CONV_HALO, :]
        u_ext[0:SC_HALO, :] = u_ext[ts:ts + SC_HALO, :]

    x = x_ref[0]
    xb = x.astype(BF16)

    c0 = 2 * dc
    c1 = c0 + 3 * ds
    zc = _dot(xb, w_in_ref[:, 0:c0]) + b_in_ref[:, 0:c0]
    a_ext[CONV_HALO:CONV_HALO + ts, :] = zc[:, :dc] * _sigmoid(zc[:, dc:])
    zs = _dot(xb, w_in_ref[:, c0:c1]) + b_in_ref[:, c0:c1]
    scb_buf[...] = zs[:, :ds]
    u_ext[SC_HALO:SC_HALO + ts, :] = zs[:, ds:2 * ds] * zs[:, 2 * ds:]

    d = x.shape[-1]
    n_piece = 2 * d // MXU_COLS
    units = (ts // CONV_ROWS) * (dc // LANES)
    zg_pieces = []

    def gate_piece(j):
        cols = slice(c1 + j * MXU_COLS, c1 + (j + 1) * MXU_COLS)
        z = _dot(xb, w_in_ref[:, cols]) + b_in_ref[:, cols]
        zg_buf[:, j * MXU_COLS:(j + 1) * MXU_COLS] = z
        zg_pieces.append(z[0:SUBLANES, 0:LANES])

    def after(piece):
        bits = pltpu.bitcast(piece, jnp.uint32)
        return pltpu.bitcast((bits >> 16) >> 16, F32)

    u = 0
    for ci in range(ts // CONV_ROWS):
        row0 = ci * CONV_ROWS
        conv = []
        for cg in range(dc // LANES):
            lanes = slice(cg * LANES, (cg + 1) * LANES)
            want = min(n_piece, -(-(u + 1) * n_piece // units))
            while len(zg_pieces) < want:
                gate_piece(len(zg_pieces))
            u += 1
            init = jnp.broadcast_to(ccb_ref[:, lanes], (CONV_ROWS, LANES))
            init = init + jnp.tile(after(zg_pieces[-1]), (CONV_ROWS // SUBLANES, 1))
            conv.append(_causal_dwconv_chunk(a_ext, row0, CONV_HALO, ccw_ref, kc, lanes, init))
        conv = jnp.concatenate(conv, axis=-1)
        ln = _layer_norm(conv, clg_ref[...], clb_ref[...])
        act_a[row0:row0 + CONV_ROWS, :] = (ln * _sigmoid(ln)).astype(BF16)
        sconv = []
        for cg in range(ds // LANES):
            lanes = slice(cg * LANES, (cg + 1) * LANES)
            init = jnp.zeros((CONV_ROWS, LANES), F32)
            sconv.append(_causal_dwconv_chunk(u_ext, row0, SC_HALO, scw_ref, ks, lanes, init))
        sconv = jnp.concatenate(sconv, axis=-1)
        act_b[row0:row0 + CONV_ROWS, :] = (scb_buf[row0:row0 + CONV_ROWS, :] * sconv).astype(BF16)
    zg = zg_buf[...]

    y_a = _dot(act_a[...], wa_ref[...])
    y_b = _dot(act_b[...], wb_ref[...])
    m = _sigmoid(zg[:, :d]) * y_a + _sigmoid(zg[:, d:]) * y_b
    mix = _dot(m.astype(BF16), wo_ref[...])
    out = _layer_norm(alpha * x + mix, g1_ref[...], b1_ref[...])
    o_ref[0] = out
    if n_exp:
        info, counts = _route(out, wr_ref, br_ref, upper_ref, n_exp)
        info_ref[...] = info
        cnt_ref[...] = jnp.broadcast_to(counts, cnt_ref.shape)
    else:
        info_ref[...] = jnp.zeros(info_ref.shape, F32)
        cnt_ref[...] = jnp.zeros(cnt_ref.shape, F32)


def _mixer_call(x, w_in, b_in, ccw, ccb, clg, clb, wa, scw, wb, wo, g1, b1, w_router, b_router,
                side, *, alpha, ts):
    bsz, seq, d = x.shape
    kc, dc = ccw.shape
    ks, ds = scw.shape
    assert seq % ts == 0 and ts % CONV_ROWS == 0 and ts >= CONV_HALO
    assert kc - 1 <= CONV_HALO and ks - 1 <= SC_HALO
    assert dc % LANES == 0 and ds % LANES == 0 and (2 * d) % MXU_COLS == 0
    n_exp = 0 if w_router is None else w_router.shape[1]
    assert n_exp <= SUBLANES and ts % LANES == 0 and ts < 2 ** 24
    wr = jnp.zeros((d, LANES), BF16)
    br = jnp.full((1, LANES), NEG_BIG, F32)
    if n_exp:
        wr = wr.at[:, :n_exp].set(w_router.astype(BF16))
        br = br.at[0, :n_exp].set(b_router)
    upper = jnp.triu(jnp.ones((ts, ts), BF16), 1)
    kern = functools.partial(_mixer_kernel, alpha=alpha, ts=ts, dc=dc, ds=ds, kc=kc, ks=ks, n_exp=n_exp)
    consts = (w_in, b_in, ccw, ccb, clg, clb, wa, scw, wb, wo, g1, b1, wr, br, upper)
    tiles = seq // ts
    side_in, side_out, side_shape = _side_cast_specs(side, bsz * tiles, lambda b, s: b * tiles + s)
    return pl.pallas_call(
        kern,
        grid=(bsz, tiles),
        in_specs=[pl.BlockSpec((1, ts, d), lambda b, s: (b, s, 0))]
        + [_const_spec(c.shape) for c in consts] + [side_in],
        out_specs=[pl.BlockSpec((1, ts, d), lambda b, s: (b, s, 0)), side_out,
                   pl.BlockSpec((SUBLANES, ts), lambda b, s: (0, b * tiles + s)),
                   pl.BlockSpec((SUBLANES, LANES), lambda b, s: (b * tiles + s, 0))],
        out_shape=[jax.ShapeDtypeStruct((bsz, seq, d), F32), side_shape,
                   jax.ShapeDtypeStruct((SUBLANES, bsz * seq), F32),
                   jax.ShapeDtypeStruct((bsz * tiles * SUBLANES, LANES), F32)],
        scratch_shapes=[
            pltpu.VMEM((ts + CONV_HALO, dc), F32),
            pltpu.VMEM((ts + SC_HALO, ds), F32),
            pltpu.VMEM((ts, ds), F32),
            pltpu.VMEM((ts, dc), BF16),
            pltpu.VMEM((ts, ds), BF16),
            pltpu.VMEM((ts, 2 * d), F32),
        ],
        compiler_params=pltpu.CompilerParams(
            dimension_semantics=("arbitrary", "arbitrary"), vmem_limit_bytes=VMEM_LIMIT),
        name="mixer",
    )(x, *consts, side)


def _ple(xb, p, wpg_ref, bpg_ref, wpp_ref):
    gate = _sigmoid(_dot(xb, wpg_ref[...]) + bpg_ref[...])
    return gate * _dot(p.astype(BF16), wpp_ref[...])


def _dense_ffn_kernel(x_ref, p_ref, wg_ref, wu_ref, wd_ref, wpg_ref, bpg_ref, wpp_ref,
                      g2_ref, b2_ref, side_ref, o_ref, side_out_ref, *, alpha):
    side_out_ref[...] = side_ref[...].astype(BF16)
    x = x_ref[...]
    xb = x.astype(BF16)
    acc = alpha * x + _ple(xb, p_ref[...], wpg_ref, bpg_ref, wpp_ref)
    g = _dot(xb, wg_ref[...])
    u = _dot(xb, wu_ref[...])
    h = (g * _sigmoid(g) * u).astype(BF16)
    acc = acc + _dot(h, wd_ref[...])
    o_ref[...] = _layer_norm(acc, g2_ref[...], b2_ref[...])


def _dense_ffn_call(x, p, layer, wg, wu, wd, wpg, bpg, wpp, g2, b2, side, *, alpha, tm):
    t, d = x.shape
    assert t % tm == 0
    consts = (wg, wu, wd, wpg, bpg, wpp, g2, b2)
    side_in, side_out, side_shape = _side_cast_specs(side, t // tm, lambda i: i)
    return pl.pallas_call(
        functools.partial(_dense_ffn_kernel, alpha=alpha),
        grid=(t // tm,),
        in_specs=[pl.BlockSpec((tm, d), lambda i: (i, 0)),
                  pl.BlockSpec((None, tm, p.shape[2]), lambda i: (layer, i, 0))]
        + [_const_spec(c.shape) for c in consts] + [side_in],
        out_specs=[pl.BlockSpec((tm, d), lambda i: (i, 0)), side_out],
        out_shape=[jax.ShapeDtypeStruct((t, d), F32), side_shape],
        compiler_params=pltpu.CompilerParams(
            dimension_semantics=("arbitrary",), vmem_limit_bytes=VMEM_LIMIT),
        name="dense_ffn",
    )(x, p, *consts, side)


ROW_CHUNKS = (512, 256, 128, 64, 32, 16, 8)


def _for_row_chunks(n, fn):
    off = 0
    for rows in ROW_CHUNKS:
        take = (n & rows) != 0
        pl.when(take)(functools.partial(fn, off, rows))
        off = off + jnp.where(take, rows, 0)


TOTAL_CHUNKS = (1024,) + ROW_CHUNKS


def _wait_rows(n, wait_fn):
    for rows in TOTAL_CHUNKS:
        pl.when((n & rows) != 0)(functools.partial(wait_fn, rows))


def _local_slot(expert, rank, loff_ref, base, n_exp):
    off = jnp.zeros(expert.shape, F32)
    for e in range(n_exp):
        off = jnp.where(expert == float(e), loff_ref[base + e].astype(F32), off)
    return (off + rank).astype(jnp.int32)


def _dispatch_kernel(n8_ref, loff_ref, gdst_ref, gap_ref, x_ref, info_ref, xs_ref,
                     comp, zbuf, sems, *, tm, n_exp):
    i = pl.program_id(0)
    n_steps = pl.num_programs(0)
    slot = i % 2
    nc = comp.shape[1]

    def group_copy(step, e, s, off, rows):
        src = comp.at[s, pl.ds(pl.multiple_of(loff_ref[step * n_exp + e] + off, SUBLANES), rows)]
        dst = xs_ref.at[pl.ds(pl.multiple_of(gdst_ref[step * n_exp + e] + off, SUBLANES), rows)]
        return pltpu.make_async_copy(src, dst, sems.at[s])

    def start_groups(step, s):
        for e in range(n_exp):
            _for_row_chunks(n8_ref[step * n_exp + e],
                            lambda off, rows, e=e: group_copy(step, e, s, off, rows).start())

    def wait_groups(step, s):
        total = n8_ref[step * n_exp]
        for e in range(1, n_exp):
            total = total + n8_ref[step * n_exp + e]
        _wait_rows(total, lambda rows: pltpu.make_async_copy(
            comp.at[s, pl.ds(0, rows)], xs_ref.at[pl.ds(0, rows)], sems.at[s]).wait())

    info_t = info_ref[...]
    slot_row = lax.broadcasted_iota(jnp.int32, (nc, tm), 0)
    hit = None
    for k in range(TOP_K):
        s_k = _local_slot(info_t[k:k + 1, :], info_t[TOP_K + k:TOP_K + k + 1, :],
                          loff_ref, i * n_exp, n_exp)
        hit = (slot_row == s_k) if hit is None else hit | (slot_row == s_k)
    onehot = jnp.where(hit, 1.0, 0.0).astype(BF16)
    comp[slot] = _dot(onehot, x_ref[...].astype(BF16))

    @pl.when(i > 0)
    def _():
        wait_groups(i - 1, 1 - slot)

    start_groups(i, slot)

    @pl.when(i == n_steps - 1)
    def _():
        wait_groups(i, slot)
        zbuf[...] = jnp.zeros(zbuf.shape, F32)

        def gap_copy(e, off, rows):
            dst = xs_ref.at[pl.ds(pl.multiple_of(gap_ref[2 * e] + off, SUBLANES), rows)]
            return pltpu.make_async_copy(zbuf.at[pl.ds(0, rows)], dst, sems.at[0])

        for e in range(n_exp):
            _for_row_chunks(gap_ref[2 * e + 1],
                            lambda off, rows, e=e: gap_copy(e, off, rows).start())
        for e in range(n_exp):
            _for_row_chunks(gap_ref[2 * e + 1],
                            lambda off, rows, e=e: gap_copy(e, off, rows).wait())

        tail_rows = zbuf.shape[0]

        def tail_copy(j):
            start = pl.multiple_of(gap_ref[2 * n_exp] + j * tail_rows, SUBLANES)
            return pltpu.make_async_copy(zbuf, xs_ref.at[pl.ds(start, tail_rows)], sems.at[0])

        def tail(j, c):
            tail_copy(j).start()
            tail_copy(j).wait()
            return c

        lax.fori_loop(0, gap_ref[2 * n_exp + 1], tail, 0)


def _dispatch_call(n8, loff, gdst, gap, x, info, *, tm, tme, n_exp, n_rows):
    t, d = x.shape
    nc = TOP_K * tm + SUBLANES * n_exp
    return pl.pallas_call(
        functools.partial(_dispatch_kernel, tm=tm, n_exp=n_exp),
        grid_spec=pltpu.PrefetchScalarGridSpec(
            num_scalar_prefetch=4,
            grid=(t // tm,),
            in_specs=[pl.BlockSpec((tm, d), lambda i, *_: (i, 0)),
                      pl.BlockSpec((SUBLANES, tm), lambda i, *_: (0, i))],
            out_specs=pl.BlockSpec(memory_space=pl.ANY),
            scratch_shapes=[pltpu.VMEM((2, nc, d), F32),
                            pltpu.VMEM((tme, d), F32),
                            pltpu.SemaphoreType.DMA((2,))],
        ),
        out_shape=jax.ShapeDtypeStruct((n_rows, d), F32),
        compiler_params=pltpu.CompilerParams(
            dimension_semantics=("arbitrary",), vmem_limit_bytes=VMEM_LIMIT),
        name="dispatch",
    )(n8, loff, gdst, gap, x, info)


def _expert_kernel(tile_e_ref, tile_n_ref, xs_ref, wg_ref, wu_ref, wd_ref, ys_ref, *, fc):
    del tile_e_ref
    i = pl.program_id(0)

    @pl.when(tile_n_ref[i] > 0)
    def _():
        xb = xs_ref[...].astype(BF16)
        f = wg_ref.shape[2]
        acc = jnp.zeros(ys_ref.shape, F32)
        for c in range(f // fc):
            cols = slice(c * fc, (c + 1) * fc)
            g = _dot(xb, wg_ref[0, :, cols])
            u = _dot(xb, wu_ref[0, :, cols])
            h = (g * _sigmoid(g) * u).astype(BF16)
            acc = acc + _dot(h, wd_ref[0, cols, :])
        ys_ref[...] = acc

    @pl.when(tile_n_ref[i] == 0)
    def _():
        ys_ref[...] = jnp.zeros(ys_ref.shape, F32)


def _expert_call(tile_e, tile_n, xs, wg, wu, wd, *, tme, fc):
    n_rows, d = xs.shape
    f = wg.shape[2]
    assert n_rows % tme == 0 and f % fc == 0 and fc % LANES == 0

    def w_spec(shape):
        return pl.BlockSpec((1,) + shape[1:], lambda i, te, tn: (te[i], 0, 0))

    return pl.pallas_call(
        functools.partial(_expert_kernel, fc=fc),
        grid_spec=pltpu.PrefetchScalarGridSpec(
            num_scalar_prefetch=2,
            grid=(n_rows // tme,),
            in_specs=[pl.BlockSpec((tme, d), lambda i, te, tn: (i, 0)),
                      w_spec(wg.shape), w_spec(wu.shape), w_spec(wd.shape)],
            out_specs=pl.BlockSpec((tme, d), lambda i, te, tn: (i, 0)),
        ),
        out_shape=jax.ShapeDtypeStruct((n_rows, d), F32),
        compiler_params=pltpu.CompilerParams(
            dimension_semantics=("arbitrary",), vmem_limit_bytes=VMEM_LIMIT),
        name="experts",
    )(tile_e, tile_n, xs, wg, wu, wd)


def _combine_kernel(n8_ref, loff_ref, gdst_ref, x_ref, p_ref, info_ref, ys_ref, wpg_ref, bpg_ref,
                    wpp_ref, g2_ref, b2_ref, o_ref, ycomp, sems, *, alpha, tm, n_exp):
    i = pl.program_id(0)
    n_steps = pl.num_programs(0)
    slot = i % 2
    nc = ycomp.shape[1]

    def group_copy(step, e, s, off, rows):
        src = ys_ref.at[pl.ds(pl.multiple_of(gdst_ref[step * n_exp + e] + off, SUBLANES), rows)]
        dst = ycomp.at[s, pl.ds(pl.multiple_of(loff_ref[step * n_exp + e] + off, SUBLANES), rows)]
        return pltpu.make_async_copy(src, dst, sems.at[s])

    def start_groups(step, s):
        for e in range(n_exp):
            _for_row_chunks(n8_ref[step * n_exp + e],
                            lambda off, rows, e=e: group_copy(step, e, s, off, rows).start())

    def wait_groups(step, s):
        total = n8_ref[step * n_exp]
        for e in range(1, n_exp):
            total = total + n8_ref[step * n_exp + e]
        _wait_rows(total, lambda rows: pltpu.make_async_copy(
            ys_ref.at[pl.ds(0, rows)], ycomp.at[s, pl.ds(0, rows)], sems.at[s]).wait())

    @pl.when(i == 0)
    def _():
        ycomp[...] = jnp.zeros(ycomp.shape, F32)
        start_groups(0, 0)

    wait_groups(i, slot)

    @pl.when(i + 1 < n_steps)
    def _():
        start_groups(i + 1, 1 - slot)

    x = x_ref[...]
    acc = alpha * x + _ple(x.astype(BF16), p_ref[...], wpg_ref, bpg_ref, wpp_ref)

    info_t = info_ref[...]
    info = jnp.concatenate([info_t, jnp.zeros((LANES - SUBLANES, tm), F32)], axis=0).T
    slot_col = lax.broadcasted_iota(jnp.int32, (tm, nc), 1)
    pick = jnp.zeros((tm, nc), F32)
    for k in range(TOP_K):
        s_k = _local_slot(info[:, k:k + 1], info[:, TOP_K + k:TOP_K + k + 1],
                          loff_ref, i * n_exp, n_exp)
        pick = jnp.where(slot_col == s_k, info[:, 4 + k:5 + k], pick)
    acc = acc + _dot(pick.astype(BF16), ycomp[slot].astype(BF16))
    o_ref[...] = _layer_norm(acc, g2_ref[...], b2_ref[...])


def _combine_call(n8, loff, gdst, x, p, layer, info, ys, wpg, bpg, wpp, g2, b2, *, alpha, tm, n_exp):
    t, d = x.shape
    nc = TOP_K * tm + SUBLANES * n_exp
    consts = (wpg, bpg, wpp, g2, b2)

    def const_spec(shape):
        nd = len(shape)
        return pl.BlockSpec(shape, lambda i, *_: (0,) * nd, pipeline_mode=pl.Buffered(1))

    return pl.pallas_call(
        functools.partial(_combine_kernel, alpha=alpha, tm=tm, n_exp=n_exp),
        grid_spec=pltpu.PrefetchScalarGridSpec(
            num_scalar_prefetch=3,
            grid=(t // tm,),
            in_specs=[pl.BlockSpec((tm, d), lambda i, *_: (i, 0)),
                      pl.BlockSpec((None, tm, p.shape[2]), lambda i, *_: (layer, i, 0)),
                      pl.BlockSpec((SUBLANES, tm), lambda i, *_: (0, i)),
                      pl.BlockSpec(memory_space=pl.ANY)]
            + [const_spec(c.shape) for c in consts],
            out_specs=pl.BlockSpec((tm, d), lambda i, *_: (i, 0)),
            scratch_shapes=[pltpu.VMEM((2, nc, d), F32), pltpu.SemaphoreType.DMA((2,))],
        ),
        out_shape=jax.ShapeDtypeStruct((t, d), F32),
        compiler_params=pltpu.CompilerParams(
            dimension_semantics=("arbitrary",), vmem_limit_bytes=VMEM_LIMIT),
        name="combine",
    )(n8, loff, gdst, x, p, info, ys, *consts)


def _moe_layer(x, p, layer, info, cnt, n_exp, we_gate, we_up, we_down, wpg, bpg, wpp, g2, b2,
               *, alpha, tm, tme, fc):
    t, d = x.shape
    n_tok_tiles = t // tm
    assert tm <= ROW_CHUNKS[0] and tme <= ROW_CHUNKS[0]

    i32 = jnp.int32
    n = cnt[:, 0].reshape(n_tok_tiles, SUBLANES)[:, :n_exp].astype(i32)
    n8 = (n + SUBLANES - 1) // SUBLANES * SUBLANES
    loff = jnp.cumsum(n8, axis=1) - n8
    tot = jnp.sum(n8, axis=0)
    reg = (tot + tme - 1) // tme * tme
    reg_end = jnp.cumsum(reg)
    base = reg_end - reg
    gdst = base[None, :] + jnp.cumsum(n8, axis=0) - n8
    n_tiles = (t * TOP_K + n_tok_tiles * n_exp * (SUBLANES - 1)) // tme + n_exp + 1
    n_rows = n_tiles * tme
    tile_start = jnp.arange(n_tiles, dtype=i32) * tme
    tile_e = jnp.minimum(jnp.sum(tile_start[:, None] >= reg_end[None, :], axis=1), n_exp - 1).astype(i32)
    tile_n = jnp.clip(tot[tile_e] - (tile_start - base[tile_e]), 0, tme).astype(i32)
    gap = jnp.stack([base + tot, reg - tot], axis=1).reshape(-1)
    gap = jnp.concatenate([gap, jnp.stack([reg_end[-1], (n_rows - reg_end[-1]) // tme])]).astype(i32)
    n8f, lofff, gdstf = (a.reshape(-1).astype(i32) for a in (n8, loff, gdst))

    xs = _dispatch_call(n8f, lofff, gdstf, gap, x, info, tm=tm, tme=tme, n_exp=n_exp, n_rows=n_rows)
    ys = _expert_call(tile_e, tile_n, xs, we_gate, we_up, we_down, tme=tme, fc=fc)
    return _combine_call(n8f, lofff, gdstf, x, p, layer, info, ys, wpg, bpg, wpp, g2, b2,
                         alpha=alpha, tm=tm, n_exp=n_exp)


def _row(v):
    return v.reshape(1, -1)


def kernel(x, p, w_in, b_in, conf_conv_w, conf_conv_b, conf_ln_g, conf_ln_b, w_conf_out,
           sc_conv_w, w_sc_out, w_o, ln1_g, ln1_b, w_ff_gate, w_ff_up, w_ff_down, w_router,
           b_router, we_gate, we_up, we_down, w_ple_gate, b_ple_gate, w_ple_proj, ln2_g, ln2_b,
           ):
    ts, tm, tme, fc_moe = TILES
    depth = w_in.shape[0]
    alpha = (2 * depth) ** 0.25
    bsz, seq, d = x.shape
    pt = p.reshape(depth, bsz * seq, -1)

    expert_w = (("gate", we_gate), ("up", we_up), ("down", we_down))
    queue = [((i // 2, name), w[i // 2], i) for i in range(depth) if i % 2 == 1 for name, w in expert_w]
    cast = {}
    idle = jnp.zeros((bsz * seq // min(ts, tm) * 2 * SUBLANES, LANES), F32)

    def side_job(layer, is_mixer):
        for k, (key, w, needed_in) in enumerate(queue):
            if layer < needed_in or (layer == needed_in and is_mixer):
                del queue[k]
                return key, w
        return None, idle

    def run(call, *args, layer, is_mixer, **kw):
        key, w = side_job(layer, is_mixer)
        out, w_bf16, *rest = call(*args, w.reshape(-1, w.shape[-1]), **kw)
        if key is not None:
            cast[key] = w_bf16.reshape(w.shape)
        return (out, *rest) if rest else out

    assert ts == tm
    for i in range(depth):
        moe = i % 2 == 1
        x, info, cnt = run(
            _mixer_call,
            x, w_in[i].astype(BF16), _row(b_in[i]), conf_conv_w[i], _row(conf_conv_b[i]),
            _row(conf_ln_g[i]), _row(conf_ln_b[i]), w_conf_out[i].astype(BF16), sc_conv_w[i],
            w_sc_out[i].astype(BF16), w_o[i].astype(BF16), _row(ln1_g[i]), _row(ln1_b[i]),
            w_router[i // 2] if moe else None, b_router[i // 2] if moe else None,
            layer=i, is_mixer=True, alpha=alpha, ts=ts)
        xt = x.reshape(bsz * seq, d)
        ple_w = (w_ple_gate[i].astype(BF16), _row(b_ple_gate[i]), w_ple_proj[i].astype(BF16),
                 _row(ln2_g[i]), _row(ln2_b[i]))
        j = i // 2
        if i % 2 == 0:
            xt = run(_dense_ffn_call,
                     xt, pt, i, w_ff_gate[j].astype(BF16), w_ff_up[j].astype(BF16),
                     w_ff_down[j].astype(BF16), *ple_w, layer=i, is_mixer=False, alpha=alpha, tm=tm)
        else:
            queue[:] = [q for q in queue if q[2] != i]
            ew = [cast.pop((j, name)) if (j, name) in cast else w[j].astype(BF16)
                  for name, w in expert_w]
            xt = _moe_layer(xt, pt, i, info, cnt, w_router.shape[2], *ew,
                            *ple_w, alpha=alpha, tm=tm, tme=tme, fc=fc_moe)
        x = xt.reshape(bsz, seq, d)
    return x
```

```python
import functools
from typing import NamedTuple

import jax
import jax.numpy as jnp
from jax import lax
from jax.experimental import pallas as pl
from jax.experimental.pallas import tpu as pltpu

LN_EPS = 1e-5
TOP_K = 2
LANES = 128
SUBLANES = 8
CONV_HALO = 32
SC_HALO = 8
MXU_COLS = 256
CONV_ROWS = 64
NEG_BIG = -1e30
VMEM_LIMIT = 56 * 1024 * 1024

F32 = jnp.float32
BF16 = jnp.bfloat16


class Tiles(NamedTuple):
    seq_rows: int = 512
    token_rows: int = 512
    expert_rows: int = 512
    expert_cols: int = 512


TILES = Tiles()


def _dot(a, b):
    return jnp.dot(a, b, preferred_element_type=F32)


def _sigmoid(x):
    return 1.0 / (1.0 + jnp.exp(-x))


def _layer_norm(x, g, b):
    mu = jnp.mean(x, axis=-1, keepdims=True)
    xc = x - mu
    var = jnp.mean(xc * xc, axis=-1, keepdims=True)
    return xc * lax.rsqrt(var + LN_EPS) * g + b


def _const_spec(shape):
    nd = len(shape)
    return pl.BlockSpec(shape, lambda *_: (0,) * nd, pipeline_mode=pl.Buffered(1))


def _route(x_tile, wr_ref, br_ref, upper_ref, n_exp):
    logits = _dot(x_tile.astype(BF16), wr_ref[...]) + br_ref[...]
    lt = logits.T
    tm = lt.shape[1]
    row = [lt[e:e + 1, :] for e in range(n_exp)]
    zero = jnp.zeros((1, tm), F32)
    m1, i1, m2, i2 = row[0], zero, jnp.full((1, tm), 2.0 * NEG_BIG, F32), zero
    for e in range(1, n_exp):
        first = row[e] > m1
        second = row[e] > m2
        m2 = jnp.where(first, m1, jnp.where(second, row[e], m2))
        i2 = jnp.where(first, i1, jnp.where(second, float(e), i2))
        m1 = jnp.where(first, row[e], m1)
        i1 = jnp.where(first, float(e), i1)
    e21 = jnp.exp(m2 - m1)
    w1 = 1.0 / (1.0 + e21)
    w2 = e21 / (1.0 + e21)

    expert = lax.broadcasted_iota(jnp.int32, (SUBLANES, tm), 0).astype(F32)
    hit1 = expert == i1
    hit2 = expert == i2
    onehot = jnp.where(hit1 | hit2, 1.0, 0.0)
    ranks = _dot(onehot.astype(BF16), upper_ref[...])
    r1 = jnp.sum(jnp.where(hit1, ranks, 0.0), axis=0, keepdims=True)
    r2 = jnp.sum(jnp.where(hit2, ranks, 0.0), axis=0, keepdims=True)
    info = jnp.concatenate([i1, i2, r1, r2, w1, w2, zero, zero], axis=0)
    return info, jnp.sum(onehot, axis=1, keepdims=True)


def _side_cast_specs(side, n_steps, step_index):
    rows, cols = side.shape
    assert rows % (n_steps * 2 * SUBLANES) == 0
    block = (rows // n_steps, cols)
    spec = pl.BlockSpec(block, lambda *idx: (step_index(*idx), 0))
    return spec, spec, jax.ShapeDtypeStruct(side.shape, BF16)


def _causal_dwconv_chunk(src_ref, row0, halo, w_ref, taps, lanes, init):
    acc = init
    first = halo - (taps - 1)
    for r in range(SUBLANES):
        offs = [o for o in range(first, halo + 1) if o % SUBLANES == r]
        if not offs:
            continue
        base = offs[0]
        span = offs[-1] - base + CONV_ROWS
        blk = src_ref[row0 + base:row0 + base + span, lanes]
        part = None
        for o in offs:
            j = o - first
            term = w_ref[j:j + 1, lanes] * blk[o - base:o - base + CONV_ROWS, :]
            part = term if part is None else part + term
        acc = acc + part
    return acc


def _mixer_kernel(x_ref, w_in_ref, b_in_ref, ccw_ref, ccb_ref, clg_ref, clb_ref, wa_ref,
                  scw_ref, wb_ref, wo_ref, g1_ref, b1_ref, wr_ref, br_ref, upper_ref, side_ref,
                  o_ref, side_out_ref, info_ref, cnt_ref,
                  a_ext, u_ext, scb_buf, act_a, act_b, zg_buf, *, alpha, ts, dc, ds, kc, ks, n_exp):
    s = pl.program_id(1)
    side_out_ref[...] = side_ref[...].astype(BF16)

    @pl.when(s == 0)
    def _():
        a_ext[0:CONV_HALO, :] = jnp.zeros((CONV_HALO, dc), F32)
        u_ext[0:SC_HALO, :] = jnp.zeros((SC_HALO, ds), F32)

    @pl.when(s > 0)
    def _():
        a_ext[0:CONV_HALO, :] = a_ext[ts:ts + CONV_HALO, :]
        u_ext[0:SC_HALO, :] = u_ext[ts:ts + SC_HALO, :]

    x = x_ref[0]
    xb = x.astype(BF16)

    c0 = 2 * dc
    c1 = c0 + 3 * ds
    zc = _dot(xb, w_in_ref[:, 0:c0]) + b_in_ref[:, 0:c0]
    a_ext[CONV_HALO:CONV_HALO + ts, :] = zc[:, :dc] * _sigmoid(zc[:, dc:])
    zs = _dot(xb, w_in_ref[:, c0:c1]) + b_in_ref[:, c0:c1]
    scb_buf[...] = zs[:, :ds]
    u_ext[SC_HALO:SC_HALO + ts, :] = zs[:, ds:2 * ds] * zs[:, 2 * ds:]

    d = x.shape[-1]
    n_piece = 2 * d // MXU_COLS
    units = (ts // CONV_ROWS) * (dc // LANES)
    zg_pieces = []

    def gate_piece(j):
        cols = slice(c1 + j * MXU_COLS, c1 + (j + 1) * MXU_COLS)
        z = _dot(xb, w_in_ref[:, cols]) + b_in_ref[:, cols]
        zg_buf[:, j * MXU_COLS:(j + 1) * MXU_COLS] = z
        zg_pieces.append(z[0:SUBLANES, 0:LANES])

    def after(piece):
        bits = pltpu.bitcast(piece, jnp.uint32)
        return pltpu.bitcast((bits >> 16) >> 16, F32)

    u = 0
    for ci in range(ts // CONV_ROWS):
        row0 = ci * CONV_ROWS
        conv = []
        for cg in range(dc // LANES):
            lanes = slice(cg * LANES, (cg + 1) * LANES)
            want = min(n_piece, -(-(u + 1) * n_piece // units))
            while len(zg_pieces) < want:
                gate_piece(len(zg_pieces))
            u += 1
            init = jnp.broadcast_to(ccb_ref[:, lanes], (CONV_ROWS, LANES))
            init = init + jnp.tile(after(zg_pieces[-1]), (CONV_ROWS // SUBLANES, 1))
            conv.append(_causal_dwconv_chunk(a_ext, row0, CONV_HALO, ccw_ref, kc, lanes, init))
        conv = jnp.concatenate(conv, axis=-1)
        ln = _layer_norm(conv, clg_ref[...], clb_ref[...])
        act_a[row0:row0 + CONV_ROWS, :] = (ln * _sigmoid(ln)).astype(BF16)
        sconv = []
        for cg in range(ds // LANES):
            lanes = slice(cg * LANES, (cg + 1) * LANES)
            init = jnp.zeros((CONV_ROWS, LANES), F32)
            sconv.append(_causal_dwconv_chunk(u_ext, row0, SC_HALO, scw_ref, ks, lanes, init))
        sconv = jnp.concatenate(sconv, axis=-1)
        act_b[row0:row0 + CONV_ROWS, :] = (scb_buf[row0:row0 + CONV_ROWS, :] * sconv).astype(BF16)
    zg = zg_buf[...]

    y_a = _dot(act_a[...], wa_ref[...])
    y_b = _dot(act_b[...], wb_ref[...])
    m = _sigmoid(zg[:, :d]) * y_a + _sigmoid(zg[:, d:]) * y_b
    mix = _dot(m.astype(BF16), wo_ref[...])
    out = _layer_norm(alpha * x + mix, g1_ref[...], b1_ref[...])
    o_ref[0] = out
    if n_exp:
        info, counts = _route(out, wr_ref, br_ref, upper_ref, n_exp)
        info_ref[...] = info
        cnt_ref[...] = jnp.broadcast_to(counts, cnt_ref.shape)
    else:
        info_ref[...] = jnp.zeros(info_ref.shape, F32)
        cnt_ref[...] = jnp.zeros(cnt_ref.shape, F32)


def _mixer_call(x, w_in, b_in, ccw, ccb, clg, clb, wa, scw, wb, wo, g1, b1, w_router, b_router,
                side, *, alpha, ts):
    bsz, seq, d = x.shape
    kc, dc = ccw.shape
    ks, ds = scw.shape
    assert seq % ts == 0 and ts % CONV_ROWS == 0 and ts >= CONV_HALO
    assert kc - 1 <= CONV_HALO and ks - 1 <= SC_HALO
    assert dc % LANES == 0 and ds % LANES == 0 and (2 * d) % MXU_COLS == 0
    n_exp = 0 if w_router is None else w_router.shape[1]
    assert n_exp <= SUBLANES and ts % LANES == 0 and ts < 2 ** 24
    wr = jnp.zeros((d, LANES), BF16)
    br = jnp.full((1, LANES), NEG_BIG, F32)
    if n_exp:
        wr = wr.at[:, :n_exp].set(w_router.astype(BF16))
        br = br.at[0, :n_exp].set(b_router)
    upper = jnp.triu(jnp.ones((ts, ts), BF16), 1)
    kern = functools.partial(_mixer_kernel, alpha=alpha, ts=ts, dc=dc, ds=ds, kc=kc, ks=ks, n_exp=n_exp)
    consts = (w_in, b_in, ccw, ccb, clg, clb, wa, scw, wb, wo, g1, b1, wr, br, upper)
    tiles = seq // ts
    side_in, side_out, side_shape = _side_cast_specs(side, bsz * tiles, lambda b, s: b * tiles + s)
    return pl.pallas_call(
        kern,
        grid=(bsz, tiles),
        in_specs=[pl.BlockSpec((1, ts, d), lambda b, s: (b, s, 0))]
        + [_const_spec(c.shape) for c in consts] + [side_in],
        out_specs=[pl.BlockSpec((1, ts, d), lambda b, s: (b, s, 0)), side_out,
                   pl.BlockSpec((SUBLANES, ts), lambda b, s: (0, b * tiles + s)),
                   pl.BlockSpec((SUBLANES, LANES), lambda b, s: (b * tiles + s, 0))],
        out_shape=[jax.ShapeDtypeStruct((bsz, seq, d), F32), side_shape,
                   jax.ShapeDtypeStruct((SUBLANES, bsz * seq), F32),
                   jax.ShapeDtypeStruct((bsz * tiles * SUBLANES, LANES), F32)],
        scratch_shapes=[
            pltpu.VMEM((ts + CONV_HALO, dc), F32),
            pltpu.VMEM((ts + SC_HALO, ds), F32),
            pltpu.VMEM((ts, ds), F32),
            pltpu.VMEM((ts, dc), BF16),
            pltpu.VMEM((ts, ds), BF16),
            pltpu.VMEM((ts, 2 * d), F32),
        ],
        compiler_params=pltpu.CompilerParams(
            dimension_semantics=("arbitrary", "arbitrary"), vmem_limit_bytes=VMEM_LIMIT),
        name="mixer",
    )(x, *consts, side)


def _ple(xb, p, wpg_ref, bpg_ref, wpp_ref):
    gate = _sigmoid(_dot(xb, wpg_ref[...]) + bpg_ref[...])
    return gate * _dot(p.astype(BF16), wpp_ref[...])


def _dense_ffn_kernel(x_ref, p_ref, wg_ref, wu_ref, wd_ref, wpg_ref, bpg_ref, wpp_ref,
                      g2_ref, b2_ref, side_ref, o_ref, side_out_ref, *, alpha):
    side_out_ref[...] = side_ref[...].astype(BF16)
    x = x_ref[...]
    xb = x.astype(BF16)
    acc = alpha * x + _ple(xb, p_ref[...], wpg_ref, bpg_ref, wpp_ref)
    g = _dot(xb, wg_ref[...])
    u = _dot(xb, wu_ref[...])
    h = (g * _sigmoid(g) * u).astype(BF16)
    acc = acc + _dot(h, wd_ref[...])
    o_ref[...] = _layer_norm(acc, g2_ref[...], b2_ref[...])


def _dense_ffn_call(x, p, layer, wg, wu, wd, wpg, bpg, wpp, g2, b2, side, *, alpha, tm):
    t, d = x.shape
    assert t % tm == 0
    consts = (wg, wu, wd, wpg, bpg, wpp, g2, b2)
    side_in, side_out, side_shape = _side_cast_specs(side, t // tm, lambda i: i)
    return pl.pallas_call(
        functools.partial(_dense_ffn_kernel, alpha=alpha),
        grid=(t // tm,),
        in_specs=[pl.BlockSpec((tm, d), lambda i: (i, 0)),
                  pl.BlockSpec((None, tm, p.shape[2]), lambda i: (layer, i, 0))]
        + [_const_spec(c.shape) for c in consts] + [side_in],
        out_specs=[pl.BlockSpec((tm, d), lambda i: (i, 0)), side_out],
        out_shape=[jax.ShapeDtypeStruct((t, d), F32), side_shape],
        compiler_params=pltpu.CompilerParams(
            dimension_semantics=("arbitrary",), vmem_limit_bytes=VMEM_LIMIT),
        name="dense_ffn",
    )(x, p, *consts, side)


ROW_CHUNKS = (512, 256, 128, 64, 32, 16, 8)


def _for_row_chunks(n, fn):
    off = 0
    for rows in ROW_CHUNKS:
        take = (n & rows) != 0
        pl.when(take)(functools.partial(fn, off, rows))
        off = off + jnp.where(take, rows, 0)


TOTAL_CHUNKS = (1024,) + ROW_CHUNKS


def _wait_rows(n, wait_fn):
    for rows in TOTAL_CHUNKS:
        pl.when((n & rows) != 0)(functools.partial(wait_fn, rows))


def _local_slot(expert, rank, loff_ref, base, n_exp):
    off = jnp.zeros(expert.shape, F32)
    for e in range(n_exp):
        off = jnp.where(expert == float(e), loff_ref[base + e].astype(F32), off)
    return (off + rank).astype(jnp.int32)


def _dispatch_kernel(n8_ref, loff_ref, gdst_ref, gap_ref, x_ref, info_ref, xs_ref,
                     comp, zbuf, sems, *, tm, n_exp):
    i = pl.program_id(0)
    n_steps = pl.num_programs(0)
    slot = i % 2
    nc = comp.shape[1]

    def group_copy(step, e, s, off, rows):
        src = comp.at[s, pl.ds(pl.multiple_of(loff_ref[step * n_exp + e] + off, SUBLANES), rows)]
        dst = xs_ref.at[pl.ds(pl.multiple_of(gdst_ref[step * n_exp + e] + off, SUBLANES), rows)]
        return pltpu.make_async_copy(src, dst, sems.at[s])

    def start_groups(step, s):
        for e in range(n_exp):
            _for_row_chunks(n8_ref[step * n_exp + e],
                            lambda off, rows, e=e: group_copy(step, e, s, off, rows).start())

    def wait_groups(step, s):
        total = n8_ref[step * n_exp]
        for e in range(1, n_exp):
            total = total + n8_ref[step * n_exp + e]
        _wait_rows(total, lambda rows: pltpu.make_async_copy(
            comp.at[s, pl.ds(0, rows)], xs_ref.at[pl.ds(0, rows)], sems.at[s]).wait())

    info_t = info_ref[...]
    slot_row = lax.broadcasted_iota(jnp.int32, (nc, tm), 0)
    hit = None
    for k in range(TOP_K):
        s_k = _local_slot(info_t[k:k + 1, :], info_t[TOP_K + k:TOP_K + k + 1, :],
                          loff_ref, i * n_exp, n_exp)
        hit = (slot_row == s_k) if hit is None else hit | (slot_row == s_k)
    onehot = jnp.where(hit, 1.0, 0.0).astype(BF16)
    comp[slot] = _dot(onehot, x_ref[...].astype(BF16))

    @pl.when(i > 0)
    def _():
        wait_groups(i - 1, 1 - slot)

    start_groups(i, slot)

    @pl.when(i == n_steps - 1)
    def _():
        wait_groups(i, slot)
        zbuf[...] = jnp.zeros(zbuf.shape, F32)

        def gap_copy(e, off, rows):
            dst = xs_ref.at[pl.ds(pl.multiple_of(gap_ref[2 * e] + off, SUBLANES), rows)]
            return pltpu.make_async_copy(zbuf.at[pl.ds(0, rows)], dst, sems.at[0])

        for e in range(n_exp):
            _for_row_chunks(gap_ref[2 * e + 1],
                            lambda off, rows, e=e: gap_copy(e, off, rows).start())
        for e in range(n_exp):
            _for_row_chunks(gap_ref[2 * e + 1],
                            lambda off, rows, e=e: gap_copy(e, off, rows).wait())

        tail_rows = zbuf.shape[0]

        def tail_copy(j):
            start = pl.multiple_of(gap_ref[2 * n_exp] + j * tail_rows, SUBLANES)
            return pltpu.make_async_copy(zbuf, xs_ref.at[pl.ds(start, tail_rows)], sems.at[0])

        def tail(j, c):
            tail_copy(j).start()
            tail_copy(j).wait()
            return c

        lax.fori_loop(0, gap_ref[2 * n_exp + 1], tail, 0)


def _dispatch_call(n8, loff, gdst, gap, x, info, *, tm, tme, n_exp, n_rows):
    t, d = x.shape
    nc = TOP_K * tm + SUBLANES * n_exp
    return pl.pallas_call(
        functools.partial(_dispatch_kernel, tm=tm, n_exp=n_exp),
        grid_spec=pltpu.PrefetchScalarGridSpec(
            num_scalar_prefetch=4,
            grid=(t // tm,),
            in_specs=[pl.BlockSpec((tm, d), lambda i, *_: (i, 0)),
                      pl.BlockSpec((SUBLANES, tm), lambda i, *_: (0, i))],
            out_specs=pl.BlockSpec(memory_space=pl.ANY),
            scratch_shapes=[pltpu.VMEM((2, nc, d), F32),
                            pltpu.VMEM((tme, d), F32),
                            pltpu.SemaphoreType.DMA((2,))],
        ),
        out_shape=jax.ShapeDtypeStruct((n_rows, d), F32),
        compiler_params=pltpu.CompilerParams(
            dimension_semantics=("arbitrary",), vmem_limit_bytes=VMEM_LIMIT),
        name="dispatch",
    )(n8, loff, gdst, gap, x, info)


def _expert_kernel(tile_e_ref, tile_n_ref, xs_ref, wg_ref, wu_ref, wd_ref, ys_ref, *, fc):
    del tile_e_ref
    i = pl.program_id(0)
    n_valid = tile_n_ref[i]
    tme = ys_ref.shape[0]
    half = tme // 2

    def swiglu(rows):
        xb = xs_ref[0:rows, :].astype(BF16)
        f = wg_ref.shape[2]
        acc = jnp.zeros((rows, ys_ref.shape[1]), F32)
        for c in range(f // fc):
            cols = slice(c * fc, (c + 1) * fc)
            g = _dot(xb, wg_ref[0, :, cols])
            u = _dot(xb, wu_ref[0, :, cols])
            h = (g * _sigmoid(g) * u).astype(BF16)
            acc = acc + _dot(h, wd_ref[0, cols, :])
        ys_ref[0:rows, :] = acc

    @pl.when(n_valid > half)
    def _():
        swiglu(tme)

    @pl.when((n_valid > 0) & (n_valid <= half))
    def _():
        swiglu(half)
        ys_ref[half:tme, :] = jnp.zeros((tme - half, ys_ref.shape[1]), F32)

    @pl.when(n_valid == 0)
    def _():
        ys_ref[...] = jnp.zeros(ys_ref.shape, F32)


def _expert_call(tile_e, tile_n, xs, wg, wu, wd, *, tme, fc):
    n_rows, d = xs.shape
    f = wg.shape[2]
    assert n_rows % tme == 0 and f % fc == 0 and fc % LANES == 0

    def w_spec(shape):
        return pl.BlockSpec((1,) + shape[1:], lambda i, te, tn: (te[i], 0, 0))

    return pl.pallas_call(
        functools.partial(_expert_kernel, fc=fc),
        grid_spec=pltpu.PrefetchScalarGridSpec(
            num_scalar_prefetch=2,
            grid=(n_rows // tme,),
            in_specs=[pl.BlockSpec((tme, d), lambda i, te, tn: (i, 0)),
                      w_spec(wg.shape), w_spec(wu.shape), w_spec(wd.shape)],
            out_specs=pl.BlockSpec((tme, d), lambda i, te, tn: (i, 0)),
        ),
        out_shape=jax.ShapeDtypeStruct((n_rows, d), F32),
        compiler_params=pltpu.CompilerParams(
            dimension_semantics=("arbitrary",), vmem_limit_bytes=VMEM_LIMIT),
        name="experts",
    )(tile_e, tile_n, xs, wg, wu, wd)


def _combine_kernel(n8_ref, loff_ref, gdst_ref, x_ref, p_ref, info_ref, ys_ref, wpg_ref, bpg_ref,
                    wpp_ref, g2_ref, b2_ref, o_ref, ycomp, sems, *, alpha, tm, n_exp):
    i = pl.program_id(0)
    n_steps = pl.num_programs(0)
    slot = i % 2
    nc = ycomp.shape[1]

    def group_copy(step, e, s, off, rows):
        src = ys_ref.at[pl.ds(pl.multiple_of(gdst_ref[step * n_exp + e] + off, SUBLANES), rows)]
        dst = ycomp.at[s, pl.ds(pl.multiple_of(loff_ref[step * n_exp + e] + off, SUBLANES), rows)]
        return pltpu.make_async_copy(src, dst, sems.at[s])

    def start_groups(step, s):
        for e in range(n_exp):
            _for_row_chunks(n8_ref[step * n_exp + e],
                            lambda off, rows, e=e: group_copy(step, e, s, off, rows).start())

    def wait_groups(step, s):
        total = n8_ref[step * n_exp]
        for e in range(1, n_exp):
            total = total + n8_ref[step * n_exp + e]
        _wait_rows(total, lambda rows: pltpu.make_async_copy(
            ys_ref.at[pl.ds(0, rows)], ycomp.at[s, pl.ds(0, rows)], sems.at[s]).wait())

    @pl.when(i == 0)
    def _():
        ycomp[...] = jnp.zeros(ycomp.shape, F32)
        start_groups(0, 0)

    wait_groups(i, slot)

    @pl.when(i + 1 < n_steps)
    def _():
        start_groups(i + 1, 1 - slot)

    x = x_ref[...]
    acc = alpha * x + _ple(x.astype(BF16), p_ref[...], wpg_ref, bpg_ref, wpp_ref)

    info_t = info_ref[...]
    info = jnp.concatenate([info_t, jnp.zeros((LANES - SUBLANES, tm), F32)], axis=0).T
    slot_col = lax.broadcasted_iota(jnp.int32, (tm, nc), 1)
    pick = jnp.zeros((tm, nc), F32)
    for k in range(TOP_K):
        s_k = _local_slot(info[:, k:k + 1], info[:, TOP_K + k:TOP_K + k + 1],
                          loff_ref, i * n_exp, n_exp)
        pick = jnp.where(slot_col == s_k, info[:, 4 + k:5 + k], pick)
    acc = acc + _dot(pick.astype(BF16), ycomp[slot].astype(BF16))
    o_ref[...] = _layer_norm(acc, g2_ref[...], b2_ref[...])


def _combine_call(n8, loff, gdst, x, p, layer, info, ys, wpg, bpg, wpp, g2, b2, *, alpha, tm, n_exp):
    t, d = x.shape
    nc = TOP_K * tm + SUBLANES * n_exp
    consts = (wpg, bpg, wpp, g2, b2)

    def const_spec(shape):
        nd = len(shape)
        return pl.BlockSpec(shape, lambda i, *_: (0,) * nd, pipeline_mode=pl.Buffered(1))

    return pl.pallas_call(
        functools.partial(_combine_kernel, alpha=alpha, tm=tm, n_exp=n_exp),
        grid_spec=pltpu.PrefetchScalarGridSpec(
            num_scalar_prefetch=3,
            grid=(t // tm,),
            in_specs=[pl.BlockSpec((tm, d), lambda i, *_: (i, 0)),
                      pl.BlockSpec((None, tm, p.shape[2]), lambda i, *_: (layer, i, 0)),
                      pl.BlockSpec((SUBLANES, tm), lambda i, *_: (0, i)),
                      pl.BlockSpec(memory_space=pl.ANY)]
            + [const_spec(c.shape) for c in consts],
            out_specs=pl.BlockSpec((tm, d), lambda i, *_: (i, 0)),
            scratch_shapes=[pltpu.VMEM((2, nc, d), F32), pltpu.SemaphoreType.DMA((2,))],
        ),
        out_shape=jax.ShapeDtypeStruct((t, d), F32),
        compiler_params=pltpu.CompilerParams(
            dimension_semantics=("arbitrary",), vmem_limit_bytes=VMEM_LIMIT),
        name="combine",
    )(n8, loff, gdst, x, p, info, ys, *consts)


def _moe_layer(x, p, layer, info, cnt, n_exp, we_gate, we_up, we_down, wpg, bpg, wpp, g2, b2,
               *, alpha, tm, tme, fc):
    t, d = x.shape
    n_tok_tiles = t // tm
    assert tm <= ROW_CHUNKS[0] and tme <= ROW_CHUNKS[0]

    i32 = jnp.int32
    n = cnt[:, 0].reshape(n_tok_tiles, SUBLANES)[:, :n_exp].astype(i32)
    n8 = (n + SUBLANES - 1) // SUBLANES * SUBLANES
    loff = jnp.cumsum(n8, axis=1) - n8
    tot = jnp.sum(n8, axis=0)
    reg = (tot + tme - 1) // tme * tme
    reg_end = jnp.cumsum(reg)
    base = reg_end - reg
    gdst = base[None, :] + jnp.cumsum(n8, axis=0) - n8
    n_tiles = (t * TOP_K + n_tok_tiles * n_exp * (SUBLANES - 1)) // tme + n_exp + 1
    n_rows = n_tiles * tme
    tile_start = jnp.arange(n_tiles, dtype=i32) * tme
    tile_e = jnp.minimum(jnp.sum(tile_start[:, None] >= reg_end[None, :], axis=1), n_exp - 1).astype(i32)
    tile_n = jnp.clip(tot[tile_e] - (tile_start - base[tile_e]), 0, tme).astype(i32)
    gap = jnp.stack([base + tot, reg - tot], axis=1).reshape(-1)
    gap = jnp.concatenate([gap, jnp.stack([reg_end[-1], (n_rows - reg_end[-1]) // tme])]).astype(i32)
    n8f, lofff, gdstf = (a.reshape(-1).astype(i32) for a in (n8, loff, gdst))

    xs = _dispatch_call(n8f, lofff, gdstf, gap, x, info, tm=tm, tme=tme, n_exp=n_exp, n_rows=n_rows)
    ys = _expert_call(tile_e, tile_n, xs, we_gate, we_up, we_down, tme=tme, fc=fc)
    return _combine_call(n8f, lofff, gdstf, x, p, layer, info, ys, wpg, bpg, wpp, g2, b2,
                         alpha=alpha, tm=tm, n_exp=n_exp)


def _row(v):
    return v.reshape(1, -1)


def kernel(x, p, w_in, b_in, conf_conv_w, conf_conv_b, conf_ln_g, conf_ln_b, w_conf_out,
           sc_conv_w, w_sc_out, w_o, ln1_g, ln1_b, w_ff_gate, w_ff_up, w_ff_down, w_router,
           b_router, we_gate, we_up, we_down, w_ple_gate, b_ple_gate, w_ple_proj, ln2_g, ln2_b,
           ):
    ts, tm, tme, fc_moe = TILES
    depth = w_in.shape[0]
    alpha = (2 * depth) ** 0.25
    bsz, seq, d = x.shape
    pt = p.reshape(depth, bsz * seq, -1)

    expert_w = (("gate", we_gate), ("up", we_up), ("down", we_down))
    queue = [((i // 2, name), w[i // 2], i) for i in range(depth) if i % 2 == 1 for name, w in expert_w]
    cast = {}
    idle = jnp.zeros((bsz * seq // min(ts, tm) * 2 * SUBLANES, LANES), F32)

    def side_job(layer, is_mixer):
        for k, (key, w, needed_in) in enumerate(queue):
            if layer < needed_in or (layer == needed_in and is_mixer):
                del queue[k]
                return key, w
        return None, idle

    def run(call, *args, layer, is_mixer, **kw):
        key, w = side_job(layer, is_mixer)
        out, w_bf16, *rest = call(*args, w.reshape(-1, w.shape[-1]), **kw)
        if key is not None:
            cast[key] = w_bf16.reshape(w.shape)
        return (out, *rest) if rest else out

    assert ts == tm
    for i in range(depth):
        moe = i % 2 == 1
        x, info, cnt = run(
            _mixer_call,
            x, w_in[i].astype(BF16), _row(b_in[i]), conf_conv_w[i], _row(conf_conv_b[i]),
            _row(conf_ln_g[i]), _row(conf_ln_b[i]), w_conf_out[i].astype(BF16), sc_conv_w[i],
            w_sc_out[i].astype(BF16), w_o[i].astype(BF16), _row(ln1_g[i]), _row(ln1_b[i]),
            w_router[i // 2] if moe else None, b_router[i // 2] if moe else None,
            layer=i, is_mixer=True, alpha=alpha, ts=ts)
        xt = x.reshape(bsz * seq, d)
        ple_w = (w_ple_gate[i].astype(BF16), _row(b_ple_gate[i]), w_ple_proj[i].astype(BF16),
                 _row(ln2_g[i]), _row(ln2_b[i]))
        j = i // 2
        if i % 2 == 0:
            xt = run(_dense_ffn_call,
                     xt, pt, i, w_ff_gate[j].astype(BF16), w_ff_up[j].astype(BF16),
                     w_ff_down[j].astype(BF16), *ple_w, layer=i, is_mixer=False, alpha=alpha, tm=tm)
        else:
            queue[:] = [q for q in queue if q[2] != i]
            ew = [cast.pop((j, name)) if (j, name) in cast else w[j].astype(BF16)
                  for name, w in expert_w]
            xt = _moe_layer(xt, pt, i, info, cnt, w_router.shape[2], *ew,
                            *ple_w, alpha=alpha, tm=tm, tme=tme, fc=fc_moe)
        x = xt.reshape(bsz, seq, d)
    return x
```

```python
import functools
from typing import NamedTuple

import jax
import jax.numpy as jnp
import numpy as np
from jax import lax
from jax.experimental import pallas as pl
from jax.experimental.pallas import tpu as pltpu

LN_EPS = 1e-5
TOP_K = 2
LANES = 128
SUBLANES = 8
CONV_HALO = 32
SC_HALO = 8
MXU_COLS = 256
CONV_ROWS = 64
NEG_BIG = -1e30
VMEM_LIMIT = 56 * 1024 * 1024

F32 = jnp.float32
BF16 = jnp.bfloat16


class Tiles(NamedTuple):
    seq_rows: int = 512
    token_rows: int = 512
    expert_rows: int = 512
    expert_cols: int = 512


TILES = Tiles()


def _dot(a, b):
    return jnp.dot(a, b, preferred_element_type=F32)


def _sigmoid(x):
    return 1.0 / (1.0 + jnp.exp(-x))


def _layer_norm(x, g, b):
    mu = jnp.mean(x, axis=-1, keepdims=True)
    xc = x - mu
    var = jnp.mean(xc * xc, axis=-1, keepdims=True)
    return xc * lax.rsqrt(var + LN_EPS) * g + b


def _const_spec(shape):
    nd = len(shape)
    return pl.BlockSpec(shape, lambda *_: (0,) * nd, pipeline_mode=pl.Buffered(1))


def _layer_spec(stacked, index):
    nd = stacked.ndim
    return pl.BlockSpec((None,) + stacked.shape[1:], lambda *_: (index,) + (0,) * (nd - 1),
                        pipeline_mode=pl.Buffered(1))


def _route(x_tile, wr_ref, br_ref, upper_ref, n_exp):
    logits = _dot(x_tile.astype(BF16), wr_ref[...]) + br_ref[...]
    lt = logits.T
    tm = lt.shape[1]
    row = [lt[e:e + 1, :] for e in range(n_exp)]
    zero = jnp.zeros((1, tm), F32)
    m1, i1, m2, i2 = row[0], zero, jnp.full((1, tm), 2.0 * NEG_BIG, F32), zero
    for e in range(1, n_exp):
        first = row[e] > m1
        second = row[e] > m2
        m2 = jnp.where(first, m1, jnp.where(second, row[e], m2))
        i2 = jnp.where(first, i1, jnp.where(second, float(e), i2))
        m1 = jnp.where(first, row[e], m1)
        i1 = jnp.where(first, float(e), i1)
    e21 = jnp.exp(m2 - m1)
    w1 = 1.0 / (1.0 + e21)
    w2 = e21 / (1.0 + e21)

    expert = lax.broadcasted_iota(jnp.int32, (SUBLANES, tm), 0).astype(F32)
    hit1 = expert == i1
    hit2 = expert == i2
    onehot = jnp.where(hit1 | hit2, 1.0, 0.0)
    ranks = _dot(onehot.astype(BF16), upper_ref[...])
    r1 = jnp.sum(jnp.where(hit1, ranks, 0.0), axis=0, keepdims=True)
    r2 = jnp.sum(jnp.where(hit2, ranks, 0.0), axis=0, keepdims=True)
    info = jnp.concatenate([i1, i2, r1, r2, w1, w2, zero, zero], axis=0)
    return info, jnp.sum(onehot, axis=1, keepdims=True)


def _side_cast_specs(side, n_steps, step_index):
    rows, cols = side.shape
    assert rows % (n_steps * 2 * SUBLANES) == 0
    block = (rows // n_steps, cols)
    spec = pl.BlockSpec(block, lambda *idx: (step_index(*idx), 0))
    return spec, spec, jax.ShapeDtypeStruct(side.shape, BF16)


def _causal_dwconv_chunk(src_ref, row0, halo, w_ref, taps, lanes, init):
    acc = init
    first = halo - (taps - 1)
    for r in range(SUBLANES):
        offs = [o for o in range(first, halo + 1) if o % SUBLANES == r]
        if not offs:
            continue
        base = offs[0]
        span = offs[-1] - base + CONV_ROWS
        blk = src_ref[row0 + base:row0 + base + span, lanes]
        part = None
        for o in offs:
            j = o - first
            term = w_ref[j:j + 1, lanes] * blk[o - base:o - base + CONV_ROWS, :]
            part = term if part is None else part + term
        acc = acc + part
    return acc


def _mixer_kernel(x_ref, w_in_ref, b_in_ref, ccw_ref, ccb_ref, clg_ref, clb_ref, wa_ref,
                  scw_ref, wb_ref, wo_ref, g1_ref, b1_ref, wr_ref, br_ref, upper_ref, side_ref,
                  o_ref, side_out_ref, info_ref, cnt_ref,
                  a_ext, u_ext, scb_buf, act_a, act_b, zg_buf, *, alpha, ts, dc, ds, kc, ks, n_exp):
    s = pl.program_id(1)
    side_out_ref[...] = side_ref[...].astype(BF16)

    @pl.when(s == 0)
    def _():
        a_ext[0:CONV_HALO, :] = jnp.zeros((CONV_HALO, dc), F32)
        u_ext[0:SC_HALO, :] = jnp.zeros((SC_HALO, ds), F32)

    @pl.when(s > 0)
    def _():
        a_ext[0:CONV_HALO, :] = a_ext[ts:ts + CONV_HALO, :]
        u_ext[0:SC_HALO, :] = u_ext[ts:ts + SC_HALO, :]

    x = x_ref[0]
    xb = x.astype(BF16)

    c0 = 2 * dc
    c1 = c0 + 3 * ds
    zc = _dot(xb, w_in_ref[:, 0:c0]) + b_in_ref[:, 0:c0]
    a_ext[CONV_HALO:CONV_HALO + ts, :] = zc[:, :dc] * _sigmoid(zc[:, dc:])
    zs = _dot(xb, w_in_ref[:, c0:c1]) + b_in_ref[:, c0:c1]
    scb_buf[...] = zs[:, :ds]
    u_ext[SC_HALO:SC_HALO + ts, :] = zs[:, ds:2 * ds] * zs[:, 2 * ds:]

    d = x.shape[-1]
    n_piece = 2 * d // MXU_COLS
    units = (ts // CONV_ROWS) * (dc // LANES)
    zg_pieces = []

    def gate_piece(j):
        cols = slice(c1 + j * MXU_COLS, c1 + (j + 1) * MXU_COLS)
        z = _dot(xb, w_in_ref[:, cols]) + b_in_ref[:, cols]
        zg_buf[:, j * MXU_COLS:(j + 1) * MXU_COLS] = z
        zg_pieces.append(z[0:SUBLANES, 0:LANES])

    def after(piece):
        bits = pltpu.bitcast(piece, jnp.uint32)
        return pltpu.bitcast((bits >> 16) >> 16, F32)

    u = 0
    for ci in range(ts // CONV_ROWS):
        row0 = ci * CONV_ROWS
        conv = []
        for cg in range(dc // LANES):
            lanes = slice(cg * LANES, (cg + 1) * LANES)
            want = min(n_piece, -(-(u + 1) * n_piece // units))
            while len(zg_pieces) < want:
                gate_piece(len(zg_pieces))
            u += 1
            init = jnp.broadcast_to(ccb_ref[:, lanes], (CONV_ROWS, LANES))
            init = init + jnp.tile(after(zg_pieces[-1]), (CONV_ROWS // SUBLANES, 1))
            conv.append(_causal_dwconv_chunk(a_ext, row0, CONV_HALO, ccw_ref, kc, lanes, init))
        conv = jnp.concatenate(conv, axis=-1)
        ln = _layer_norm(conv, clg_ref[...], clb_ref[...])
        act_a[row0:row0 + CONV_ROWS, :] = (ln * _sigmoid(ln)).astype(BF16)
        sconv = []
        for cg in range(ds // LANES):
            lanes = slice(cg * LANES, (cg + 1) * LANES)
            init = jnp.zeros((CONV_ROWS, LANES), F32)
            sconv.append(_causal_dwconv_chunk(u_ext, row0, SC_HALO, scw_ref, ks, lanes, init))
        sconv = jnp.concatenate(sconv, axis=-1)
        act_b[row0:row0 + CONV_ROWS, :] = (scb_buf[row0:row0 + CONV_ROWS, :] * sconv).astype(BF16)
    zg = zg_buf[...]

    y_a = _dot(act_a[...], wa_ref[...])
    y_b = _dot(act_b[...], wb_ref[...])
    m = _sigmoid(zg[:, :d]) * y_a + _sigmoid(zg[:, d:]) * y_b
    mix = _dot(m.astype(BF16), wo_ref[...])
    out = _layer_norm(alpha * x + mix, g1_ref[...], b1_ref[...])
    o_ref[0] = out
    if n_exp:
        info, counts = _route(out, wr_ref, br_ref, upper_ref, n_exp)
        info_ref[...] = info
        cnt_ref[...] = jnp.broadcast_to(counts, cnt_ref.shape)
    else:
        info_ref[...] = jnp.zeros(info_ref.shape, F32)
        cnt_ref[...] = jnp.zeros(cnt_ref.shape, F32)


def _mixer_call(x, layer, w_in, b_in, ccw, ccb, clg, clb, wa, scw, wb, wo, g1, b1, w_router, b_router,
                side, *, alpha, ts):
    bsz, seq, d = x.shape
    _, kc, dc = ccw.shape
    _, ks, ds = scw.shape
    assert seq % ts == 0 and ts % CONV_ROWS == 0 and ts >= CONV_HALO
    assert kc - 1 <= CONV_HALO and ks - 1 <= SC_HALO
    assert dc % LANES == 0 and ds % LANES == 0 and (2 * d) % MXU_COLS == 0
    n_exp = 0 if w_router is None else w_router.shape[1]
    assert n_exp <= SUBLANES and ts % LANES == 0 and ts < 2 ** 24
    wr = jnp.zeros((d, LANES), BF16)
    br = jnp.full((1, LANES), NEG_BIG, F32)
    if n_exp:
        wr = wr.at[:, :n_exp].set(w_router.astype(BF16))
        br = br.at[0, :n_exp].set(b_router)
    upper = jnp.asarray(np.triu(np.ones((ts, ts), np.float32), 1), dtype=BF16)
    kern = functools.partial(_mixer_kernel, alpha=alpha, ts=ts, dc=dc, ds=ds, kc=kc, ks=ks, n_exp=n_exp)
    stacks = (w_in, b_in, ccw, ccb, clg, clb, wa, scw, wb, wo, g1, b1)
    consts = (wr, br, upper)
    tiles = seq // ts
    side_in, side_out, side_shape = _side_cast_specs(side, bsz * tiles, lambda b, s: b * tiles + s)
    return pl.pallas_call(
        kern,
        grid=(bsz, tiles),
        in_specs=[pl.BlockSpec((1, ts, d), lambda b, s: (b, s, 0))]
        + [_layer_spec(c, layer) for c in stacks] + [_const_spec(c.shape) for c in consts] + [side_in],
        out_specs=[pl.BlockSpec((1, ts, d), lambda b, s: (b, s, 0)), side_out,
                   pl.BlockSpec((SUBLANES, ts), lambda b, s: (0, b * tiles + s)),
                   pl.BlockSpec((SUBLANES, LANES), lambda b, s: (b * tiles + s, 0))],
        out_shape=[jax.ShapeDtypeStruct((bsz, seq, d), F32), side_shape,
                   jax.ShapeDtypeStruct((SUBLANES, bsz * seq), F32),
                   jax.ShapeDtypeStruct((bsz * tiles * SUBLANES, LANES), F32)],
        scratch_shapes=[
            pltpu.VMEM((ts + CONV_HALO, dc), F32),
            pltpu.VMEM((ts + SC_HALO, ds), F32),
            pltpu.VMEM((ts, ds), F32),
            pltpu.VMEM((ts, dc), BF16),
            pltpu.VMEM((ts, ds), BF16),
            pltpu.VMEM((ts, 2 * d), F32),
        ],
        compiler_params=pltpu.CompilerParams(
            dimension_semantics=("arbitrary", "arbitrary"), vmem_limit_bytes=VMEM_LIMIT),
        name="mixer",
    )(x, *stacks, *consts, side)


def _ple(xb, p, wpg_ref, bpg_ref, wpp_ref):
    gate = _sigmoid(_dot(xb, wpg_ref[...]) + bpg_ref[...])
    return gate * _dot(p.astype(BF16), wpp_ref[...])


def _dense_ffn_kernel(x_ref, p_ref, wg_ref, wu_ref, wd_ref, wpg_ref, bpg_ref, wpp_ref,
                      g2_ref, b2_ref, side_ref, o_ref, side_out_ref, *, alpha):
    side_out_ref[...] = side_ref[...].astype(BF16)
    x = x_ref[...]
    xb = x.astype(BF16)
    acc = alpha * x + _ple(xb, p_ref[...], wpg_ref, bpg_ref, wpp_ref)
    g = _dot(xb, wg_ref[...])
    u = _dot(xb, wu_ref[...])
    h = (g * _sigmoid(g) * u).astype(BF16)
    acc = acc + _dot(h, wd_ref[...])
    o_ref[...] = _layer_norm(acc, g2_ref[...], b2_ref[...])


def _dense_ffn_call(x, p, layer, ffn_index, wg, wu, wd, wpg, bpg, wpp, g2, b2, side, *, alpha, tm):
    t, d = x.shape
    assert t % tm == 0
    ffn = (wg, wu, wd)
    per_layer = (wpg, bpg, wpp, g2, b2)
    side_in, side_out, side_shape = _side_cast_specs(side, t // tm, lambda i: i)
    return pl.pallas_call(
        functools.partial(_dense_ffn_kernel, alpha=alpha),
        grid=(t // tm,),
        in_specs=[pl.BlockSpec((tm, d), lambda i: (i, 0)),
                  pl.BlockSpec((None, tm, p.shape[2]), lambda i: (layer, i, 0))]
        + [_layer_spec(c, ffn_index) for c in ffn] + [_layer_spec(c, layer) for c in per_layer]
        + [side_in],
        out_specs=[pl.BlockSpec((tm, d), lambda i: (i, 0)), side_out],
        out_shape=[jax.ShapeDtypeStruct((t, d), F32), side_shape],
        compiler_params=pltpu.CompilerParams(
            dimension_semantics=("arbitrary",), vmem_limit_bytes=VMEM_LIMIT),
        name="dense_ffn",
    )(x, p, *ffn, *per_layer, side)


ROW_CHUNKS = (512, 256, 128, 64, 32, 16, 8)


def _for_row_chunks(n, fn):
    off = 0
    for rows in ROW_CHUNKS:
        take = (n & rows) != 0
        pl.when(take)(functools.partial(fn, off, rows))
        off = off + jnp.where(take, rows, 0)


TOTAL_CHUNKS = (1024,) + ROW_CHUNKS


def _wait_rows(n, wait_fn):
    for rows in TOTAL_CHUNKS:
        pl.when((n & rows) != 0)(functools.partial(wait_fn, rows))


def _local_slot(expert, rank, loff_ref, base, n_exp):
    off = jnp.zeros(expert.shape, F32)
    for e in range(n_exp):
        off = jnp.where(expert == float(e), loff_ref[base + e].astype(F32), off)
    return (off + rank).astype(jnp.int32)


def _dispatch_kernel(n8_ref, loff_ref, gdst_ref, gap_ref, x_ref, info_ref, xs_ref,
                     comp, zbuf, sems, *, tm, n_exp):
    i = pl.program_id(0)
    n_steps = pl.num_programs(0)
    slot = i % 2
    nc = comp.shape[1]

    def group_copy(step, e, s, off, rows):
        src = comp.at[s, pl.ds(pl.multiple_of(loff_ref[step * n_exp + e] + off, SUBLANES), rows)]
        dst = xs_ref.at[pl.ds(pl.multiple_of(gdst_ref[step * n_exp + e] + off, SUBLANES), rows)]
        return pltpu.make_async_copy(src, dst, sems.at[s])

    def start_groups(step, s):
        for e in range(n_exp):
            _for_row_chunks(n8_ref[step * n_exp + e],
                            lambda off, rows, e=e: group_copy(step, e, s, off, rows).start())

    def wait_groups(step, s):
        total = n8_ref[step * n_exp]
        for e in range(1, n_exp):
            total = total + n8_ref[step * n_exp + e]
        _wait_rows(total, lambda rows: pltpu.make_async_copy(
            comp.at[s, pl.ds(0, rows)], xs_ref.at[pl.ds(0, rows)], sems.at[s]).wait())

    info_t = info_ref[...]
    slot_row = lax.broadcasted_iota(jnp.int32, (nc, tm), 0)
    hit = None
    for k in range(TOP_K):
        s_k = _local_slot(info_t[k:k + 1, :], info_t[TOP_K + k:TOP_K + k + 1, :],
                          loff_ref, i * n_exp, n_exp)
        hit = (slot_row == s_k) if hit is None else hit | (slot_row == s_k)
    onehot = jnp.where(hit, 1.0, 0.0).astype(BF16)
    comp[slot] = _dot(onehot, x_ref[...].astype(BF16))

    @pl.when(i > 0)
    def _():
        wait_groups(i - 1, 1 - slot)

    start_groups(i, slot)

    @pl.when(i == n_steps - 1)
    def _():
        wait_groups(i, slot)
        zbuf[...] = jnp.zeros(zbuf.shape, F32)

        def gap_copy(e, off, rows):
            dst = xs_ref.at[pl.ds(pl.multiple_of(gap_ref[2 * e] + off, SUBLANES), rows)]
            return pltpu.make_async_copy(zbuf.at[pl.ds(0, rows)], dst, sems.at[0])

        for e in range(n_exp):
            _for_row_chunks(gap_ref[2 * e + 1],
                            lambda off, rows, e=e: gap_copy(e, off, rows).start())
        for e in range(n_exp):
            _for_row_chunks(gap_ref[2 * e + 1],
                            lambda off, rows, e=e: gap_copy(e, off, rows).wait())

        tail_rows = zbuf.shape[0]

        def tail_copy(j):
            start = pl.multiple_of(gap_ref[2 * n_exp] + j * tail_rows, SUBLANES)
            return pltpu.make_async_copy(zbuf, xs_ref.at[pl.ds(start, tail_rows)], sems.at[0])

        def tail(j, c):
            tail_copy(j).start()
            tail_copy(j).wait()
            return c

        lax.fori_loop(0, gap_ref[2 * n_exp + 1], tail, 0)


def _dispatch_call(n8, loff, gdst, gap, x, info, *, tm, tme, n_exp, n_rows):
    t, d = x.shape
    nc = TOP_K * tm + SUBLANES * n_exp
    return pl.pallas_call(
        functools.partial(_dispatch_kernel, tm=tm, n_exp=n_exp),
        grid_spec=pltpu.PrefetchScalarGridSpec(
            num_scalar_prefetch=4,
            grid=(t // tm,),
            in_specs=[pl.BlockSpec((tm, d), lambda i, *_: (i, 0)),
                      pl.BlockSpec((SUBLANES, tm), lambda i, *_: (0, i))],
            out_specs=pl.BlockSpec(memory_space=pl.ANY),
            scratch_shapes=[pltpu.VMEM((2, nc, d), F32),
                            pltpu.VMEM((tme, d), F32),
                            pltpu.SemaphoreType.DMA((2,))],
        ),
        out_shape=jax.ShapeDtypeStruct((n_rows, d), F32),
        compiler_params=pltpu.CompilerParams(
            dimension_semantics=("arbitrary",), vmem_limit_bytes=VMEM_LIMIT),
        name="dispatch",
    )(n8, loff, gdst, gap, x, info)


def _expert_kernel(tile_e_ref, tile_n_ref, xs_ref, wg_ref, wu_ref, wd_ref, ys_ref, *, fc):
    del tile_e_ref
    i = pl.program_id(0)
    n_valid = tile_n_ref[i]
    tme = ys_ref.shape[0]
    half = tme // 2

    def swiglu(rows):
        xb = xs_ref[0:rows, :].astype(BF16)
        f = wg_ref.shape[2]
        acc = jnp.zeros((rows, ys_ref.shape[1]), F32)
        for c in range(f // fc):
            cols = slice(c * fc, (c + 1) * fc)
            g = _dot(xb, wg_ref[0, :, cols])
            u = _dot(xb, wu_ref[0, :, cols])
            h = (g * _sigmoid(g) * u).astype(BF16)
            acc = acc + _dot(h, wd_ref[0, cols, :])
        ys_ref[0:rows, :] = acc

    @pl.when(n_valid > half)
    def _():
        swiglu(tme)

    @pl.when((n_valid > 0) & (n_valid <= half))
    def _():
        swiglu(half)
        ys_ref[half:tme, :] = jnp.zeros((tme - half, ys_ref.shape[1]), F32)

    @pl.when(n_valid == 0)
    def _():
        ys_ref[...] = jnp.zeros(ys_ref.shape, F32)


def _expert_call(tile_e, tile_n, xs, wg, wu, wd, *, tme, fc):
    n_rows, d = xs.shape
    f = wg.shape[2]
    assert n_rows % tme == 0 and f % fc == 0 and fc % LANES == 0

    def w_spec(shape):
        return pl.BlockSpec((1,) + shape[1:], lambda i, te, tn: (te[i], 0, 0))

    return pl.pallas_call(
        functools.partial(_expert_kernel, fc=fc),
        grid_spec=pltpu.PrefetchScalarGridSpec(
            num_scalar_prefetch=2,
            grid=(n_rows // tme,),
            in_specs=[pl.BlockSpec((tme, d), lambda i, te, tn: (i, 0)),
                      w_spec(wg.shape), w_spec(wu.shape), w_spec(wd.shape)],
            out_specs=pl.BlockSpec((tme, d), lambda i, te, tn: (i, 0)),
        ),
        out_shape=jax.ShapeDtypeStruct((n_rows, d), F32),
        compiler_params=pltpu.CompilerParams(
            dimension_semantics=("arbitrary",), vmem_limit_bytes=VMEM_LIMIT),
        name="experts",
    )(tile_e, tile_n, xs, wg, wu, wd)


def _combine_kernel(n8_ref, loff_ref, gdst_ref, x_ref, p_ref, info_ref, ys_ref, wpg_ref, bpg_ref,
                    wpp_ref, g2_ref, b2_ref, o_ref, ycomp, sems, *, alpha, tm, n_exp):
    i = pl.program_id(0)
    n_steps = pl.num_programs(0)
    slot = i % 2
    nc = ycomp.shape[1]

    def group_copy(step, e, s, off, rows):
        src = ys_ref.at[pl.ds(pl.multiple_of(gdst_ref[step * n_exp + e] + off, SUBLANES), rows)]
        dst = ycomp.at[s, pl.ds(pl.multiple_of(loff_ref[step * n_exp + e] + off, SUBLANES), rows)]
        return pltpu.make_async_copy(src, dst, sems.at[s])

    def start_groups(step, s):
        for e in range(n_exp):
            _for_row_chunks(n8_ref[step * n_exp + e],
                            lambda off, rows, e=e: group_copy(step, e, s, off, rows).start())

    def wait_groups(step, s):
        total = n8_ref[step * n_exp]
        for e in range(1, n_exp):
            total = total + n8_ref[step * n_exp + e]
        _wait_rows(total, lambda rows: pltpu.make_async_copy(
            ys_ref.at[pl.ds(0, rows)], ycomp.at[s, pl.ds(0, rows)], sems.at[s]).wait())

    @pl.when(i == 0)
    def _():
        ycomp[...] = jnp.zeros(ycomp.shape, F32)
        start_groups(0, 0)

    wait_groups(i, slot)

    @pl.when(i + 1 < n_steps)
    def _():
        start_groups(i + 1, 1 - slot)

    x = x_ref[...]
    acc = alpha * x + _ple(x.astype(BF16), p_ref[...], wpg_ref, bpg_ref, wpp_ref)

    info_t = info_ref[...]
    info = jnp.concatenate([info_t, jnp.zeros((LANES - SUBLANES, tm), F32)], axis=0).T
    slot_col = lax.broadcasted_iota(jnp.int32, (tm, nc), 1)
    pick = jnp.zeros((tm, nc), F32)
    for k in range(TOP_K):
        s_k = _local_slot(info[:, k:k + 1], info[:, TOP_K + k:TOP_K + k + 1],
                          loff_ref, i * n_exp, n_exp)
        pick = jnp.where(slot_col == s_k, info[:, 4 + k:5 + k], pick)
    acc = acc + _dot(pick.astype(BF16), ycomp[slot].astype(BF16))
    o_ref[...] = _layer_norm(acc, g2_ref[...], b2_ref[...])


def _combine_call(n8, loff, gdst, x, p, layer, info, ys, wpg, bpg, wpp, g2, b2, *, alpha, tm, n_exp):
    t, d = x.shape
    nc = TOP_K * tm + SUBLANES * n_exp
    per_layer = (wpg, bpg, wpp, g2, b2)

    return pl.pallas_call(
        functools.partial(_combine_kernel, alpha=alpha, tm=tm, n_exp=n_exp),
        grid_spec=pltpu.PrefetchScalarGridSpec(
            num_scalar_prefetch=3,
            grid=(t // tm,),
            in_specs=[pl.BlockSpec((tm, d), lambda i, *_: (i, 0)),
                      pl.BlockSpec((None, tm, p.shape[2]), lambda i, *_: (layer, i, 0)),
                      pl.BlockSpec((SUBLANES, tm), lambda i, *_: (0, i)),
                      pl.BlockSpec(memory_space=pl.ANY)]
            + [_layer_spec(c, layer) for c in per_layer],
            out_specs=pl.BlockSpec((tm, d), lambda i, *_: (i, 0)),
            scratch_shapes=[pltpu.VMEM((2, nc, d), F32), pltpu.SemaphoreType.DMA((2,))],
        ),
        out_shape=jax.ShapeDtypeStruct((t, d), F32),
        compiler_params=pltpu.CompilerParams(
            dimension_semantics=("arbitrary",), vmem_limit_bytes=VMEM_LIMIT),
        name="combine",
    )(n8, loff, gdst, x, p, info, ys, *per_layer)


def _moe_layer(x, p, layer, info, cnt, n_exp, we_gate, we_up, we_down, wpg, bpg, wpp, g2, b2,
               *, alpha, tm, tme, fc):
    t, d = x.shape
    n_tok_tiles = t // tm
    assert tm <= ROW_CHUNKS[0] and tme <= ROW_CHUNKS[0]

    i32 = jnp.int32
    n = cnt[:, 0].reshape(n_tok_tiles, SUBLANES)[:, :n_exp].astype(i32)
    n8 = (n + SUBLANES - 1) // SUBLANES * SUBLANES
    loff = jnp.cumsum(n8, axis=1) - n8
    tot = jnp.sum(n8, axis=0)
    reg = (tot + tme - 1) // tme * tme
    reg_end = jnp.cumsum(reg)
    base = reg_end - reg
    gdst = base[None, :] + jnp.cumsum(n8, axis=0) - n8
    n_tiles = (t * TOP_K + n_tok_tiles * n_exp * (SUBLANES - 1)) // tme + n_exp + 1
    n_rows = n_tiles * tme
    tile_start = jnp.arange(n_tiles, dtype=i32) * tme
    tile_e = jnp.minimum(jnp.sum(tile_start[:, None] >= reg_end[None, :], axis=1), n_exp - 1).astype(i32)
    tile_n = jnp.clip(tot[tile_e] - (tile_start - base[tile_e]), 0, tme).astype(i32)
    gap = jnp.stack([base + tot, reg - tot], axis=1).reshape(-1)
    gap = jnp.concatenate([gap, jnp.stack([reg_end[-1], (n_rows - reg_end[-1]) // tme])]).astype(i32)
    n8f, lofff, gdstf = (a.reshape(-1).astype(i32) for a in (n8, loff, gdst))

    xs = _dispatch_call(n8f, lofff, gdstf, gap, x, info, tm=tm, tme=tme, n_exp=n_exp, n_rows=n_rows)
    ys = _expert_call(tile_e, tile_n, xs, we_gate, we_up, we_down, tme=tme, fc=fc)
    return _combine_call(n8f, lofff, gdstf, x, p, layer, info, ys, wpg, bpg, wpp, g2, b2,
                         alpha=alpha, tm=tm, n_exp=n_exp)


def kernel(x, p, w_in, b_in, conf_conv_w, conf_conv_b, conf_ln_g, conf_ln_b, w_conf_out,
           sc_conv_w, w_sc_out, w_o, ln1_g, ln1_b, w_ff_gate, w_ff_up, w_ff_down, w_router,
           b_router, we_gate, we_up, we_down, w_ple_gate, b_ple_gate, w_ple_proj, ln2_g, ln2_b,
           ):
    ts, tm, tme, fc_moe = TILES
    depth = w_in.shape[0]
    alpha = (2 * depth) ** 0.25
    bsz, seq, d = x.shape
    pt = p.reshape(depth, bsz * seq, -1)

    expert_w = (("gate", we_gate), ("up", we_up), ("down", we_down))
    queue = [((i // 2, name), w[i // 2], i) for i in range(depth) if i % 2 == 1 for name, w in expert_w]
    cast = {}
    idle = jnp.zeros((bsz * seq // min(ts, tm) * 2 * SUBLANES, LANES), F32)

    def side_job(layer, is_mixer):
        for k, (key, w, needed_in) in enumerate(queue):
            if layer < needed_in or (layer == needed_in and is_mixer):
                del queue[k]
                return key, w
        return None, idle

    def run(call, *args, layer, is_mixer, **kw):
        key, w = side_job(layer, is_mixer)
        out, w_bf16, *rest = call(*args, w.reshape(-1, w.shape[-1]), **kw)
        if key is not None:
            cast[key] = w_bf16.reshape(w.shape)
        return (out, *rest) if rest else out

    def rows(v):
        return v[:, None, :]

    mixer_w = (w_in.astype(BF16), rows(b_in), conf_conv_w, rows(conf_conv_b), rows(conf_ln_g),
               rows(conf_ln_b), w_conf_out.astype(BF16), sc_conv_w, w_sc_out.astype(BF16),
               w_o.astype(BF16), rows(ln1_g), rows(ln1_b))
    ple_w = (w_ple_gate.astype(BF16), rows(b_ple_gate), w_ple_proj.astype(BF16), rows(ln2_g), rows(ln2_b))
    ffn_w = (w_ff_gate.astype(BF16), w_ff_up.astype(BF16), w_ff_down.astype(BF16))

    assert ts == tm
    for i in range(depth):
        moe = i % 2 == 1
        j = i // 2
        x, info, cnt = run(
            _mixer_call, x, i, *mixer_w,
            w_router[j] if moe else None, b_router[j] if moe else None,
            layer=i, is_mixer=True, alpha=alpha, ts=ts)
        xt = x.reshape(bsz * seq, d)
        if not moe:
            xt = run(_dense_ffn_call, xt, pt, i, j, *ffn_w, *ple_w,
                     layer=i, is_mixer=False, alpha=alpha, tm=tm)
        else:
            queue[:] = [q for q in queue if q[2] != i]
            ew = [cast.pop((j, name)) if (j, name) in cast else w[j].astype(BF16)
                  for name, w in expert_w]
            xt = _moe_layer(xt, pt, i, info, cnt, w_router.shape[2], *ew,
                            *ple_w, alpha=alpha, tm=tm, tme=tme, fc=fc_moe)
        x = xt.reshape(bsz, seq, d)
    return x
```

```python
import functools
from typing import NamedTuple

import jax
import jax.numpy as jnp
from jax import lax
from jax.experimental import pallas as pl
from jax.experimental.pallas import tpu as pltpu

LN_EPS = 1e-5
TOP_K = 2
LANES = 128
SUBLANES = 8
CONV_HALO = 32
SC_HALO = 8
MXU_COLS = 256
CONV_ROWS = 64
NEG_BIG = -1e30
VMEM_LIMIT = 56 * 1024 * 1024

F32 = jnp.float32
BF16 = jnp.bfloat16


class Tiles(NamedTuple):
    seq_rows: int = 512
    token_rows: int = 512
    expert_rows: int = 512
    expert_cols: int = 512


TILES = Tiles()


def _dot(a, b):
    return jnp.dot(a, b, preferred_element_type=F32)


def _sigmoid(x):
    return 1.0 / (1.0 + jnp.exp(-x))


def _layer_norm(x, g, b):
    mu = jnp.mean(x, axis=-1, keepdims=True)
    xc = x - mu
    var = jnp.mean(xc * xc, axis=-1, keepdims=True)
    return xc * lax.rsqrt(var + LN_EPS) * g + b


def _const_spec(shape):
    nd = len(shape)
    return pl.BlockSpec(shape, lambda *_: (0,) * nd, pipeline_mode=pl.Buffered(1))


def _route(x_tile, wr_ref, br_ref, upper_ref, n_exp):
    logits = _dot(x_tile.astype(BF16), wr_ref[...]) + br_ref[...]
    lt = logits.T
    tm = lt.shape[1]
    row = [lt[e:e + 1, :] for e in range(n_exp)]
    zero = jnp.zeros((1, tm), F32)
    m1, i1, m2, i2 = row[0], zero, jnp.full((1, tm), 2.0 * NEG_BIG, F32), zero
    for e in range(1, n_exp):
        first = row[e] > m1
        second = row[e] > m2
        m2 = jnp.where(first, m1, jnp.where(second, row[e], m2))
        i2 = jnp.where(first, i1, jnp.where(second, float(e), i2))
        m1 = jnp.where(first, row[e], m1)
        i1 = jnp.where(first, float(e), i1)
    e21 = jnp.exp(m2 - m1)
    w1 = 1.0 / (1.0 + e21)
    w2 = e21 / (1.0 + e21)

    expert = lax.broadcasted_iota(jnp.int32, (SUBLANES, tm), 0).astype(F32)
    hit1 = expert == i1
    hit2 = expert == i2
    onehot = jnp.where(hit1 | hit2, 1.0, 0.0)
    ranks = _dot(onehot.astype(BF16), upper_ref[...])
    r1 = jnp.sum(jnp.where(hit1, ranks, 0.0), axis=0, keepdims=True)
    r2 = jnp.sum(jnp.where(hit2, ranks, 0.0), axis=0, keepdims=True)
    info = jnp.concatenate([i1, i2, r1, r2, w1, w2, zero, zero], axis=0)
    return info, jnp.sum(onehot, axis=1, keepdims=True)


def _side_cast_specs(side, n_steps, step_index):
    rows, cols = side.shape
    assert rows % (n_steps * 2 * SUBLANES) == 0
    block = (rows // n_steps, cols)
    spec = pl.BlockSpec(block, lambda *idx: (step_index(*idx), 0))
    return spec, spec, jax.ShapeDtypeStruct(side.shape, BF16)


def _causal_dwconv_chunk(src_ref, row0, halo, w_ref, taps, lanes, init):
    acc = init
    first = halo - (taps - 1)
    for r in range(SUBLANES):
        offs = [o for o in range(first, halo + 1) if o % SUBLANES == r]
        if not offs:
            continue
        base = offs[0]
        span = offs[-1] - base + CONV_ROWS
        blk = src_ref[row0 + base:row0 + base + span, lanes]
        part = None
        for o in offs:
            j = o - first
            term = w_ref[j:j + 1, lanes] * blk[o - base:o - base + CONV_ROWS, :]
            part = term if part is None else part + term
        acc = acc + part
    return acc


def _mixer_kernel(x_ref, w_in_ref, b_in_ref, ccw_ref, ccb_ref, clg_ref, clb_ref, wa_ref,
                  scw_ref, wb_ref, wo_ref, g1_ref, b1_ref, wr_ref, br_ref, upper_ref, side_ref,
                  o_ref, side_out_ref, info_ref, cnt_ref,
                  a_ext, u_ext, scb_buf, act_a, act_b, zg_buf, *, alpha, ts, dc, ds, kc, ks, n_exp):
    s = pl.program_id(1)
    side_out_ref[...] = side_ref[...].astype(BF16)

    @pl.when(s == 0)
    def _():
        a_ext[0:CONV_HALO, :] = jnp.zeros((CONV_HALO, dc), F32)
        u_ext[0:SC_HALO, :] = jnp.zeros((SC_HALO, ds), F32)

    @pl.when(s > 0)
    def _():
        a_ext[0:CONV_HALO, :] = a_ext[ts:ts + CONV_HALO, :]
        u_ext[0:SC_HALO, :] = u_ext[ts:ts + SC_HALO, :]

    x = x_ref[0]
    xb = x.astype(BF16)

    c0 = 2 * dc
    c1 = c0 + 3 * ds
    zc = _dot(xb, w_in_ref[:, 0:c0]) + b_in_ref[:, 0:c0]
    a_ext[CONV_HALO:CONV_HALO + ts, :] = zc[:, :dc] * _sigmoid(zc[:, dc:])
    zs = _dot(xb, w_in_ref[:, c0:c1]) + b_in_ref[:, c0:c1]
    scb_buf[...] = zs[:, :ds]
    u_ext[SC_HALO:SC_HALO + ts, :] = zs[:, ds:2 * ds] * zs[:, 2 * ds:]

    d = x.shape[-1]
    n_piece = 2 * d // MXU_COLS
    zg_pieces = []

    def gate_piece(j):
        cols = slice(c1 + j * MXU_COLS, c1 + (j + 1) * MXU_COLS)
        z = _dot(xb, w_in_ref[:, cols]) + b_in_ref[:, cols]
        zg_buf[:, j * MXU_COLS:(j + 1) * MXU_COLS] = z
        zg_pieces.append(z[0:SUBLANES, 0:LANES])

    def after(piece):
        bits = pltpu.bitcast(piece, jnp.uint32)
        return pltpu.bitcast((bits >> 16) >> 16, F32)

    half = ts // 2
    chunks = ts // CONV_ROWS
    groups = dc // LANES
    n_units = (chunks // 2) * groups
    late = []
    outs = []

    def out_proj(rows):
        zg = zg_buf[rows, :]
        y_a = _dot(act_a[rows, :], wa_ref[...])
        y_b = _dot(act_b[rows, :], wb_ref[...])
        late.append(y_a[0:SUBLANES, 0:LANES])
        m = _sigmoid(zg[:, :d]) * y_a + _sigmoid(zg[:, d:]) * y_b
        mix = _dot(m.astype(BF16), wo_ref[...])
        late.append(mix[0:SUBLANES, 0:LANES])
        outs.append(_layer_norm(alpha * x[rows, :] + mix, g1_ref[...], b1_ref[...]))

    for ci in range(chunks):
        row0 = ci * CONV_ROWS
        first_half = ci < chunks // 2
        if ci == chunks // 2:
            out_proj(slice(0, half))
        conv = []
        for cg in range(groups):
            lanes = slice(cg * LANES, (cg + 1) * LANES)
            unit = (ci % (chunks // 2)) * groups + cg
            if first_half:
                want = min(n_piece, -(-(unit + 1) * n_piece // n_units))
                while len(zg_pieces) < want:
                    gate_piece(len(zg_pieces))
                dep = zg_pieces[-1]
            else:
                dep = late[unit * len(late) // n_units]
            init = jnp.broadcast_to(ccb_ref[:, lanes], (CONV_ROWS, LANES))
            init = init + jnp.tile(after(dep), (CONV_ROWS // SUBLANES, 1))
            conv.append(_causal_dwconv_chunk(a_ext, row0, CONV_HALO, ccw_ref, kc, lanes, init))
        conv = jnp.concatenate(conv, axis=-1)
        ln = _layer_norm(conv, clg_ref[...], clb_ref[...])
        act_a[row0:row0 + CONV_ROWS, :] = (ln * _sigmoid(ln)).astype(BF16)
        sconv = []
        for cg in range(ds // LANES):
            lanes = slice(cg * LANES, (cg + 1) * LANES)
            init = jnp.zeros((CONV_ROWS, LANES), F32)
            sconv.append(_causal_dwconv_chunk(u_ext, row0, SC_HALO, scw_ref, ks, lanes, init))
        sconv = jnp.concatenate(sconv, axis=-1)
        act_b[row0:row0 + CONV_ROWS, :] = (scb_buf[row0:row0 + CONV_ROWS, :] * sconv).astype(BF16)
    out_proj(slice(half, ts))

    out = jnp.concatenate(outs, axis=0)
    o_ref[0] = out
    if n_exp:
        info, counts = _route(out, wr_ref, br_ref, upper_ref, n_exp)
        info_ref[...] = info
        cnt_ref[...] = jnp.broadcast_to(counts, cnt_ref.shape)
    else:
        info_ref[...] = jnp.zeros(info_ref.shape, F32)
        cnt_ref[...] = jnp.zeros(cnt_ref.shape, F32)


def _mixer_call(x, w_in, b_in, ccw, ccb, clg, clb, wa, scw, wb, wo, g1, b1, w_router, b_router,
                side, *, alpha, ts):
    bsz, seq, d = x.shape
    kc, dc = ccw.shape
    ks, ds = scw.shape
    assert seq % ts == 0 and ts % (2 * CONV_ROWS) == 0 and ts >= CONV_HALO
    assert kc - 1 <= CONV_HALO and ks - 1 <= SC_HALO
    assert dc % LANES == 0 and ds % LANES == 0 and (2 * d) % MXU_COLS == 0
    n_exp = 0 if w_router is None else w_router.shape[1]
    assert n_exp <= SUBLANES and ts % LANES == 0 and ts < 2 ** 24
    wr = jnp.zeros((d, LANES), BF16)
    br = jnp.full((1, LANES), NEG_BIG, F32)
    if n_exp:
        wr = wr.at[:, :n_exp].set(w_router.astype(BF16))
        br = br.at[0, :n_exp].set(b_router)
    upper = jnp.triu(jnp.ones((ts, ts), BF16), 1)
    kern = functools.partial(_mixer_kernel, alpha=alpha, ts=ts, dc=dc, ds=ds, kc=kc, ks=ks, n_exp=n_exp)
    consts = (w_in, b_in, ccw, ccb, clg, clb, wa, scw, wb, wo, g1, b1, wr, br, upper)
    tiles = seq // ts
    side_in, side_out, side_shape = _side_cast_specs(side, bsz * tiles, lambda b, s: b * tiles + s)
    return pl.pallas_call(
        kern,
        grid=(bsz, tiles),
        in_specs=[pl.BlockSpec((1, ts, d), lambda b, s: (b, s, 0))]
        + [_const_spec(c.shape) for c in consts] + [side_in],
        out_specs=[pl.BlockSpec((1, ts, d), lambda b, s: (b, s, 0)), side_out,
                   pl.BlockSpec((SUBLANES, ts), lambda b, s: (0, b * tiles + s)),
                   pl.BlockSpec((SUBLANES, LANES), lambda b, s: (b * tiles + s, 0))],
        out_shape=[jax.ShapeDtypeStruct((bsz, seq, d), F32), side_shape,
                   jax.ShapeDtypeStruct((SUBLANES, bsz * seq), F32),
                   jax.ShapeDtypeStruct((bsz * tiles * SUBLANES, LANES), F32)],
        scratch_shapes=[
            pltpu.VMEM((ts + CONV_HALO, dc), F32),
            pltpu.VMEM((ts + SC_HALO, ds), F32),
            pltpu.VMEM((ts, ds), F32),
            pltpu.VMEM((ts, dc), BF16),
            pltpu.VMEM((ts, ds), BF16),
            pltpu.VMEM((ts, 2 * d), F32),
        ],
        compiler_params=pltpu.CompilerParams(
            dimension_semantics=("arbitrary", "arbitrary"), vmem_limit_bytes=VMEM_LIMIT),
        name="mixer",
    )(x, *consts, side)


def _ple(xb, p, wpg_ref, bpg_ref, wpp_ref):
    gate = _sigmoid(_dot(xb, wpg_ref[...]) + bpg_ref[...])
    return gate * _dot(p.astype(BF16), wpp_ref[...])


def _dense_ffn_kernel(x_ref, p_ref, wg_ref, wu_ref, wd_ref, wpg_ref, bpg_ref, wpp_ref,
                      g2_ref, b2_ref, side_ref, o_ref, side_out_ref, *, alpha):
    side_out_ref[...] = side_ref[...].astype(BF16)
    x = x_ref[...]
    xb = x.astype(BF16)
    acc = alpha * x + _ple(xb, p_ref[...], wpg_ref, bpg_ref, wpp_ref)
    g = _dot(xb, wg_ref[...])
    u = _dot(xb, wu_ref[...])
    h = (g * _sigmoid(g) * u).astype(BF16)
    acc = acc + _dot(h, wd_ref[...])
    o_ref[...] = _layer_norm(acc, g2_ref[...], b2_ref[...])


def _dense_ffn_call(x, p, layer, wg, wu, wd, wpg, bpg, wpp, g2, b2, side, *, alpha, tm):
    t, d = x.shape
    assert t % tm == 0
    consts = (wg, wu, wd, wpg, bpg, wpp, g2, b2)
    side_in, side_out, side_shape = _side_cast_specs(side, t // tm, lambda i: i)
    return pl.pallas_call(
        functools.partial(_dense_ffn_kernel, alpha=alpha),
        grid=(t // tm,),
        in_specs=[pl.BlockSpec((tm, d), lambda i: (i, 0)),
                  pl.BlockSpec((None, tm, p.shape[2]), lambda i: (layer, i, 0))]
        + [_const_spec(c.shape) for c in consts] + [side_in],
        out_specs=[pl.BlockSpec((tm, d), lambda i: (i, 0)), side_out],
        out_shape=[jax.ShapeDtypeStruct((t, d), F32), side_shape],
        compiler_params=pltpu.CompilerParams(
            dimension_semantics=("arbitrary",), vmem_limit_bytes=VMEM_LIMIT),
        name="dense_ffn",
    )(x, p, *consts, side)


ROW_CHUNKS = (512, 256, 128, 64, 32, 16, 8)


def _for_row_chunks(n, fn):
    off = 0
    for rows in ROW_CHUNKS:
        take = (n & rows) != 0
        pl.when(take)(functools.partial(fn, off, rows))
        off = off + jnp.where(take, rows, 0)


TOTAL_CHUNKS = (1024,) + ROW_CHUNKS


def _wait_rows(n, wait_fn):
    for rows in TOTAL_CHUNKS:
        pl.when((n & rows) != 0)(functools.partial(wait_fn, rows))


def _local_slot(expert, rank, loff_ref, base, n_exp):
    off = jnp.zeros(expert.shape, F32)
    for e in range(n_exp):
        off = jnp.where(expert == float(e), loff_ref[base + e].astype(F32), off)
    return (off + rank).astype(jnp.int32)


def _dispatch_kernel(n8_ref, loff_ref, gdst_ref, gap_ref, x_ref, info_ref, xs_ref,
                     comp, zbuf, sems, *, tm, n_exp):
    i = pl.program_id(0)
    n_steps = pl.num_programs(0)
    slot = i % 2
    nc = comp.shape[1]

    def group_copy(step, e, s, off, rows):
        src = comp.at[s, pl.ds(pl.multiple_of(loff_ref[step * n_exp + e] + off, SUBLANES), rows)]
        dst = xs_ref.at[pl.ds(pl.multiple_of(gdst_ref[step * n_exp + e] + off, SUBLANES), rows)]
        return pltpu.make_async_copy(src, dst, sems.at[s])

    def start_groups(step, s):
        for e in range(n_exp):
            _for_row_chunks(n8_ref[step * n_exp + e],
                            lambda off, rows, e=e: group_copy(step, e, s, off, rows).start())

    def wait_groups(step, s):
        total = n8_ref[step * n_exp]
        for e in range(1, n_exp):
            total = total + n8_ref[step * n_exp + e]
        _wait_rows(total, lambda rows: pltpu.make_async_copy(
            comp.at[s, pl.ds(0, rows)], xs_ref.at[pl.ds(0, rows)], sems.at[s]).wait())

    info_t = info_ref[...]
    slot_row = lax.broadcasted_iota(jnp.int32, (nc, tm), 0)
    hit = None
    for k in range(TOP_K):
        s_k = _local_slot(info_t[k:k + 1, :], info_t[TOP_K + k:TOP_K + k + 1, :],
                          loff_ref, i * n_exp, n_exp)
        hit = (slot_row == s_k) if hit is None else hit | (slot_row == s_k)
    onehot = jnp.where(hit, 1.0, 0.0).astype(BF16)
    comp[slot] = _dot(onehot, x_ref[...].astype(BF16))

    @pl.when(i > 0)
    def _():
        wait_groups(i - 1, 1 - slot)

    start_groups(i, slot)

    @pl.when(i == n_steps - 1)
    def _():
        wait_groups(i, slot)
        zbuf[...] = jnp.zeros(zbuf.shape, F32)

        def gap_copy(e, off, rows):
            dst = xs_ref.at[pl.ds(pl.multiple_of(gap_ref[2 * e] + off, SUBLANES), rows)]
            return pltpu.make_async_copy(zbuf.at[pl.ds(0, rows)], dst, sems.at[0])

        for e in range(n_exp):
            _for_row_chunks(gap_ref[2 * e + 1],
                            lambda off, rows, e=e: gap_copy(e, off, rows).start())
        for e in range(n_exp):
            _for_row_chunks(gap_ref[2 * e + 1],
                            lambda off, rows, e=e: gap_copy(e, off, rows).wait())

        tail_rows = zbuf.shape[0]

        def tail_copy(j):
            start = pl.multiple_of(gap_ref[2 * n_exp] + j * tail_rows, SUBLANES)
            return pltpu.make_async_copy(zbuf, xs_ref.at[pl.ds(start, tail_rows)], sems.at[0])

        def tail(j, c):
            tail_copy(j).start()
            tail_copy(j).wait()
            return c

        lax.fori_loop(0, gap_ref[2 * n_exp + 1], tail, 0)


def _dispatch_call(n8, loff, gdst, gap, x, info, *, tm, tme, n_exp, n_rows):
    t, d = x.shape
    nc = TOP_K * tm + SUBLANES * n_exp
    return pl.pallas_call(
        functools.partial(_dispatch_kernel, tm=tm, n_exp=n_exp),
        grid_spec=pltpu.PrefetchScalarGridSpec(
            num_scalar_prefetch=4,
            grid=(t // tm,),
            in_specs=[pl.BlockSpec((tm, d), lambda i, *_: (i, 0)),
                      pl.BlockSpec((SUBLANES, tm), lambda i, *_: (0, i))],
            out_specs=pl.BlockSpec(memory_space=pl.ANY),
            scratch_shapes=[pltpu.VMEM((2, nc, d), F32),
                            pltpu.VMEM((tme, d), F32),
                            pltpu.SemaphoreType.DMA((2,))],
        ),
        out_shape=jax.ShapeDtypeStruct((n_rows, d), F32),
        compiler_params=pltpu.CompilerParams(
            dimension_semantics=("arbitrary",), vmem_limit_bytes=VMEM_LIMIT),
        name="dispatch",
    )(n8, loff, gdst, gap, x, info)


def _expert_kernel(tile_e_ref, tile_n_ref, xs_ref, wg_ref, wu_ref, wd_ref, ys_ref, *, fc):
    del tile_e_ref
    i = pl.program_id(0)
    n_valid = tile_n_ref[i]
    tme = ys_ref.shape[0]
    half = tme // 2

    def swiglu(rows):
        xb = xs_ref[0:rows, :].astype(BF16)
        f = wg_ref.shape[2]
        acc = jnp.zeros((rows, ys_ref.shape[1]), F32)
        for c in range(f // fc):
            cols = slice(c * fc, (c + 1) * fc)
            g = _dot(xb, wg_ref[0, :, cols])
            u = _dot(xb, wu_ref[0, :, cols])
            h = (g * _sigmoid(g) * u).astype(BF16)
            acc = acc + _dot(h, wd_ref[0, cols, :])
        ys_ref[0:rows, :] = acc

    @pl.when(n_valid > half)
    def _():
        swiglu(tme)

    @pl.when((n_valid > 0) & (n_valid <= half))
    def _():
        swiglu(half)
        ys_ref[half:tme, :] = jnp.zeros((tme - half, ys_ref.shape[1]), F32)

    @pl.when(n_valid == 0)
    def _():
        ys_ref[...] = jnp.zeros(ys_ref.shape, F32)


def _expert_call(tile_e, tile_n, xs, wg, wu, wd, *, tme, fc):
    n_rows, d = xs.shape
    f = wg.shape[2]
    assert n_rows % tme == 0 and f % fc == 0 and fc % LANES == 0

    def w_spec(shape):
        return pl.BlockSpec((1,) + shape[1:], lambda i, te, tn: (te[i], 0, 0))

    return pl.pallas_call(
        functools.partial(_expert_kernel, fc=fc),
        grid_spec=pltpu.PrefetchScalarGridSpec(
            num_scalar_prefetch=2,
            grid=(n_rows // tme,),
            in_specs=[pl.BlockSpec((tme, d), lambda i, te, tn: (i, 0)),
                      w_spec(wg.shape), w_spec(wu.shape), w_spec(wd.shape)],
            out_specs=pl.BlockSpec((tme, d), lambda i, te, tn: (i, 0)),
        ),
        out_shape=jax.ShapeDtypeStruct((n_rows, d), F32),
        compiler_params=pltpu.CompilerParams(
            dimension_semantics=("arbitrary",), vmem_limit_bytes=VMEM_LIMIT),
        name="experts",
    )(tile_e, tile_n, xs, wg, wu, wd)


def _combine_kernel(n8_ref, loff_ref, gdst_ref, x_ref, p_ref, info_ref, ys_ref, wpg_ref, bpg_ref,
                    wpp_ref, g2_ref, b2_ref, o_ref, ycomp, sems, *, alpha, tm, n_exp):
    i = pl.program_id(0)
    n_steps = pl.num_programs(0)
    slot = i % 2
    nc = ycomp.shape[1]

    def group_copy(step, e, s, off, rows):
        src = ys_ref.at[pl.ds(pl.multiple_of(gdst_ref[step * n_exp + e] + off, SUBLANES), rows)]
        dst = ycomp.at[s, pl.ds(pl.multiple_of(loff_ref[step * n_exp + e] + off, SUBLANES), rows)]
        return pltpu.make_async_copy(src, dst, sems.at[s])

    def start_groups(step, s):
        for e in range(n_exp):
            _for_row_chunks(n8_ref[step * n_exp + e],
                            lambda off, rows, e=e: group_copy(step, e, s, off, rows).start())

    def wait_groups(step, s):
        total = n8_ref[step * n_exp]
        for e in range(1, n_exp):
            total = total + n8_ref[step * n_exp + e]
        _wait_rows(total, lambda rows: pltpu.make_async_copy(
            ys_ref.at[pl.ds(0, rows)], ycomp.at[s, pl.ds(0, rows)], sems.at[s]).wait())

    @pl.when(i == 0)
    def _():
        ycomp[...] = jnp.zeros(ycomp.shape, F32)
        start_groups(0, 0)

    wait_groups(i, slot)

    @pl.when(i + 1 < n_steps)
    def _():
        start_groups(i + 1, 1 - slot)

    x = x_ref[...]
    acc = alpha * x + _ple(x.astype(BF16), p_ref[...], wpg_ref, bpg_ref, wpp_ref)

    info_t = info_ref[...]
    info = jnp.concatenate([info_t, jnp.zeros((LANES - SUBLANES, tm), F32)], axis=0).T
    slot_col = lax.broadcasted_iota(jnp.int32, (tm, nc), 1)
    pick = jnp.zeros((tm, nc), F32)
    for k in range(TOP_K):
        s_k = _local_slot(info[:, k:k + 1], info[:, TOP_K + k:TOP_K + k + 1],
                          loff_ref, i * n_exp, n_exp)
        pick = jnp.where(slot_col == s_k, info[:, 4 + k:5 + k], pick)
    acc = acc + _dot(pick.astype(BF16), ycomp[slot].astype(BF16))
    o_ref[...] = _layer_norm(acc, g2_ref[...], b2_ref[...])


def _combine_call(n8, loff, gdst, x, p, layer, info, ys, wpg, bpg, wpp, g2, b2, *, alpha, tm, n_exp):
    t, d = x.shape
    nc = TOP_K * tm + SUBLANES * n_exp
    consts = (wpg, bpg, wpp, g2, b2)

    def const_spec(shape):
        nd = len(shape)
        return pl.BlockSpec(shape, lambda i, *_: (0,) * nd, pipeline_mode=pl.Buffered(1))

    return pl.pallas_call(
        functools.partial(_combine_kernel, alpha=alpha, tm=tm, n_exp=n_exp),
        grid_spec=pltpu.PrefetchScalarGridSpec(
            num_scalar_prefetch=3,
            grid=(t // tm,),
            in_specs=[pl.BlockSpec((tm, d), lambda i, *_: (i, 0)),
                      pl.BlockSpec((None, tm, p.shape[2]), lambda i, *_: (layer, i, 0)),
                      pl.BlockSpec((SUBLANES, tm), lambda i, *_: (0, i)),
                      pl.BlockSpec(memory_space=pl.ANY)]
            + [const_spec(c.shape) for c in consts],
            out_specs=pl.BlockSpec((tm, d), lambda i, *_: (i, 0)),
            scratch_shapes=[pltpu.VMEM((2, nc, d), F32), pltpu.SemaphoreType.DMA((2,))],
        ),
        out_shape=jax.ShapeDtypeStruct((t, d), F32),
        compiler_params=pltpu.CompilerParams(
            dimension_semantics=("arbitrary",), vmem_limit_bytes=VMEM_LIMIT),
        name="combine",
    )(n8, loff, gdst, x, p, info, ys, *consts)


def _moe_layer(x, p, layer, info, cnt, n_exp, we_gate, we_up, we_down, wpg, bpg, wpp, g2, b2,
               *, alpha, tm, tme, fc):
    t, d = x.shape
    n_tok_tiles = t // tm
    assert tm <= ROW_CHUNKS[0] and tme <= ROW_CHUNKS[0]

    i32 = jnp.int32
    n = cnt[:, 0].reshape(n_tok_tiles, SUBLANES)[:, :n_exp].astype(i32)
    n8 = (n + SUBLANES - 1) // SUBLANES * SUBLANES
    loff = jnp.cumsum(n8, axis=1) - n8
    tot = jnp.sum(n8, axis=0)
    reg = (tot + tme - 1) // tme * tme
    reg_end = jnp.cumsum(reg)
    base = reg_end - reg
    gdst = base[None, :] + jnp.cumsum(n8, axis=0) - n8
    n_tiles = (t * TOP_K + n_tok_tiles * n_exp * (SUBLANES - 1)) // tme + n_exp + 1
    n_rows = n_tiles * tme
    tile_start = jnp.arange(n_tiles, dtype=i32) * tme
    tile_e = jnp.minimum(jnp.sum(tile_start[:, None] >= reg_end[None, :], axis=1), n_exp - 1).astype(i32)
    mine = tile_e[:, None] == jnp.arange(n_exp, dtype=i32)[None, :]
    left = jnp.sum(jnp.where(mine, (base + tot)[None, :], 0), axis=1) - tile_start
    tile_n = jnp.clip(left, 0, tme).astype(i32)
    gap = jnp.stack([base + tot, reg - tot], axis=1).reshape(-1)
    gap = jnp.concatenate([gap, jnp.stack([reg_end[-1], (n_rows - reg_end[-1]) // tme])]).astype(i32)
    n8f, lofff, gdstf = (a.reshape(-1).astype(i32) for a in (n8, loff, gdst))

    xs = _dispatch_call(n8f, lofff, gdstf, gap, x, info, tm=tm, tme=tme, n_exp=n_exp, n_rows=n_rows)
    ys = _expert_call(tile_e, tile_n, xs, we_gate, we_up, we_down, tme=tme, fc=fc)
    return _combine_call(n8f, lofff, gdstf, x, p, layer, info, ys, wpg, bpg, wpp, g2, b2,
                         alpha=alpha, tm=tm, n_exp=n_exp)


def _row(v):
    return v.reshape(1, -1)


def kernel(x, p, w_in, b_in, conf_conv_w, conf_conv_b, conf_ln_g, conf_ln_b, w_conf_out,
           sc_conv_w, w_sc_out, w_o, ln1_g, ln1_b, w_ff_gate, w_ff_up, w_ff_down, w_router,
           b_router, we_gate, we_up, we_down, w_ple_gate, b_ple_gate, w_ple_proj, ln2_g, ln2_b,
           ):
    ts, tm, tme, fc_moe = TILES
    depth = w_in.shape[0]
    alpha = (2 * depth) ** 0.25
    bsz, seq, d = x.shape
    pt = p.reshape(depth, bsz * seq, -1)

    expert_w = (("gate", we_gate), ("up", we_up), ("down", we_down))
    queue = [((i // 2, name), w[i // 2], i) for i in range(depth) if i % 2 == 1 for name, w in expert_w]
    cast = {}
    idle = jnp.zeros((bsz * seq // min(ts, tm) * 2 * SUBLANES, LANES), F32)

    def side_job(layer, is_mixer):
        for k, (key, w, needed_in) in enumerate(queue):
            if layer < needed_in or (layer == needed_in and is_mixer):
                del queue[k]
                return key, w
        return None, idle

    def run(call, *args, layer, is_mixer, **kw):
        key, w = side_job(layer, is_mixer)
        out, w_bf16, *rest = call(*args, w.reshape(-1, w.shape[-1]), **kw)
        if key is not None:
            cast[key] = w_bf16.reshape(w.shape)
        return (out, *rest) if rest else out

    assert ts == tm
    for i in range(depth):
        moe = i % 2 == 1
        x, info, cnt = run(
            _mixer_call,
            x, w_in[i].astype(BF16), _row(b_in[i]), conf_conv_w[i], _row(conf_conv_b[i]),
            _row(conf_ln_g[i]), _row(conf_ln_b[i]), w_conf_out[i].astype(BF16), sc_conv_w[i],
            w_sc_out[i].astype(BF16), w_o[i].astype(BF16), _row(ln1_g[i]), _row(ln1_b[i]),
            w_router[i // 2] if moe else None, b_router[i // 2] if moe else None,
            layer=i, is_mixer=True, alpha=alpha, ts=ts)
        xt = x.reshape(bsz * seq, d)
        ple_w = (w_ple_gate[i].astype(BF16), _row(b_ple_gate[i]), w_ple_proj[i].astype(BF16),
                 _row(ln2_g[i]), _row(ln2_b[i]))
        j = i // 2
        if i % 2 == 0:
            xt = run(_dense_ffn_call,
                     xt, pt, i, w_ff_gate[j].astype(BF16), w_ff_up[j].astype(BF16),
                     w_ff_down[j].astype(BF16), *ple_w, layer=i, is_mixer=False, alpha=alpha, tm=tm)
        else:
            queue[:] = [q for q in queue if q[2] != i]
            ew = [cast.pop((j, name)) if (j, name) in cast else w[j].astype(BF16)
                  for name, w in expert_w]
            xt = _moe_layer(xt, pt, i, info, cnt, w_router.shape[2], *ew,
                            *ple_w, alpha=alpha, tm=tm, tme=tme, fc=fc_moe)
        x = xt.reshape(bsz, seq, d)
    return x
```

```python
import functools
from typing import NamedTuple

import jax
import jax.numpy as jnp
from jax import lax
from jax.experimental import pallas as pl
from jax.experimental.pallas import tpu as pltpu

LN_EPS = 1e-5
TOP_K = 2
LANES = 128
SUBLANES = 8
CONV_HALO = 32
SC_HALO = 8
MXU_COLS = 256
CONV_ROWS = 64
NEG_BIG = -1e30
VMEM_LIMIT = 56 * 1024 * 1024

F32 = jnp.float32
BF16 = jnp.bfloat16


class Tiles(NamedTuple):
    seq_rows: int = 512
    token_rows: int = 512
    expert_rows: int = 512
    expert_cols: int = 512


TILES = Tiles()


def _dot(a, b):
    return jnp.dot(a, b, preferred_element_type=F32)


def _sigmoid(x):
    return 1.0 / (1.0 + jnp.exp(-x))


def _layer_norm(x, g, b):
    mu = jnp.mean(x, axis=-1, keepdims=True)
    xc = x - mu
    var = jnp.mean(xc * xc, axis=-1, keepdims=True)
    return xc * lax.rsqrt(var + LN_EPS) * g + b


def _const_spec(shape):
    nd = len(shape)
    return pl.BlockSpec(shape, lambda *_: (0,) * nd, pipeline_mode=pl.Buffered(1))


def _route(x_tile, wr_ref, br_ref, upper_ref, n_exp):
    logits = _dot(x_tile.astype(BF16), wr_ref[...]) + br_ref[...]
    lt = logits.T
    tm = lt.shape[1]
    row = [lt[e:e + 1, :] for e in range(n_exp)]
    zero = jnp.zeros((1, tm), F32)
    m1, i1, m2, i2 = row[0], zero, jnp.full((1, tm), 2.0 * NEG_BIG, F32), zero
    for e in range(1, n_exp):
        first = row[e] > m1
        second = row[e] > m2
        m2 = jnp.where(first, m1, jnp.where(second, row[e], m2))
        i2 = jnp.where(first, i1, jnp.where(second, float(e), i2))
        m1 = jnp.where(first, row[e], m1)
        i1 = jnp.where(first, float(e), i1)
    e21 = jnp.exp(m2 - m1)
    w1 = 1.0 / (1.0 + e21)
    w2 = e21 / (1.0 + e21)

    expert = lax.broadcasted_iota(jnp.int32, (SUBLANES, tm), 0).astype(F32)
    hit1 = expert == i1
    hit2 = expert == i2
    onehot = jnp.where(hit1 | hit2, 1.0, 0.0)
    ranks = _dot(onehot.astype(BF16), upper_ref[...])
    r1 = jnp.sum(jnp.where(hit1, ranks, 0.0), axis=0, keepdims=True)
    r2 = jnp.sum(jnp.where(hit2, ranks, 0.0), axis=0, keepdims=True)
    info = jnp.concatenate([i1, i2, r1, r2, w1, w2, zero, zero], axis=0)
    return info, jnp.sum(onehot, axis=1, keepdims=True)


def _side_cast_specs(side, n_steps, step_index):
    rows, cols = side.shape
    assert rows % (n_steps * 2 * SUBLANES) == 0
    block = (rows // n_steps, cols)
    spec = pl.BlockSpec(block, lambda *idx: (step_index(*idx), 0))
    return spec, spec, jax.ShapeDtypeStruct(side.shape, BF16)


def _causal_dwconv_chunk(src_ref, row0, halo, w_ref, taps, lanes, init):
    acc = init
    first = halo - (taps - 1)
    for r in range(SUBLANES):
        offs = [o for o in range(first, halo + 1) if o % SUBLANES == r]
        if not offs:
            continue
        base = offs[0]
        span = offs[-1] - base + CONV_ROWS
        blk = src_ref[row0 + base:row0 + base + span, lanes]
        part = None
        for o in offs:
            j = o - first
            term = w_ref[j:j + 1, lanes] * blk[o - base:o - base + CONV_ROWS, :]
            part = term if part is None else part + term
        acc = acc + part
    return acc


def _mixer_kernel(x_ref, w_in_ref, b_in_ref, ccw_ref, ccb_ref, clg_ref, clb_ref, wa_ref,
                  scw_ref, wb_ref, wo_ref, g1_ref, b1_ref, wr_ref, br_ref, upper_ref, side_ref,
                  o_ref, side_out_ref, info_ref, cnt_ref,
                  a_ext, u_ext, scb_buf, act_a, act_b, zg_buf, *, alpha, ts, dc, ds, kc, ks, n_exp):
    s = pl.program_id(1)
    side_out_ref[...] = side_ref[...].astype(BF16)

    @pl.when(s == 0)
    def _():
        a_ext[0:CONV_HALO, :] = jnp.zeros((CONV_HALO, dc), F32)
        u_ext[0:SC_HALO, :] = jnp.zeros((SC_HALO, ds), F32)

    @pl.when(s > 0)
    def _():
        a_ext[0:CONV_HALO, :] = a_ext[ts:ts + CONV_HALO, :]
        u_ext[0:SC_HALO, :] = u_ext[ts:ts + SC_HALO, :]

    x = x_ref[0]
    xb = x.astype(BF16)

    c0 = 2 * dc
    c1 = c0 + 3 * ds
    zc = _dot(xb, w_in_ref[:, 0:c0]) + b_in_ref[:, 0:c0]
    a_ext[CONV_HALO:CONV_HALO + ts, :] = zc[:, :dc] * _sigmoid(zc[:, dc:])
    zs = _dot(xb, w_in_ref[:, c0:c1]) + b_in_ref[:, c0:c1]
    scb_buf[...] = zs[:, :ds]
    u_ext[SC_HALO:SC_HALO + ts, :] = zs[:, ds:2 * ds] * zs[:, 2 * ds:]

    d = x.shape[-1]
    n_piece = 2 * d // MXU_COLS
    zg_pieces = []

    def gate_piece(j):
        cols = slice(c1 + j * MXU_COLS, c1 + (j + 1) * MXU_COLS)
        z = _dot(xb, w_in_ref[:, cols]) + b_in_ref[:, cols]
        zg_buf[:, j * MXU_COLS:(j + 1) * MXU_COLS] = z
        zg_pieces.append(z[0:SUBLANES, 0:LANES])

    def after(piece):
        bits = pltpu.bitcast(piece, jnp.uint32)
        return pltpu.bitcast((bits >> 16) >> 16, F32)

    half = ts // 2
    chunks = ts // CONV_ROWS
    groups = dc // LANES
    n_units = (chunks // 2) * groups
    late = []
    outs = []

    def out_proj(rows):
        zg = zg_buf[rows, :]
        y_a = _dot(act_a[rows, :], wa_ref[...])
        y_b = _dot(act_b[rows, :], wb_ref[...])
        late.append(y_a[0:SUBLANES, 0:LANES])
        m = _sigmoid(zg[:, :d]) * y_a + _sigmoid(zg[:, d:]) * y_b
        mix = _dot(m.astype(BF16), wo_ref[...])
        late.append(mix[0:SUBLANES, 0:LANES])
        outs.append(_layer_norm(alpha * x[rows, :] + mix, g1_ref[...], b1_ref[...]))

    for ci in range(chunks):
        row0 = ci * CONV_ROWS
        first_half = ci < chunks // 2
        if ci == chunks // 2:
            out_proj(slice(0, half))
        conv = []
        for cg in range(groups):
            lanes = slice(cg * LANES, (cg + 1) * LANES)
            unit = (ci % (chunks // 2)) * groups + cg
            if first_half:
                want = min(n_piece, -(-(unit + 1) * n_piece // n_units))
                while len(zg_pieces) < want:
                    gate_piece(len(zg_pieces))
                dep = zg_pieces[-1]
            else:
                dep = late[unit * len(late) // n_units]
            init = jnp.broadcast_to(ccb_ref[:, lanes], (CONV_ROWS, LANES))
            init = init + jnp.tile(after(dep), (CONV_ROWS // SUBLANES, 1))
            conv.append(_causal_dwconv_chunk(a_ext, row0, CONV_HALO, ccw_ref, kc, lanes, init))
        conv = jnp.concatenate(conv, axis=-1)
        ln = _layer_norm(conv, clg_ref[...], clb_ref[...])
        act_a[row0:row0 + CONV_ROWS, :] = (ln * _sigmoid(ln)).astype(BF16)
        sconv = []
        for cg in range(ds // LANES):
            lanes = slice(cg * LANES, (cg + 1) * LANES)
            init = jnp.zeros((CONV_ROWS, LANES), F32)
            sconv.append(_causal_dwconv_chunk(u_ext, row0, SC_HALO, scw_ref, ks, lanes, init))
        sconv = jnp.concatenate(sconv, axis=-1)
        act_b[row0:row0 + CONV_ROWS, :] = (scb_buf[row0:row0 + CONV_ROWS, :] * sconv).astype(BF16)
    out_proj(slice(half, ts))

    out = jnp.concatenate(outs, axis=0)
    o_ref[0] = out
    if n_exp:
        info, counts = _route(out, wr_ref, br_ref, upper_ref, n_exp)
        info_ref[...] = info
        cnt_ref[...] = jnp.broadcast_to(counts, cnt_ref.shape)
    else:
        info_ref[...] = jnp.zeros(info_ref.shape, F32)
        cnt_ref[...] = jnp.zeros(cnt_ref.shape, F32)


def _mixer_call(x, w_in, b_in, ccw, ccb, clg, clb, wa, scw, wb, wo, g1, b1, w_router, b_router,
                side, *, alpha, ts):
    bsz, seq, d = x.shape
    kc, dc = ccw.shape
    ks, ds = scw.shape
    assert seq % ts == 0 and ts % (2 * CONV_ROWS) == 0 and ts >= CONV_HALO
    assert kc - 1 <= CONV_HALO and ks - 1 <= SC_HALO
    assert dc % LANES == 0 and ds % LANES == 0 and (2 * d) % MXU_COLS == 0
    n_exp = 0 if w_router is None else w_router.shape[1]
    assert n_exp <= SUBLANES and ts % LANES == 0 and ts < 2 ** 24
    wr = jnp.zeros((d, LANES), BF16)
    br = jnp.full((1, LANES), NEG_BIG, F32)
    if n_exp:
        wr = wr.at[:, :n_exp].set(w_router.astype(BF16))
        br = br.at[0, :n_exp].set(b_router)
    upper = jnp.triu(jnp.ones((ts, ts), BF16), 1)
    kern = functools.partial(_mixer_kernel, alpha=alpha, ts=ts, dc=dc, ds=ds, kc=kc, ks=ks, n_exp=n_exp)
    consts = (w_in, b_in, ccw, ccb, clg, clb, wa, scw, wb, wo, g1, b1, wr, br, upper)
    tiles = seq // ts
    side_in, side_out, side_shape = _side_cast_specs(side, bsz * tiles, lambda b, s: b * tiles + s)
    return pl.pallas_call(
        kern,
        grid=(bsz, tiles),
        in_specs=[pl.BlockSpec((1, ts, d), lambda b, s: (b, s, 0))]
        + [_const_spec(c.shape) for c in consts] + [side_in],
        out_specs=[pl.BlockSpec((1, ts, d), lambda b, s: (b, s, 0)), side_out,
                   pl.BlockSpec((SUBLANES, ts), lambda b, s: (0, b * tiles + s)),
                   pl.BlockSpec((SUBLANES, LANES), lambda b, s: (b * tiles + s, 0))],
        out_shape=[jax.ShapeDtypeStruct((bsz, seq, d), F32), side_shape,
                   jax.ShapeDtypeStruct((SUBLANES, bsz * seq), F32),
                   jax.ShapeDtypeStruct((bsz * tiles * SUBLANES, LANES), F32)],
        scratch_shapes=[
            pltpu.VMEM((ts + CONV_HALO, dc), F32),
            pltpu.VMEM((ts + SC_HALO, ds), F32),
            pltpu.VMEM((ts, ds), F32),
            pltpu.VMEM((ts, dc), BF16),
            pltpu.VMEM((ts, ds), BF16),
            pltpu.VMEM((ts, 2 * d), F32),
        ],
        compiler_params=pltpu.CompilerParams(
            dimension_semantics=("arbitrary", "arbitrary"), vmem_limit_bytes=VMEM_LIMIT),
        name="mixer",
    )(x, *consts, side)


def _ple(xb, p, wpg_ref, bpg_ref, wpp_ref):
    gate = _sigmoid(_dot(xb, wpg_ref[...]) + bpg_ref[...])
    return gate * _dot(p.astype(BF16), wpp_ref[...])


def _dense_ffn_kernel(x_ref, p_ref, wg_ref, wu_ref, wd_ref, wpg_ref, bpg_ref, wpp_ref,
                      g2_ref, b2_ref, side_ref, o_ref, side_out_ref, *, alpha):
    side_out_ref[...] = side_ref[...].astype(BF16)
    x = x_ref[...]
    xb = x.astype(BF16)
    acc = alpha * x + _ple(xb, p_ref[...], wpg_ref, bpg_ref, wpp_ref)
    g = _dot(xb, wg_ref[...])
    u = _dot(xb, wu_ref[...])
    h = (g * _sigmoid(g) * u).astype(BF16)
    acc = acc + _dot(h, wd_ref[...])
    o_ref[...] = _layer_norm(acc, g2_ref[...], b2_ref[...])


def _dense_ffn_call(x, p, layer, wg, wu, wd, wpg, bpg, wpp, g2, b2, side, *, alpha, tm):
    t, d = x.shape
    assert t % tm == 0
    consts = (wg, wu, wd, wpg, bpg, wpp, g2, b2)
    side_in, side_out, side_shape = _side_cast_specs(side, t // tm, lambda i: i)
    return pl.pallas_call(
        functools.partial(_dense_ffn_kernel, alpha=alpha),
        grid=(t // tm,),
        in_specs=[pl.BlockSpec((tm, d), lambda i: (i, 0)),
                  pl.BlockSpec((None, tm, p.shape[2]), lambda i: (layer, i, 0))]
        + [_const_spec(c.shape) for c in consts] + [side_in],
        out_specs=[pl.BlockSpec((tm, d), lambda i: (i, 0)), side_out],
        out_shape=[jax.ShapeDtypeStruct((t, d), F32), side_shape],
        compiler_params=pltpu.CompilerParams(
            dimension_semantics=("arbitrary",), vmem_limit_bytes=VMEM_LIMIT),
        name="dense_ffn",
    )(x, p, *consts, side)


ROW_CHUNKS = (512, 256, 128, 64, 32, 16, 8)


def _for_row_chunks(n, fn):
    off = 0
    for rows in ROW_CHUNKS:
        take = (n & rows) != 0
        pl.when(take)(functools.partial(fn, off, rows))
        off = off + jnp.where(take, rows, 0)


TOTAL_CHUNKS = (1024,) + ROW_CHUNKS


def _wait_rows(n, wait_fn):
    for rows in TOTAL_CHUNKS:
        pl.when((n & rows) != 0)(functools.partial(wait_fn, rows))


def _local_slot(expert, rank, loff_ref, base, n_exp):
    off = jnp.zeros(expert.shape, F32)
    for e in range(n_exp):
        off = jnp.where(expert == float(e), loff_ref[base + e].astype(F32), off)
    return (off + rank).astype(jnp.int32)


def _dispatch_kernel(n8_ref, loff_ref, gdst_ref, gap_ref, x_ref, info_ref, xs_ref,
                     comp, zbuf, sems, *, tm, n_exp):
    i = pl.program_id(0)
    n_steps = pl.num_programs(0)
    slot = i % 2
    nc = comp.shape[1]

    def group_copy(step, e, s, off, rows):
        src = comp.at[s, pl.ds(pl.multiple_of(loff_ref[step * n_exp + e] + off, SUBLANES), rows)]
        dst = xs_ref.at[pl.ds(pl.multiple_of(gdst_ref[step * n_exp + e] + off, SUBLANES), rows)]
        return pltpu.make_async_copy(src, dst, sems.at[s])

    def start_groups(step, s):
        for e in range(n_exp):
            _for_row_chunks(n8_ref[step * n_exp + e],
                            lambda off, rows, e=e: group_copy(step, e, s, off, rows).start())

    def wait_groups(step, s):
        total = n8_ref[step * n_exp]
        for e in range(1, n_exp):
            total = total + n8_ref[step * n_exp + e]
        _wait_rows(total, lambda rows: pltpu.make_async_copy(
            comp.at[s, pl.ds(0, rows)], xs_ref.at[pl.ds(0, rows)], sems.at[s]).wait())

    info_t = info_ref[...]
    slot_row = lax.broadcasted_iota(jnp.int32, (nc, tm), 0)
    hit = None
    for k in range(TOP_K):
        s_k = _local_slot(info_t[k:k + 1, :], info_t[TOP_K + k:TOP_K + k + 1, :],
                          loff_ref, i * n_exp, n_exp)
        hit = (slot_row == s_k) if hit is None else hit | (slot_row == s_k)
    onehot = jnp.where(hit, 1.0, 0.0).astype(BF16)
    comp[slot] = _dot(onehot, x_ref[...].astype(BF16))

    @pl.when(i > 0)
    def _():
        wait_groups(i - 1, 1 - slot)

    start_groups(i, slot)

    @pl.when(i == n_steps - 1)
    def _():
        wait_groups(i, slot)
        zbuf[...] = jnp.zeros(zbuf.shape, F32)

        def gap_copy(e, off, rows):
            dst = xs_ref.at[pl.ds(pl.multiple_of(gap_ref[2 * e] + off, SUBLANES), rows)]
            return pltpu.make_async_copy(zbuf.at[pl.ds(0, rows)], dst, sems.at[0])

        for e in range(n_exp):
            _for_row_chunks(gap_ref[2 * e + 1],
                            lambda off, rows, e=e: gap_copy(e, off, rows).start())
        for e in range(n_exp):
            _for_row_chunks(gap_ref[2 * e + 1],
                            lambda off, rows, e=e: gap_copy(e, off, rows).wait())

        tail_rows = zbuf.shape[0]

        def tail_copy(j):
            start = pl.multiple_of(gap_ref[2 * n_exp] + j * tail_rows, SUBLANES)
            return pltpu.make_async_copy(zbuf, xs_ref.at[pl.ds(start, tail_rows)], sems.at[0])

        def tail(j, c):
            tail_copy(j).start()
            tail_copy(j).wait()
            return c

        lax.fori_loop(0, gap_ref[2 * n_exp + 1], tail, 0)


def _dispatch_call(n8, loff, gdst, gap, x, info, *, tm, tme, n_exp, n_rows):
    t, d = x.shape
    nc = TOP_K * tm + SUBLANES * n_exp
    return pl.pallas_call(
        functools.partial(_dispatch_kernel, tm=tm, n_exp=n_exp),
        grid_spec=pltpu.PrefetchScalarGridSpec(
            num_scalar_prefetch=4,
            grid=(t // tm,),
            in_specs=[pl.BlockSpec((tm, d), lambda i, *_: (i, 0)),
                      pl.BlockSpec((SUBLANES, tm), lambda i, *_: (0, i))],
            out_specs=pl.BlockSpec(memory_space=pl.ANY),
            scratch_shapes=[pltpu.VMEM((2, nc, d), F32),
                            pltpu.VMEM((tme, d), F32),
                            pltpu.SemaphoreType.DMA((2,))],
        ),
        out_shape=jax.ShapeDtypeStruct((n_rows, d), F32),
        compiler_params=pltpu.CompilerParams(
            dimension_semantics=("arbitrary",), vmem_limit_bytes=VMEM_LIMIT),
        name="dispatch",
    )(n8, loff, gdst, gap, x, info)


def _expert_kernel(tile_e_ref, tile_n_ref, xs_ref, wg_ref, wu_ref, wd_ref, ys_ref, *, fc):
    del tile_e_ref
    i = pl.program_id(0)
    n_valid = tile_n_ref[i]
    tme = ys_ref.shape[0]

    def swiglu(rows):
        xb = xs_ref[0:rows, :].astype(BF16)
        f = wg_ref.shape[2]
        acc = jnp.zeros((rows, ys_ref.shape[1]), F32)
        for c in range(f // fc):
            cols = slice(c * fc, (c + 1) * fc)
            g = _dot(xb, wg_ref[0, :, cols])
            u = _dot(xb, wu_ref[0, :, cols])
            h = (g * _sigmoid(g) * u).astype(BF16)
            acc = acc + _dot(h, wd_ref[0, cols, :])
        ys_ref[0:rows, :] = acc

    quarter = tme // 4
    for rows in range(quarter, tme + 1, quarter):
        @pl.when((n_valid > rows - quarter) & (n_valid <= rows))
        def _(rows=rows):
            swiglu(rows)
            if rows < tme:
                ys_ref[rows:tme, :] = jnp.zeros((tme - rows, ys_ref.shape[1]), F32)

    @pl.when(n_valid == 0)
    def _():
        ys_ref[...] = jnp.zeros(ys_ref.shape, F32)


def _expert_call(tile_e, tile_n, xs, wg, wu, wd, *, tme, fc):
    n_rows, d = xs.shape
    f = wg.shape[2]
    assert n_rows % tme == 0 and f % fc == 0 and fc % LANES == 0

    def w_spec(shape):
        return pl.BlockSpec((1,) + shape[1:], lambda i, te, tn: (te[i], 0, 0))

    return pl.pallas_call(
        functools.partial(_expert_kernel, fc=fc),
        grid_spec=pltpu.PrefetchScalarGridSpec(
            num_scalar_prefetch=2,
            grid=(n_rows // tme,),
            in_specs=[pl.BlockSpec((tme, d), lambda i, te, tn: (i, 0)),
                      w_spec(wg.shape), w_spec(wu.shape), w_spec(wd.shape)],
            out_specs=pl.BlockSpec((tme, d), lambda i, te, tn: (i, 0)),
        ),
        out_shape=jax.ShapeDtypeStruct((n_rows, d), F32),
        compiler_params=pltpu.CompilerParams(
            dimension_semantics=("arbitrary",), vmem_limit_bytes=VMEM_LIMIT),
        name="experts",
    )(tile_e, tile_n, xs, wg, wu, wd)


def _combine_kernel(n8_ref, loff_ref, gdst_ref, x_ref, p_ref, info_ref, ys_ref, wpg_ref, bpg_ref,
                    wpp_ref, g2_ref, b2_ref, o_ref, ycomp, sems, *, alpha, tm, n_exp):
    i = pl.program_id(0)
    n_steps = pl.num_programs(0)
    slot = i % 2
    nc = ycomp.shape[1]

    def group_copy(step, e, s, off, rows):
        src = ys_ref.at[pl.ds(pl.multiple_of(gdst_ref[step * n_exp + e] + off, SUBLANES), rows)]
        dst = ycomp.at[s, pl.ds(pl.multiple_of(loff_ref[step * n_exp + e] + off, SUBLANES), rows)]
        return pltpu.make_async_copy(src, dst, sems.at[s])

    def start_groups(step, s):
        for e in range(n_exp):
            _for_row_chunks(n8_ref[step * n_exp + e],
                            lambda off, rows, e=e: group_copy(step, e, s, off, rows).start())

    def wait_groups(step, s):
        total = n8_ref[step * n_exp]
        for e in range(1, n_exp):
            total = total + n8_ref[step * n_exp + e]
        _wait_rows(total, lambda rows: pltpu.make_async_copy(
            ys_ref.at[pl.ds(0, rows)], ycomp.at[s, pl.ds(0, rows)], sems.at[s]).wait())

    @pl.when(i == 0)
    def _():
        ycomp[...] = jnp.zeros(ycomp.shape, F32)
        start_groups(0, 0)

    wait_groups(i, slot)

    @pl.when(i + 1 < n_steps)
    def _():
        start_groups(i + 1, 1 - slot)

    x = x_ref[...]
    acc = alpha * x + _ple(x.astype(BF16), p_ref[...], wpg_ref, bpg_ref, wpp_ref)

    info_t = info_ref[...]
    info = jnp.concatenate([info_t, jnp.zeros((LANES - SUBLANES, tm), F32)], axis=0).T
    slot_col = lax.broadcasted_iota(jnp.int32, (tm, nc), 1)
    pick = jnp.zeros((tm, nc), F32)
    for k in range(TOP_K):
        s_k = _local_slot(info[:, k:k + 1], info[:, TOP_K + k:TOP_K + k + 1],
                          loff_ref, i * n_exp, n_exp)
        pick = jnp.where(slot_col == s_k, info[:, 4 + k:5 + k], pick)
    acc = acc + _dot(pick.astype(BF16), ycomp[slot].astype(BF16))
    o_ref[...] = _layer_norm(acc, g2_ref[...], b2_ref[...])


def _combine_call(n8, loff, gdst, x, p, layer, info, ys, wpg, bpg, wpp, g2, b2, *, alpha, tm, n_exp):
    t, d = x.shape
    nc = TOP_K * tm + SUBLANES * n_exp
    consts = (wpg, bpg, wpp, g2, b2)

    def const_spec(shape):
        nd = len(shape)
        return pl.BlockSpec(shape, lambda i, *_: (0,) * nd, pipeline_mode=pl.Buffered(1))

    return pl.pallas_call(
        functools.partial(_combine_kernel, alpha=alpha, tm=tm, n_exp=n_exp),
        grid_spec=pltpu.PrefetchScalarGridSpec(
            num_scalar_prefetch=3,
            grid=(t // tm,),
            in_specs=[pl.BlockSpec((tm, d), lambda i, *_: (i, 0)),
                      pl.BlockSpec((None, tm, p.shape[2]), lambda i, *_: (layer, i, 0)),
                      pl.BlockSpec((SUBLANES, tm), lambda i, *_: (0, i)),
                      pl.BlockSpec(memory_space=pl.ANY)]
            + [const_spec(c.shape) for c in consts],
            out_specs=pl.BlockSpec((tm, d), lambda i, *_: (i, 0)),
            scratch_shapes=[pltpu.VMEM((2, nc, d), F32), pltpu.SemaphoreType.DMA((2,))],
        ),
        out_shape=jax.ShapeDtypeStruct((t, d), F32),
        compiler_params=pltpu.CompilerParams(
            dimension_semantics=("arbitrary",), vmem_limit_bytes=VMEM_LIMIT),
        name="combine",
    )(n8, loff, gdst, x, p, info, ys, *consts)


def _moe_layer(x, p, layer, info, cnt, n_exp, we_gate, we_up, we_down, wpg, bpg, wpp, g2, b2,
               *, alpha, tm, tme, fc):
    t, d = x.shape
    n_tok_tiles = t // tm
    assert tm <= ROW_CHUNKS[0] and tme <= ROW_CHUNKS[0]

    i32 = jnp.int32
    n = cnt[:, 0].reshape(n_tok_tiles, SUBLANES)[:, :n_exp].astype(i32)
    n8 = (n + SUBLANES - 1) // SUBLANES * SUBLANES
    loff = jnp.cumsum(n8, axis=1) - n8
    tot = jnp.sum(n8, axis=0)
    reg = (tot + tme - 1) // tme * tme
    reg_end = jnp.cumsum(reg)
    base = reg_end - reg
    gdst = base[None, :] + jnp.cumsum(n8, axis=0) - n8
    n_tiles = (t * TOP_K + n_tok_tiles * n_exp * (SUBLANES - 1)) // tme + n_exp + 1
    n_rows = n_tiles * tme
    tile_start = jnp.arange(n_tiles, dtype=i32) * tme
    tile_e = jnp.minimum(jnp.sum(tile_start[:, None] >= reg_end[None, :], axis=1), n_exp - 1).astype(i32)
    mine = tile_e[:, None] == jnp.arange(n_exp, dtype=i32)[None, :]
    left = jnp.sum(jnp.where(mine, (base + tot)[None, :], 0), axis=1) - tile_start
    tile_n = jnp.clip(left, 0, tme).astype(i32)
    gap = jnp.stack([base + tot, reg - tot], axis=1).reshape(-1)
    gap = jnp.concatenate([gap, jnp.stack([reg_end[-1], (n_rows - reg_end[-1]) // tme])]).astype(i32)
    n8f, lofff, gdstf = (a.reshape(-1).astype(i32) for a in (n8, loff, gdst))

    xs = _dispatch_call(n8f, lofff, gdstf, gap, x, info, tm=tm, tme=tme, n_exp=n_exp, n_rows=n_rows)
    ys = _expert_call(tile_e, tile_n, xs, we_gate, we_up, we_down, tme=tme, fc=fc)
    return _combine_call(n8f, lofff, gdstf, x, p, layer, info, ys, wpg, bpg, wpp, g2, b2,
                         alpha=alpha, tm=tm, n_exp=n_exp)


def _row(v):
    return v.reshape(1, -1)


def kernel(x, p, w_in, b_in, conf_conv_w, conf_conv_b, conf_ln_g, conf_ln_b, w_conf_out,
           sc_conv_w, w_sc_out, w_o, ln1_g, ln1_b, w_ff_gate, w_ff_up, w_ff_down, w_router,
           b_router, we_gate, we_up, we_down, w_ple_gate, b_ple_gate, w_ple_proj, ln2_g, ln2_b,
           ):
    ts, tm, tme, fc_moe = TILES
    depth = w_in.shape[0]
    alpha = (2 * depth) ** 0.25
    bsz, seq, d = x.shape
    pt = p.reshape(depth, bsz * seq, -1)

    expert_w = (("gate", we_gate), ("up", we_up), ("down", we_down))
    queue = [((i // 2, name), w[i // 2], i) for i in range(depth) if i % 2 == 1 for name, w in expert_w]
    cast = {}
    idle = jnp.zeros((bsz * seq // min(ts, tm) * 2 * SUBLANES, LANES), F32)

    def side_job(layer, is_mixer):
        for k, (key, w, needed_in) in enumerate(queue):
            if layer < needed_in or (layer == needed_in and is_mixer):
                del queue[k]
                return key, w
        return None, idle

    def run(call, *args, layer, is_mixer, **kw):
        key, w = side_job(layer, is_mixer)
        out, w_bf16, *rest = call(*args, w.reshape(-1, w.shape[-1]), **kw)
        if key is not None:
            cast[key] = w_bf16.reshape(w.shape)
        return (out, *rest) if rest else out

    assert ts == tm
    for i in range(depth):
        moe = i % 2 == 1
        x, info, cnt = run(
            _mixer_call,
            x, w_in[i].astype(BF16), _row(b_in[i]), conf_conv_w[i], _row(conf_conv_b[i]),
            _row(conf_ln_g[i]), _row(conf_ln_b[i]), w_conf_out[i].astype(BF16), sc_conv_w[i],
            w_sc_out[i].astype(BF16), w_o[i].astype(BF16), _row(ln1_g[i]), _row(ln1_b[i]),
            w_router[i // 2] if moe else None, b_router[i // 2] if moe else None,
            layer=i, is_mixer=True, alpha=alpha, ts=ts)
        xt = x.reshape(bsz * seq, d)
        ple_w = (w_ple_gate[i].astype(BF16), _row(b_ple_gate[i]), w_ple_proj[i].astype(BF16),
                 _row(ln2_g[i]), _row(ln2_b[i]))
        j = i // 2
        if i % 2 == 0:
            xt = run(_dense_ffn_call,
                     xt, pt, i, w_ff_gate[j].astype(BF16), w_ff_up[j].astype(BF16),
                     w_ff_down[j].astype(BF16), *ple_w, layer=i, is_mixer=False, alpha=alpha, tm=tm)
        else:
            queue[:] = [q for q in queue if q[2] != i]
            ew = [cast.pop((j, name)) if (j, name) in cast else w[j].astype(BF16)
                  for name, w in expert_w]
            xt = _moe_layer(xt, pt, i, info, cnt, w_router.shape[2], *ew,
                            *ple_w, alpha=alpha, tm=tm, tme=tme, fc=fc_moe)
        x = xt.reshape(bsz, seq, d)
    return x
```

```python
import functools
from typing import NamedTuple

import jax
import jax.numpy as jnp
from jax import lax
from jax.experimental import pallas as pl
from jax.experimental.pallas import tpu as pltpu

LN_EPS = 1e-5
TOP_K = 2
LANES = 128
SUBLANES = 8
CONV_HALO = 32
SC_HALO = 8
MXU_COLS = 256
CONV_ROWS = 64
NEG_BIG = -1e30
VMEM_LIMIT = 56 * 1024 * 1024

F32 = jnp.float32
BF16 = jnp.bfloat16


class Tiles(NamedTuple):
    seq_rows: int = 512
    token_rows: int = 512
    expert_rows: int = 512
    expert_cols: int = 512


TILES = Tiles()


def _dot(a, b):
    return jnp.dot(a, b, preferred_element_type=F32)


def _sigmoid(x):
    return 1.0 / (1.0 + jnp.exp(-x))


def _layer_norm(x, g, b):
    mu = jnp.mean(x, axis=-1, keepdims=True)
    xc = x - mu
    var = jnp.mean(xc * xc, axis=-1, keepdims=True)
    return xc * lax.rsqrt(var + LN_EPS) * g + b


def _const_spec(shape):
    nd = len(shape)
    return pl.BlockSpec(shape, lambda *_: (0,) * nd, pipeline_mode=pl.Buffered(1))


def _route(x_tile, wr_ref, br_ref, upper_ref, n_exp):
    logits = _dot(x_tile.astype(BF16), wr_ref[...]) + br_ref[...]
    lt = logits.T
    tm = lt.shape[1]
    row = [lt[e:e + 1, :] for e in range(n_exp)]
    zero = jnp.zeros((1, tm), F32)
    m1, i1, m2, i2 = row[0], zero, jnp.full((1, tm), 2.0 * NEG_BIG, F32), zero
    for e in range(1, n_exp):
        first = row[e] > m1
        second = row[e] > m2
        m2 = jnp.where(first, m1, jnp.where(second, row[e], m2))
        i2 = jnp.where(first, i1, jnp.where(second, float(e), i2))
        m1 = jnp.where(first, row[e], m1)
        i1 = jnp.where(first, float(e), i1)
    e21 = jnp.exp(m2 - m1)
    w1 = 1.0 / (1.0 + e21)
    w2 = e21 / (1.0 + e21)

    expert = lax.broadcasted_iota(jnp.int32, (SUBLANES, tm), 0).astype(F32)
    hit1 = expert == i1
    hit2 = expert == i2
    onehot = jnp.where(hit1 | hit2, 1.0, 0.0)
    ranks = _dot(onehot.astype(BF16), upper_ref[...])
    r1 = jnp.sum(jnp.where(hit1, ranks, 0.0), axis=0, keepdims=True)
    r2 = jnp.sum(jnp.where(hit2, ranks, 0.0), axis=0, keepdims=True)
    info = jnp.concatenate([i1, i2, r1, r2, w1, w2, zero, zero], axis=0)
    return info, jnp.sum(onehot, axis=1, keepdims=True)


def _side_cast_specs(side, n_steps, step_index):
    rows, cols = side.shape
    assert rows % (n_steps * 2 * SUBLANES) == 0
    block = (rows // n_steps, cols)
    spec = pl.BlockSpec(block, lambda *idx: (step_index(*idx), 0))
    return spec, spec, jax.ShapeDtypeStruct(side.shape, BF16)


def _causal_dwconv_chunk(src_ref, row0, halo, w_ref, taps, lanes, init):
    acc = init
    first = halo - (taps - 1)
    for r in range(SUBLANES):
        offs = [o for o in range(first, halo + 1) if o % SUBLANES == r]
        if not offs:
            continue
        base = offs[0]
        span = offs[-1] - base + CONV_ROWS
        blk = src_ref[row0 + base:row0 + base + span, lanes]
        part = None
        for o in offs:
            j = o - first
            term = w_ref[j:j + 1, lanes] * blk[o - base:o - base + CONV_ROWS, :]
            part = term if part is None else part + term
        acc = acc + part
    return acc


def _mixer_kernel(x_ref, w_in_ref, b_in_ref, ccw_ref, ccb_ref, clg_ref, clb_ref, wa_ref,
                  scw_ref, wb_ref, wo_ref, g1_ref, b1_ref, wr_ref, br_ref, upper_ref, side_ref,
                  o_ref, side_out_ref, info_ref, cnt_ref,
                  a_ext, u_ext, scb_buf, act_a, act_b, zg_buf, *, alpha, ts, dc, ds, kc, ks, n_exp):
    s = pl.program_id(1)
    side_out_ref[...] = side_ref[...].astype(BF16)

    @pl.when(s == 0)
    def _():
        a_ext[0:CONV_HALO, :] = jnp.zeros((CONV_HALO, dc), F32)
        u_ext[0:SC_HALO, :] = jnp.zeros((SC_HALO, ds), F32)

    @pl.when(s > 0)
    def _():
        a_ext[0:CONV_HALO, :] = a_ext[ts:ts + CONV_HALO, :]
        u_ext[0:SC_HALO, :] = u_ext[ts:ts + SC_HALO, :]

    x = x_ref[0]
    xb = x.astype(BF16)

    c0 = 2 * dc
    c1 = c0 + 3 * ds
    zc = _dot(xb, w_in_ref[:, 0:c0]) + b_in_ref[:, 0:c0]
    a_ext[CONV_HALO:CONV_HALO + ts, :] = zc[:, :dc] * _sigmoid(zc[:, dc:])
    zs = _dot(xb, w_in_ref[:, c0:c1]) + b_in_ref[:, c0:c1]
    scb_buf[...] = zs[:, :ds]
    u_ext[SC_HALO:SC_HALO + ts, :] = zs[:, ds:2 * ds] * zs[:, 2 * ds:]

    d = x.shape[-1]
    n_piece = 2 * d // MXU_COLS
    zg_pieces = []

    def gate_piece(j):
        cols = slice(c1 + j * MXU_COLS, c1 + (j + 1) * MXU_COLS)
        z = _dot(xb, w_in_ref[:, cols]) + b_in_ref[:, cols]
        zg_buf[:, j * MXU_COLS:(j + 1) * MXU_COLS] = z
        zg_pieces.append(z[0:SUBLANES, 0:LANES])

    def after(piece):
        bits = pltpu.bitcast(piece, jnp.uint32)
        return pltpu.bitcast((bits >> 16) >> 16, F32)

    half = ts // 2
    chunks = ts // CONV_ROWS
    groups = dc // LANES
    n_units = (chunks // 2) * groups
    late = []
    outs = []

    def out_proj(rows):
        zg = zg_buf[rows, :]
        y_a = _dot(act_a[rows, :], wa_ref[...])
        y_b = _dot(act_b[rows, :], wb_ref[...])
        late.append(y_a[0:SUBLANES, 0:LANES])
        m = _sigmoid(zg[:, :d]) * y_a + _sigmoid(zg[:, d:]) * y_b
        mix = _dot(m.astype(BF16), wo_ref[...])
        late.append(mix[0:SUBLANES, 0:LANES])
        outs.append(_layer_norm(alpha * x[rows, :] + mix, g1_ref[...], b1_ref[...]))

    for ci in range(chunks):
        row0 = ci * CONV_ROWS
        first_half = ci < chunks // 2
        if ci == chunks // 2:
            out_proj(slice(0, half))
        conv = []
        for cg in range(groups):
            lanes = slice(cg * LANES, (cg + 1) * LANES)
            unit = (ci % (chunks // 2)) * groups + cg
            if first_half:
                want = min(n_piece, -(-(unit + 1) * n_piece // n_units))
                while len(zg_pieces) < want:
                    gate_piece(len(zg_pieces))
                dep = zg_pieces[-1]
            else:
                dep = late[unit * len(late) // n_units]
            init = jnp.broadcast_to(ccb_ref[:, lanes], (CONV_ROWS, LANES))
            init = init + jnp.tile(after(dep), (CONV_ROWS // SUBLANES, 1))
            conv.append(_causal_dwconv_chunk(a_ext, row0, CONV_HALO, ccw_ref, kc, lanes, init))
        conv = jnp.concatenate(conv, axis=-1)
        ln = _layer_norm(conv, clg_ref[...], clb_ref[...])
        act_a[row0:row0 + CONV_ROWS, :] = (ln * _sigmoid(ln)).astype(BF16)
        sconv = []
        for cg in range(ds // LANES):
            lanes = slice(cg * LANES, (cg + 1) * LANES)
            init = jnp.zeros((CONV_ROWS, LANES), F32)
            sconv.append(_causal_dwconv_chunk(u_ext, row0, SC_HALO, scw_ref, ks, lanes, init))
        sconv = jnp.concatenate(sconv, axis=-1)
        act_b[row0:row0 + CONV_ROWS, :] = (scb_buf[row0:row0 + CONV_ROWS, :] * sconv).astype(BF16)
    out_proj(slice(half, ts))

    out = jnp.concatenate(outs, axis=0)
    o_ref[0] = out
    if n_exp:
        info, counts = _route(out, wr_ref, br_ref, upper_ref, n_exp)
        info_ref[...] = info
        cnt_ref[...] = jnp.broadcast_to(counts, cnt_ref.shape)
    else:
        info_ref[...] = jnp.zeros(info_ref.shape, F32)
        cnt_ref[...] = jnp.zeros(cnt_ref.shape, F32)


def _mixer_call(x, w_in, b_in, ccw, ccb, clg, clb, wa, scw, wb, wo, g1, b1, w_router, b_router,
                side, *, alpha, ts):
    bsz, seq, d = x.shape
    kc, dc = ccw.shape
    ks, ds = scw.shape
    assert seq % ts == 0 and ts % (2 * CONV_ROWS) == 0 and ts >= CONV_HALO
    assert kc - 1 <= CONV_HALO and ks - 1 <= SC_HALO
    assert dc % LANES == 0 and ds % LANES == 0 and (2 * d) % MXU_COLS == 0
    n_exp = 0 if w_router is None else w_router.shape[1]
    assert n_exp <= SUBLANES and ts % LANES == 0 and ts < 2 ** 24
    wr = jnp.zeros((d, LANES), BF16)
    br = jnp.full((1, LANES), NEG_BIG, F32)
    if n_exp:
        wr = wr.at[:, :n_exp].set(w_router.astype(BF16))
        br = br.at[0, :n_exp].set(b_router)
    upper = jnp.triu(jnp.ones((ts, ts), BF16), 1)
    kern = functools.partial(_mixer_kernel, alpha=alpha, ts=ts, dc=dc, ds=ds, kc=kc, ks=ks, n_exp=n_exp)
    consts = (w_in, b_in, ccw, ccb, clg, clb, wa, scw, wb, wo, g1, b1, wr, br, upper)
    tiles = seq // ts
    side_in, side_out, side_shape = _side_cast_specs(side, bsz * tiles, lambda b, s: b * tiles + s)
    return pl.pallas_call(
        kern,
        grid=(bsz, tiles),
        in_specs=[pl.BlockSpec((1, ts, d), lambda b, s: (b, s, 0))]
        + [_const_spec(c.shape) for c in consts] + [side_in],
        out_specs=[pl.BlockSpec((1, ts, d), lambda b, s: (b, s, 0)), side_out,
                   pl.BlockSpec((SUBLANES, ts), lambda b, s: (0, b * tiles + s)),
                   pl.BlockSpec((SUBLANES, LANES), lambda b, s: (b * tiles + s, 0))],
        out_shape=[jax.ShapeDtypeStruct((bsz, seq, d), F32), side_shape,
                   jax.ShapeDtypeStruct((SUBLANES, bsz * seq), F32),
                   jax.ShapeDtypeStruct((bsz * tiles * SUBLANES, LANES), F32)],
        scratch_shapes=[
            pltpu.VMEM((ts + CONV_HALO, dc), F32),
            pltpu.VMEM((ts + SC_HALO, ds), F32),
            pltpu.VMEM((ts, ds), F32),
            pltpu.VMEM((ts, dc), BF16),
            pltpu.VMEM((ts, ds), BF16),
            pltpu.VMEM((ts, 2 * d), F32),
        ],
        compiler_params=pltpu.CompilerParams(
            dimension_semantics=("arbitrary", "arbitrary"), vmem_limit_bytes=VMEM_LIMIT),
        name="mixer",
    )(x, *consts, side)


def _ple(xb, p, wpg_ref, bpg_ref, wpp_ref):
    gate = _sigmoid(_dot(xb, wpg_ref[...]) + bpg_ref[...])
    return gate * _dot(p.astype(BF16), wpp_ref[...])


def _dense_ffn_kernel(x_ref, p_ref, wg_ref, wu_ref, wd_ref, wpg_ref, bpg_ref, wpp_ref,
                      g2_ref, b2_ref, side_ref, o_ref, side_out_ref, *, alpha):
    side_out_ref[...] = side_ref[...].astype(BF16)
    x = x_ref[...]
    xb = x.astype(BF16)
    acc = alpha * x + _ple(xb, p_ref[...], wpg_ref, bpg_ref, wpp_ref)
    g = _dot(xb, wg_ref[...])
    u = _dot(xb, wu_ref[...])
    h = (g * _sigmoid(g) * u).astype(BF16)
    acc = acc + _dot(h, wd_ref[...])
    o_ref[...] = _layer_norm(acc, g2_ref[...], b2_ref[...])


def _dense_ffn_call(x, p, layer, wg, wu, wd, wpg, bpg, wpp, g2, b2, side, *, alpha, tm):
    t, d = x.shape
    assert t % tm == 0
    consts = (wg, wu, wd, wpg, bpg, wpp, g2, b2)
    side_in, side_out, side_shape = _side_cast_specs(side, t // tm, lambda i: i)
    return pl.pallas_call(
        functools.partial(_dense_ffn_kernel, alpha=alpha),
        grid=(t // tm,),
        in_specs=[pl.BlockSpec((tm, d), lambda i: (i, 0)),
                  pl.BlockSpec((None, tm, p.shape[2]), lambda i: (layer, i, 0))]
        + [_const_spec(c.shape) for c in consts] + [side_in],
        out_specs=[pl.BlockSpec((tm, d), lambda i: (i, 0)), side_out],
        out_shape=[jax.ShapeDtypeStruct((t, d), F32), side_shape],
        compiler_params=pltpu.CompilerParams(
            dimension_semantics=("arbitrary",), vmem_limit_bytes=VMEM_LIMIT),
        name="dense_ffn",
    )(x, p, *consts, side)


ROW_CHUNKS = (512, 256, 128, 64, 32, 16, 8)


def _for_row_chunks(n, fn):
    off = 0
    for rows in ROW_CHUNKS:
        take = (n & rows) != 0
        pl.when(take)(functools.partial(fn, off, rows))
        off = off + jnp.where(take, rows, 0)


TOTAL_CHUNKS = (1024,) + ROW_CHUNKS


def _wait_rows(n, wait_fn):
    for rows in TOTAL_CHUNKS:
        pl.when((n & rows) != 0)(functools.partial(wait_fn, rows))


def _local_slot(expert, rank, loff_ref, base, n_exp):
    off = jnp.zeros(expert.shape, F32)
    for e in range(n_exp):
        off = jnp.where(expert == float(e), loff_ref[base + e].astype(F32), off)
    return (off + rank).astype(jnp.int32)


def _dispatch_kernel(n8_ref, loff_ref, gdst_ref, gap_ref, x_ref, info_ref, xs_ref,
                     comp, zbuf, sems, *, tm, n_exp):
    i = pl.program_id(0)
    n_steps = pl.num_programs(0)
    slot = i % 2
    nc = comp.shape[1]

    def group_copy(step, e, s, off, rows):
        src = comp.at[s, pl.ds(pl.multiple_of(loff_ref[step * n_exp + e] + off, SUBLANES), rows)]
        dst = xs_ref.at[pl.ds(pl.multiple_of(gdst_ref[step * n_exp + e] + off, SUBLANES), rows)]
        return pltpu.make_async_copy(src, dst, sems.at[s])

    def start_groups(step, s):
        for e in range(n_exp):
            _for_row_chunks(n8_ref[step * n_exp + e],
                            lambda off, rows, e=e: group_copy(step, e, s, off, rows).start())

    def wait_groups(step, s):
        total = n8_ref[step * n_exp]
        for e in range(1, n_exp):
            total = total + n8_ref[step * n_exp + e]
        _wait_rows(total, lambda rows: pltpu.make_async_copy(
            comp.at[s, pl.ds(0, rows)], xs_ref.at[pl.ds(0, rows)], sems.at[s]).wait())

    info_t = info_ref[...]
    slot_row = lax.broadcasted_iota(jnp.int32, (nc, tm), 0)
    hit = None
    for k in range(TOP_K):
        s_k = _local_slot(info_t[k:k + 1, :], info_t[TOP_K + k:TOP_K + k + 1, :],
                          loff_ref, i * n_exp, n_exp)
        hit = (slot_row == s_k) if hit is None else hit | (slot_row == s_k)
    onehot = jnp.where(hit, 1.0, 0.0).astype(BF16)
    comp[slot] = _dot(onehot, x_ref[...].astype(BF16))

    @pl.when(i > 0)
    def _():
        wait_groups(i - 1, 1 - slot)

    start_groups(i, slot)

    @pl.when(i == n_steps - 1)
    def _():
        wait_groups(i, slot)
        zbuf[...] = jnp.zeros(zbuf.shape, F32)

        def gap_copy(e, off, rows):
            dst = xs_ref.at[pl.ds(pl.multiple_of(gap_ref[2 * e] + off, SUBLANES), rows)]
            return pltpu.make_async_copy(zbuf.at[pl.ds(0, rows)], dst, sems.at[0])

        for e in range(n_exp):
            _for_row_chunks(gap_ref[2 * e + 1],
                            lambda off, rows, e=e: gap_copy(e, off, rows).start())
        for e in range(n_exp):
            _for_row_chunks(gap_ref[2 * e + 1],
                            lambda off, rows, e=e: gap_copy(e, off, rows).wait())

        tail_rows = zbuf.shape[0]

        def tail_copy(j):
            start = pl.multiple_of(gap_ref[2 * n_exp] + j * tail_rows, SUBLANES)
            return pltpu.make_async_copy(zbuf, xs_ref.at[pl.ds(start, tail_rows)], sems.at[0])

        def tail(j, c):
            tail_copy(j).start()
            tail_copy(j).wait()
            return c

        lax.fori_loop(0, gap_ref[2 * n_exp + 1], tail, 0)


def _dispatch_call(n8, loff, gdst, gap, x, info, *, tm, tme, n_exp, n_rows):
    t, d = x.shape
    nc = TOP_K * tm + SUBLANES * n_exp
    return pl.pallas_call(
        functools.partial(_dispatch_kernel, tm=tm, n_exp=n_exp),
        grid_spec=pltpu.PrefetchScalarGridSpec(
            num_scalar_prefetch=4,
            grid=(t // tm,),
            in_specs=[pl.BlockSpec((tm, d), lambda i, *_: (i, 0)),
                      pl.BlockSpec((SUBLANES, tm), lambda i, *_: (0, i))],
            out_specs=pl.BlockSpec(memory_space=pl.ANY),
            scratch_shapes=[pltpu.VMEM((2, nc, d), F32),
                            pltpu.VMEM((tme, d), F32),
                            pltpu.SemaphoreType.DMA((2,))],
        ),
        out_shape=jax.ShapeDtypeStruct((n_rows, d), F32),
        compiler_params=pltpu.CompilerParams(
            dimension_semantics=("arbitrary",), vmem_limit_bytes=VMEM_LIMIT),
        name="dispatch",
    )(n8, loff, gdst, gap, x, info)


def _expert_kernel(tile_e_ref, tile_n_ref, xs_ref, wg_ref, wu_ref, wd_ref, ys_ref, *, fc):
    del tile_e_ref
    i = pl.program_id(0)
    n_valid = tile_n_ref[i]
    tme = ys_ref.shape[0]

    def swiglu(rows):
        xb = xs_ref[0:rows, :].astype(BF16)
        f = wg_ref.shape[2]
        acc = jnp.zeros((rows, ys_ref.shape[1]), F32)
        for c in range(f // fc):
            cols = slice(c * fc, (c + 1) * fc)
            g = _dot(xb, wg_ref[0, :, cols])
            u = _dot(xb, wu_ref[0, :, cols])
            h = (g * _sigmoid(g) * u).astype(BF16)
            acc = acc + _dot(h, wd_ref[0, cols, :])
        ys_ref[0:rows, :] = acc

    step = tme // 8
    for rows in range(step, tme + 1, step):
        @pl.when((n_valid > rows - step) & (n_valid <= rows))
        def _(rows=rows):
            swiglu(rows)
            if rows < tme:
                ys_ref[rows:tme, :] = jnp.zeros((tme - rows, ys_ref.shape[1]), F32)

    @pl.when(n_valid == 0)
    def _():
        ys_ref[...] = jnp.zeros(ys_ref.shape, F32)


def _expert_call(tile_e, tile_n, xs, wg, wu, wd, *, tme, fc):
    n_rows, d = xs.shape
    f = wg.shape[2]
    assert n_rows % tme == 0 and f % fc == 0 and fc % LANES == 0

    def w_spec(shape):
        return pl.BlockSpec((1,) + shape[1:], lambda i, te, tn: (te[i], 0, 0))

    return pl.pallas_call(
        functools.partial(_expert_kernel, fc=fc),
        grid_spec=pltpu.PrefetchScalarGridSpec(
            num_scalar_prefetch=2,
            grid=(n_rows // tme,),
            in_specs=[pl.BlockSpec((tme, d), lambda i, te, tn: (i, 0)),
                      w_spec(wg.shape), w_spec(wu.shape), w_spec(wd.shape)],
            out_specs=pl.BlockSpec((tme, d), lambda i, te, tn: (i, 0)),
        ),
        out_shape=jax.ShapeDtypeStruct((n_rows, d), F32),
        compiler_params=pltpu.CompilerParams(
            dimension_semantics=("arbitrary",), vmem_limit_bytes=VMEM_LIMIT),
        name="experts",
    )(tile_e, tile_n, xs, wg, wu, wd)


def _combine_kernel(n8_ref, loff_ref, gdst_ref, x_ref, p_ref, info_ref, ys_ref, wpg_ref, bpg_ref,
                    wpp_ref, g2_ref, b2_ref, o_ref, ycomp, sems, *, alpha, tm, n_exp):
    i = pl.program_id(0)
    n_steps = pl.num_programs(0)
    slot = i % 2
    nc = ycomp.shape[1]

    def group_copy(step, e, s, off, rows):
        src = ys_ref.at[pl.ds(pl.multiple_of(gdst_ref[step * n_exp + e] + off, SUBLANES), rows)]
        dst = ycomp.at[s, pl.ds(pl.multiple_of(loff_ref[step * n_exp + e] + off, SUBLANES), rows)]
        return pltpu.make_async_copy(src, dst, sems.at[s])

    def start_groups(step, s):
        for e in range(n_exp):
            _for_row_chunks(n8_ref[step * n_exp + e],
                            lambda off, rows, e=e: group_copy(step, e, s, off, rows).start())

    def wait_groups(step, s):
        total = n8_ref[step * n_exp]
        for e in range(1, n_exp):
            total = total + n8_ref[step * n_exp + e]
        _wait_rows(total, lambda rows: pltpu.make_async_copy(
            ys_ref.at[pl.ds(0, rows)], ycomp.at[s, pl.ds(0, rows)], sems.at[s]).wait())

    @pl.when(i == 0)
    def _():
        ycomp[...] = jnp.zeros(ycomp.shape, F32)
        start_groups(0, 0)

    wait_groups(i, slot)

    @pl.when(i + 1 < n_steps)
    def _():
        start_groups(i + 1, 1 - slot)

    x = x_ref[...]
    acc = alpha * x + _ple(x.astype(BF16), p_ref[...], wpg_ref, bpg_ref, wpp_ref)

    info_t = info_ref[...]
    info = jnp.concatenate([info_t, jnp.zeros((LANES - SUBLANES, tm), F32)], axis=0).T
    slot_col = lax.broadcasted_iota(jnp.int32, (tm, nc), 1)
    pick = jnp.zeros((tm, nc), F32)
    for k in range(TOP_K):
        s_k = _local_slot(info[:, k:k + 1], info[:, TOP_K + k:TOP_K + k + 1],
                          loff_ref, i * n_exp, n_exp)
        pick = jnp.where(slot_col == s_k, info[:, 4 + k:5 + k], pick)
    acc = acc + _dot(pick.astype(BF16), ycomp[slot].astype(BF16))
    o_ref[...] = _layer_norm(acc, g2_ref[...], b2_ref[...])


def _combine_call(n8, loff, gdst, x, p, layer, info, ys, wpg, bpg, wpp, g2, b2, *, alpha, tm, n_exp):
    t, d = x.shape
    nc = TOP_K * tm + SUBLANES * n_exp
    consts = (wpg, bpg, wpp, g2, b2)

    def const_spec(shape):
        nd = len(shape)
        return pl.BlockSpec(shape, lambda i, *_: (0,) * nd, pipeline_mode=pl.Buffered(1))

    return pl.pallas_call(
        functools.partial(_combine_kernel, alpha=alpha, tm=tm, n_exp=n_exp),
        grid_spec=pltpu.PrefetchScalarGridSpec(
            num_scalar_prefetch=3,
            grid=(t // tm,),
            in_specs=[pl.BlockSpec((tm, d), lambda i, *_: (i, 0)),
                      pl.BlockSpec((None, tm, p.shape[2]), lambda i, *_: (layer, i, 0)),
                      pl.BlockSpec((SUBLANES, tm), lambda i, *_: (0, i)),
                      pl.BlockSpec(memory_space=pl.ANY)]
            + [const_spec(c.shape) for c in consts],
            out_specs=pl.BlockSpec((tm, d), lambda i, *_: (i, 0)),
            scratch_shapes=[pltpu.VMEM((2, nc, d), F32), pltpu.SemaphoreType.DMA((2,))],
        ),
        out_shape=jax.ShapeDtypeStruct((t, d), F32),
        compiler_params=pltpu.CompilerParams(
            dimension_semantics=("arbitrary",), vmem_limit_bytes=VMEM_LIMIT),
        name="combine",
    )(n8, loff, gdst, x, p, info, ys, *consts)


def _moe_layer(x, p, layer, info, cnt, n_exp, we_gate, we_up, we_down, wpg, bpg, wpp, g2, b2,
               *, alpha, tm, tme, fc):
    t, d = x.shape
    n_tok_tiles = t // tm
    assert tm <= ROW_CHUNKS[0] and tme <= ROW_CHUNKS[0]

    i32 = jnp.int32
    n = cnt[:, 0].reshape(n_tok_tiles, SUBLANES)[:, :n_exp].astype(i32)
    n8 = (n + SUBLANES - 1) // SUBLANES * SUBLANES
    loff = jnp.cumsum(n8, axis=1) - n8
    tot = jnp.sum(n8, axis=0)
    reg = (tot + tme - 1) // tme * tme
    reg_end = jnp.cumsum(reg)
    base = reg_end - reg
    gdst = base[None, :] + jnp.cumsum(n8, axis=0) - n8
    n_tiles = (t * TOP_K + n_tok_tiles * n_exp * (SUBLANES - 1)) // tme + n_exp + 1
    n_rows = n_tiles * tme
    tile_start = jnp.arange(n_tiles, dtype=i32) * tme
    tile_e = jnp.minimum(jnp.sum(tile_start[:, None] >= reg_end[None, :], axis=1), n_exp - 1).astype(i32)
    mine = tile_e[:, None] == jnp.arange(n_exp, dtype=i32)[None, :]
    left = jnp.sum(jnp.where(mine, (base + tot)[None, :], 0), axis=1) - tile_start
    tile_n = jnp.clip(left, 0, tme).astype(i32)
    gap = jnp.stack([base + tot, reg - tot], axis=1).reshape(-1)
    gap = jnp.concatenate([gap, jnp.stack([reg_end[-1], (n_rows - reg_end[-1]) // tme])]).astype(i32)
    n8f, lofff, gdstf = (a.reshape(-1).astype(i32) for a in (n8, loff, gdst))

    xs = _dispatch_call(n8f, lofff, gdstf, gap, x, info, tm=tm, tme=tme, n_exp=n_exp, n_rows=n_rows)
    ys = _expert_call(tile_e, tile_n, xs, we_gate, we_up, we_down, tme=tme, fc=fc)
    return _combine_call(n8f, lofff, gdstf, x, p, layer, info, ys, wpg, bpg, wpp, g2, b2,
                         alpha=alpha, tm=tm, n_exp=n_exp)


def _row(v):
    return v.reshape(1, -1)


def kernel(x, p, w_in, b_in, conf_conv_w, conf_conv_b, conf_ln_g, conf_ln_b, w_conf_out,
           sc_conv_w, w_sc_out, w_o, ln1_g, ln1_b, w_ff_gate, w_ff_up, w_ff_down, w_router,
           b_router, we_gate, we_up, we_down, w_ple_gate, b_ple_gate, w_ple_proj, ln2_g, ln2_b,
           ):
    ts, tm, tme, fc_moe = TILES
    depth = w_in.shape[0]
    alpha = (2 * depth) ** 0.25
    bsz, seq, d = x.shape
    pt = p.reshape(depth, bsz * seq, -1)

    expert_w = (("gate", we_gate), ("up", we_up), ("down", we_down))
    queue = [((i // 2, name), w[i // 2], i) for i in range(depth) if i % 2 == 1 for name, w in expert_w]
    cast = {}
    idle = jnp.zeros((bsz * seq // min(ts, tm) * 2 * SUBLANES, LANES), F32)

    def side_job(layer, is_mixer):
        for k, (key, w, needed_in) in enumerate(queue):
            if layer < needed_in or (layer == needed_in and is_mixer):
                del queue[k]
                return key, w
        return None, idle

    def run(call, *args, layer, is_mixer, **kw):
        key, w = side_job(layer, is_mixer)
        out, w_bf16, *rest = call(*args, w.reshape(-1, w.shape[-1]), **kw)
        if key is not None:
            cast[key] = w_bf16.reshape(w.shape)
        return (out, *rest) if rest else out

    assert ts == tm
    for i in range(depth):
        moe = i % 2 == 1
        x, info, cnt = run(
            _mixer_call,
            x, w_in[i].astype(BF16), _row(b_in[i]), conf_conv_w[i], _row(conf_conv_b[i]),
            _row(conf_ln_g[i]), _row(conf_ln_b[i]), w_conf_out[i].astype(BF16), sc_conv_w[i],
            w_sc_out[i].astype(BF16), w_o[i].astype(BF16), _row(ln1_g[i]), _row(ln1_b[i]),
            w_router[i // 2] if moe else None, b_router[i // 2] if moe else None,
            layer=i, is_mixer=True, alpha=alpha, ts=ts)
        xt = x.reshape(bsz * seq, d)
        ple_w = (w_ple_gate[i].astype(BF16), _row(b_ple_gate[i]), w_ple_proj[i].astype(BF16),
                 _row(ln2_g[i]), _row(ln2_b[i]))
        j = i // 2
        if i % 2 == 0:
            xt = run(_dense_ffn_call,
                     xt, pt, i, w_ff_gate[j].astype(BF16), w_ff_up[j].astype(BF16),
                     w_ff_down[j].astype(BF16), *ple_w, layer=i, is_mixer=False, alpha=alpha, tm=tm)
        else:
            queue[:] = [q for q in queue if q[2] != i]
            ew = [cast.pop((j, name)) if (j, name) in cast else w[j].astype(BF16)
                  for name, w in expert_w]
            xt = _moe_layer(xt, pt, i, info, cnt, w_router.shape[2], *ew,
                            *ple_w, alpha=alpha, tm=tm, tme=tme, fc=fc_moe)
        x = xt.reshape(bsz, seq, d)
    return x
```

```python
import functools
from typing import NamedTuple

import jax
import jax.numpy as jnp
from jax import lax
from jax.experimental import pallas as pl
from jax.experimental.pallas import tpu as pltpu

LN_EPS = 1e-5
TOP_K = 2
LANES = 128
SUBLANES = 8
CONV_HALO = 32
SC_HALO = 8
MXU_COLS = 256
CONV_ROWS = 64
NEG_BIG = -1e30
VMEM_LIMIT = 56 * 1024 * 1024

F32 = jnp.float32
BF16 = jnp.bfloat16


class Tiles(NamedTuple):
    seq_rows: int = 512
    token_rows: int = 512
    expert_rows: int = 512
    expert_cols: int = 512


TILES = Tiles()


def _dot(a, b):
    return jnp.dot(a, b, preferred_element_type=F32)


def _sigmoid(x):
    return 1.0 / (1.0 + jnp.exp(-x))


def _layer_norm(x, g, b):
    mu = jnp.mean(x, axis=-1, keepdims=True)
    xc = x - mu
    var = jnp.mean(xc * xc, axis=-1, keepdims=True)
    return xc * lax.rsqrt(var + LN_EPS) * g + b


def _const_spec(shape):
    nd = len(shape)
    return pl.BlockSpec(shape, lambda *_: (0,) * nd, pipeline_mode=pl.Buffered(1))


def _route(x_tile, wr_ref, br_ref, upper_ref, n_exp):
    logits = _dot(x_tile.astype(BF16), wr_ref[...]) + br_ref[...]
    lt = logits.T
    tm = lt.shape[1]
    row = [lt[e:e + 1, :] for e in range(n_exp)]
    zero = jnp.zeros((1, tm), F32)
    m1, i1, m2, i2 = row[0], zero, jnp.full((1, tm), 2.0 * NEG_BIG, F32), zero
    for e in range(1, n_exp):
        first = row[e] > m1
        second = row[e] > m2
        m2 = jnp.where(first, m1, jnp.where(second, row[e], m2))
        i2 = jnp.where(first, i1, jnp.where(second, float(e), i2))
        m1 = jnp.where(first, row[e], m1)
        i1 = jnp.where(first, float(e), i1)
    e21 = jnp.exp(m2 - m1)
    w1 = 1.0 / (1.0 + e21)
    w2 = e21 / (1.0 + e21)

    expert = lax.broadcasted_iota(jnp.int32, (SUBLANES, tm), 0).astype(F32)
    hit1 = expert == i1
    hit2 = expert == i2
    onehot = jnp.where(hit1 | hit2, 1.0, 0.0)
    ranks = _dot(onehot.astype(BF16), upper_ref[...])
    r1 = jnp.sum(jnp.where(hit1, ranks, 0.0), axis=0, keepdims=True)
    r2 = jnp.sum(jnp.where(hit2, ranks, 0.0), axis=0, keepdims=True)
    info = jnp.concatenate([i1, i2, r1, r2, w1, w2, zero, zero], axis=0)
    return info, jnp.sum(onehot, axis=1, keepdims=True)


def _side_cast_specs(side, n_steps, step_index):
    rows, cols = side.shape
    assert rows % (n_steps * 2 * SUBLANES) == 0
    block = (rows // n_steps, cols)
    spec = pl.BlockSpec(block, lambda *idx: (step_index(*idx), 0))
    return spec, spec, jax.ShapeDtypeStruct(side.shape, BF16)


def _causal_dwconv_chunk(src_ref, row0, halo, w_ref, taps, lanes, init):
    acc = init
    first = halo - (taps - 1)
    for r in range(SUBLANES):
        offs = [o for o in range(first, halo + 1) if o % SUBLANES == r]
        if not offs:
            continue
        base = offs[0]
        span = offs[-1] - base + CONV_ROWS
        blk = src_ref[row0 + base:row0 + base + span, lanes]
        part = None
        for o in offs:
            j = o - first
            term = w_ref[j:j + 1, lanes] * blk[o - base:o - base + CONV_ROWS, :]
            part = term if part is None else part + term
        acc = acc + part
    return acc


def _mixer_kernel(x_ref, w_in_ref, b_in_ref, ccw_ref, ccb_ref, clg_ref, clb_ref, wa_ref,
                  scw_ref, wb_ref, wo_ref, g1_ref, b1_ref, wr_ref, br_ref, upper_ref, side_ref,
                  o_ref, side_out_ref, info_ref, cnt_ref,
                  a_ext, u_ext, scb_buf, act_a, act_b, zg_buf, *, alpha, ts, dc, ds, kc, ks, n_exp):
    s = pl.program_id(1)
    side_out_ref[...] = side_ref[...].astype(BF16)

    @pl.when(s == 0)
    def _():
        a_ext[0:CONV_HALO, :] = jnp.zeros((CONV_HALO, dc), F32)
        u_ext[0:SC_HALO, :] = jnp.zeros((SC_HALO, ds), F32)

    @pl.when(s > 0)
    def _():
        a_ext[0:CONV_HALO, :] = a_ext[ts:ts + CONV_HALO, :]
        u_ext[0:SC_HALO, :] = u_ext[ts:ts + SC_HALO, :]

    x = x_ref[0]
    xb = x.astype(BF16)

    c0 = 2 * dc
    c1 = c0 + 3 * ds
    zc = _dot(xb, w_in_ref[:, 0:c0]) + b_in_ref[:, 0:c0]
    a_ext[CONV_HALO:CONV_HALO + ts, :] = zc[:, :dc] * _sigmoid(zc[:, dc:])
    zs = _dot(xb, w_in_ref[:, c0:c1]) + b_in_ref[:, c0:c1]
    scb_buf[...] = zs[:, :ds]
    u_ext[SC_HALO:SC_HALO + ts, :] = zs[:, ds:2 * ds] * zs[:, 2 * ds:]

    d = x.shape[-1]
    n_piece = 2 * d // MXU_COLS
    zg_pieces = []

    def gate_piece(j):
        cols = slice(c1 + j * MXU_COLS, c1 + (j + 1) * MXU_COLS)
        z = _dot(xb, w_in_ref[:, cols]) + b_in_ref[:, cols]
        zg_buf[:, j * MXU_COLS:(j + 1) * MXU_COLS] = z
        zg_pieces.append(z[0:SUBLANES, 0:LANES])

    def after(piece):
        bits = pltpu.bitcast(piece, jnp.uint32)
        return pltpu.bitcast((bits >> 16) >> 16, F32)

    half = ts // 2
    chunks = ts // CONV_ROWS
    groups = dc // LANES
    n_units = (chunks // 2) * groups
    late = []
    outs = []

    def out_proj(rows):
        zg = zg_buf[rows, :]
        y_a = _dot(act_a[rows, :], wa_ref[...])
        y_b = _dot(act_b[rows, :], wb_ref[...])
        late.append(y_a[0:SUBLANES, 0:LANES])
        m = _sigmoid(zg[:, :d]) * y_a + _sigmoid(zg[:, d:]) * y_b
        mix = _dot(m.astype(BF16), wo_ref[...])
        late.append(mix[0:SUBLANES, 0:LANES])
        outs.append(_layer_norm(alpha * x[rows, :] + mix, g1_ref[...], b1_ref[...]))

    for ci in range(chunks):
        row0 = ci * CONV_ROWS
        first_half = ci < chunks // 2
        if ci == chunks // 2:
            out_proj(slice(0, half))
        conv = []
        for cg in range(groups):
            lanes = slice(cg * LANES, (cg + 1) * LANES)
            unit = (ci % (chunks // 2)) * groups + cg
            if first_half:
                want = min(n_piece, -(-(unit + 1) * n_piece // n_units))
                while len(zg_pieces) < want:
                    gate_piece(len(zg_pieces))
                dep = zg_pieces[-1]
            else:
                dep = late[unit * len(late) // n_units]
            init = jnp.broadcast_to(ccb_ref[:, lanes], (CONV_ROWS, LANES))
            init = init + jnp.tile(after(dep), (CONV_ROWS // SUBLANES, 1))
            conv.append(_causal_dwconv_chunk(a_ext, row0, CONV_HALO, ccw_ref, kc, lanes, init))
        conv = jnp.concatenate(conv, axis=-1)
        ln = _layer_norm(conv, clg_ref[...], clb_ref[...])
        act_a[row0:row0 + CONV_ROWS, :] = (ln * _sigmoid(ln)).astype(BF16)
        sconv = []
        for cg in range(ds // LANES):
            lanes = slice(cg * LANES, (cg + 1) * LANES)
            init = jnp.zeros((CONV_ROWS, LANES), F32)
            sconv.append(_causal_dwconv_chunk(u_ext, row0, SC_HALO, scw_ref, ks, lanes, init))
        sconv = jnp.concatenate(sconv, axis=-1)
        act_b[row0:row0 + CONV_ROWS, :] = (scb_buf[row0:row0 + CONV_ROWS, :] * sconv).astype(BF16)
    out_proj(slice(half, ts))

    out = jnp.concatenate(outs, axis=0)
    o_ref[0] = out
    if n_exp:
        info, counts = _route(out, wr_ref, br_ref, upper_ref, n_exp)
        info_ref[...] = info
        cnt_ref[...] = jnp.broadcast_to(counts, cnt_ref.shape)
    else:
        info_ref[...] = jnp.zeros(info_ref.shape, F32)
        cnt_ref[...] = jnp.zeros(cnt_ref.shape, F32)


def _mixer_call(x, w_in, b_in, ccw, ccb, clg, clb, wa, scw, wb, wo, g1, b1, w_router, b_router,
                side, *, alpha, ts):
    bsz, seq, d = x.shape
    kc, dc = ccw.shape
    ks, ds = scw.shape
    assert seq % ts == 0 and ts % (2 * CONV_ROWS) == 0 and ts >= CONV_HALO
    assert kc - 1 <= CONV_HALO and ks - 1 <= SC_HALO
    assert dc % LANES == 0 and ds % LANES == 0 and (2 * d) % MXU_COLS == 0
    n_exp = 0 if w_router is None else w_router.shape[1]
    assert n_exp <= SUBLANES and ts % LANES == 0 and ts < 2 ** 24
    wr = jnp.zeros((d, LANES), BF16)
    br = jnp.full((1, LANES), NEG_BIG, F32)
    if n_exp:
        wr = wr.at[:, :n_exp].set(w_router.astype(BF16))
        br = br.at[0, :n_exp].set(b_router)
    upper = jnp.triu(jnp.ones((ts, ts), BF16), 1)
    kern = functools.partial(_mixer_kernel, alpha=alpha, ts=ts, dc=dc, ds=ds, kc=kc, ks=ks, n_exp=n_exp)
    consts = (w_in, b_in, ccw, ccb, clg, clb, wa, scw, wb, wo, g1, b1, wr, br, upper)
    tiles = seq // ts
    side_in, side_out, side_shape = _side_cast_specs(side, bsz * tiles, lambda b, s: b * tiles + s)
    return pl.pallas_call(
        kern,
        grid=(bsz, tiles),
        in_specs=[pl.BlockSpec((1, ts, d), lambda b, s: (b, s, 0))]
        + [_const_spec(c.shape) for c in consts] + [side_in],
        out_specs=[pl.BlockSpec((1, ts, d), lambda b, s: (b, s, 0)), side_out,
                   pl.BlockSpec((SUBLANES, ts), lambda b, s: (0, b * tiles + s)),
                   pl.BlockSpec((SUBLANES, LANES), lambda b, s: (b * tiles + s, 0))],
        out_shape=[jax.ShapeDtypeStruct((bsz, seq, d), F32), side_shape,
                   jax.ShapeDtypeStruct((SUBLANES, bsz * seq), F32),
                   jax.ShapeDtypeStruct((bsz * tiles * SUBLANES, LANES), F32)],
        scratch_shapes=[
            pltpu.VMEM((ts + CONV_HALO, dc), F32),
            pltpu.VMEM((ts + SC_HALO, ds), F32),
            pltpu.VMEM((ts, ds), F32),
            pltpu.VMEM((ts, dc), BF16),
            pltpu.VMEM((ts, ds), BF16),
            pltpu.VMEM((ts, 2 * d), F32),
        ],
        compiler_params=pltpu.CompilerParams(
            dimension_semantics=("arbitrary", "arbitrary"), vmem_limit_bytes=VMEM_LIMIT),
        name="mixer",
    )(x, *consts, side)


def _ple(xb, p, wpg_ref, bpg_ref, wpp_ref):
    gate = _sigmoid(_dot(xb, wpg_ref[...]) + bpg_ref[...])
    return gate * _dot(p.astype(BF16), wpp_ref[...])


def _dense_ffn_kernel(x_ref, p_ref, wg_ref, wu_ref, wd_ref, wpg_ref, bpg_ref, wpp_ref,
                      g2_ref, b2_ref, side_ref, o_ref, side_out_ref, *, alpha):
    side_out_ref[...] = side_ref[...].astype(BF16)
    x = x_ref[...]
    xb = x.astype(BF16)
    acc = alpha * x + _ple(xb, p_ref[...], wpg_ref, bpg_ref, wpp_ref)
    g = _dot(xb, wg_ref[...])
    u = _dot(xb, wu_ref[...])
    h = (g * _sigmoid(g) * u).astype(BF16)
    acc = acc + _dot(h, wd_ref[...])
    o_ref[...] = _layer_norm(acc, g2_ref[...], b2_ref[...])


def _dense_ffn_call(x, p, layer, wg, wu, wd, wpg, bpg, wpp, g2, b2, side, *, alpha, tm):
    t, d = x.shape
    assert t % tm == 0
    consts = (wg, wu, wd, wpg, bpg, wpp, g2, b2)
    side_in, side_out, side_shape = _side_cast_specs(side, t // tm, lambda i: i)
    return pl.pallas_call(
        functools.partial(_dense_ffn_kernel, alpha=alpha),
        grid=(t // tm,),
        in_specs=[pl.BlockSpec((tm, d), lambda i: (i, 0)),
                  pl.BlockSpec((None, tm, p.shape[2]), lambda i: (layer, i, 0))]
        + [_const_spec(c.shape) for c in consts] + [side_in],
        out_specs=[pl.BlockSpec((tm, d), lambda i: (i, 0)), side_out],
        out_shape=[jax.ShapeDtypeStruct((t, d), F32), side_shape],
        compiler_params=pltpu.CompilerParams(
            dimension_semantics=("arbitrary",), vmem_limit_bytes=VMEM_LIMIT),
        name="dense_ffn",
    )(x, p, *consts, side)


ROW_CHUNKS = (512, 256, 128, 64, 32, 16, 8)


def _for_row_chunks(n, fn):
    off = 0
    for rows in ROW_CHUNKS:
        take = (n & rows) != 0
        pl.when(take)(functools.partial(fn, off, rows))
        off = off + jnp.where(take, rows, 0)


TOTAL_CHUNKS = (1024,) + ROW_CHUNKS


def _wait_rows(n, wait_fn):
    for rows in TOTAL_CHUNKS:
        pl.when((n & rows) != 0)(functools.partial(wait_fn, rows))


def _local_slot(expert, rank, loff_ref, base, n_exp):
    off = jnp.zeros(expert.shape, F32)
    for e in range(n_exp):
        off = jnp.where(expert == float(e), loff_ref[base + e].astype(F32), off)
    return (off + rank).astype(jnp.int32)


def _dispatch_kernel(n8_ref, loff_ref, gdst_ref, gap_ref, x_ref, info_ref, xs_ref,
                     comp, zbuf, sems, *, tm, n_exp):
    i = pl.program_id(0)
    n_steps = pl.num_programs(0)
    slot = i % 2
    nc = comp.shape[1]

    def group_copy(step, e, s, off, rows):
        src = comp.at[s, pl.ds(pl.multiple_of(loff_ref[step * n_exp + e] + off, SUBLANES), rows)]
        dst = xs_ref.at[pl.ds(pl.multiple_of(gdst_ref[step * n_exp + e] + off, SUBLANES), rows)]
        return pltpu.make_async_copy(src, dst, sems.at[s])

    def start_groups(step, s):
        for e in range(n_exp):
            _for_row_chunks(n8_ref[step * n_exp + e],
                            lambda off, rows, e=e: group_copy(step, e, s, off, rows).start(priority=e % 2))

    def wait_groups(step, s):
        total = n8_ref[step * n_exp]
        for e in range(1, n_exp):
            total = total + n8_ref[step * n_exp + e]
        _wait_rows(total, lambda rows: pltpu.make_async_copy(
            comp.at[s, pl.ds(0, rows)], xs_ref.at[pl.ds(0, rows)], sems.at[s]).wait())

    info_t = info_ref[...]
    slot_row = lax.broadcasted_iota(jnp.int32, (nc, tm), 0)
    hit = None
    for k in range(TOP_K):
        s_k = _local_slot(info_t[k:k + 1, :], info_t[TOP_K + k:TOP_K + k + 1, :],
                          loff_ref, i * n_exp, n_exp)
        hit = (slot_row == s_k) if hit is None else hit | (slot_row == s_k)
    onehot = jnp.where(hit, 1.0, 0.0).astype(BF16)
    comp[slot] = _dot(onehot, x_ref[...].astype(BF16))

    @pl.when(i > 0)
    def _():
        wait_groups(i - 1, 1 - slot)

    start_groups(i, slot)

    @pl.when(i == n_steps - 1)
    def _():
        wait_groups(i, slot)
        zbuf[...] = jnp.zeros(zbuf.shape, F32)

        def gap_copy(e, off, rows):
            dst = xs_ref.at[pl.ds(pl.multiple_of(gap_ref[2 * e] + off, SUBLANES), rows)]
            return pltpu.make_async_copy(zbuf.at[pl.ds(0, rows)], dst, sems.at[0])

        for e in range(n_exp):
            _for_row_chunks(gap_ref[2 * e + 1],
                            lambda off, rows, e=e: gap_copy(e, off, rows).start())
        for e in range(n_exp):
            _for_row_chunks(gap_ref[2 * e + 1],
                            lambda off, rows, e=e: gap_copy(e, off, rows).wait())

        tail_rows = zbuf.shape[0]

        def tail_copy(j):
            start = pl.multiple_of(gap_ref[2 * n_exp] + j * tail_rows, SUBLANES)
            return pltpu.make_async_copy(zbuf, xs_ref.at[pl.ds(start, tail_rows)], sems.at[0])

        def tail(j, c):
            tail_copy(j).start()
            tail_copy(j).wait()
            return c

        lax.fori_loop(0, gap_ref[2 * n_exp + 1], tail, 0)


def _dispatch_call(n8, loff, gdst, gap, x, info, *, tm, tme, n_exp, n_rows):
    t, d = x.shape
    nc = TOP_K * tm + SUBLANES * n_exp
    return pl.pallas_call(
        functools.partial(_dispatch_kernel, tm=tm, n_exp=n_exp),
        grid_spec=pltpu.PrefetchScalarGridSpec(
            num_scalar_prefetch=4,
            grid=(t // tm,),
            in_specs=[pl.BlockSpec((tm, d), lambda i, *_: (i, 0)),
                      pl.BlockSpec((SUBLANES, tm), lambda i, *_: (0, i))],
            out_specs=pl.BlockSpec(memory_space=pl.ANY),
            scratch_shapes=[pltpu.VMEM((2, nc, d), F32),
                            pltpu.VMEM((tme, d), F32),
                            pltpu.SemaphoreType.DMA((2,))],
        ),
        out_shape=jax.ShapeDtypeStruct((n_rows, d), F32),
        compiler_params=pltpu.CompilerParams(
            dimension_semantics=("arbitrary",), vmem_limit_bytes=VMEM_LIMIT),
        name="dispatch",
    )(n8, loff, gdst, gap, x, info)


def _expert_kernel(tile_e_ref, tile_n_ref, xs_ref, wg_ref, wu_ref, wd_ref, ys_ref, *, fc):
    del tile_e_ref
    i = pl.program_id(0)
    n_valid = tile_n_ref[i]
    tme = ys_ref.shape[0]

    def swiglu(rows):
        xb = xs_ref[0:rows, :].astype(BF16)
        f = wg_ref.shape[2]
        acc = jnp.zeros((rows, ys_ref.shape[1]), F32)
        for c in range(f // fc):
            cols = slice(c * fc, (c + 1) * fc)
            g = _dot(xb, wg_ref[0, :, cols])
            u = _dot(xb, wu_ref[0, :, cols])
            h = (g * _sigmoid(g) * u).astype(BF16)
            acc = acc + _dot(h, wd_ref[0, cols, :])
        ys_ref[0:rows, :] = acc

    quarter = tme // 4
    for rows in range(quarter, tme + 1, quarter):
        @pl.when((n_valid > rows - quarter) & (n_valid <= rows))
        def _(rows=rows):
            swiglu(rows)
            if rows < tme:
                ys_ref[rows:tme, :] = jnp.zeros((tme - rows, ys_ref.shape[1]), F32)

    @pl.when(n_valid == 0)
    def _():
        ys_ref[...] = jnp.zeros(ys_ref.shape, F32)


def _expert_call(tile_e, tile_n, xs, wg, wu, wd, *, tme, fc):
    n_rows, d = xs.shape
    f = wg.shape[2]
    assert n_rows % tme == 0 and f % fc == 0 and fc % LANES == 0

    def w_spec(shape):
        return pl.BlockSpec((1,) + shape[1:], lambda i, te, tn: (te[i], 0, 0))

    return pl.pallas_call(
        functools.partial(_expert_kernel, fc=fc),
        grid_spec=pltpu.PrefetchScalarGridSpec(
            num_scalar_prefetch=2,
            grid=(n_rows // tme,),
            in_specs=[pl.BlockSpec((tme, d), lambda i, te, tn: (i, 0)),
                      w_spec(wg.shape), w_spec(wu.shape), w_spec(wd.shape)],
            out_specs=pl.BlockSpec((tme, d), lambda i, te, tn: (i, 0)),
        ),
        out_shape=jax.ShapeDtypeStruct((n_rows, d), F32),
        compiler_params=pltpu.CompilerParams(
            dimension_semantics=("arbitrary",), vmem_limit_bytes=VMEM_LIMIT),
        name="experts",
    )(tile_e, tile_n, xs, wg, wu, wd)


def _combine_kernel(n8_ref, loff_ref, gdst_ref, x_ref, p_ref, info_ref, ys_ref, wpg_ref, bpg_ref,
                    wpp_ref, g2_ref, b2_ref, o_ref, ycomp, sems, *, alpha, tm, n_exp):
    i = pl.program_id(0)
    n_steps = pl.num_programs(0)
    slot = i % 2
    nc = ycomp.shape[1]

    def group_copy(step, e, s, off, rows):
        src = ys_ref.at[pl.ds(pl.multiple_of(gdst_ref[step * n_exp + e] + off, SUBLANES), rows)]
        dst = ycomp.at[s, pl.ds(pl.multiple_of(loff_ref[step * n_exp + e] + off, SUBLANES), rows)]
        return pltpu.make_async_copy(src, dst, sems.at[s])

    def start_groups(step, s):
        for e in range(n_exp):
            _for_row_chunks(n8_ref[step * n_exp + e],
                            lambda off, rows, e=e: group_copy(step, e, s, off, rows).start(priority=e % 2))

    def wait_groups(step, s):
        total = n8_ref[step * n_exp]
        for e in range(1, n_exp):
            total = total + n8_ref[step * n_exp + e]
        _wait_rows(total, lambda rows: pltpu.make_async_copy(
            ys_ref.at[pl.ds(0, rows)], ycomp.at[s, pl.ds(0, rows)], sems.at[s]).wait())

    @pl.when(i == 0)
    def _():
        ycomp[...] = jnp.zeros(ycomp.shape, F32)
        start_groups(0, 0)

    wait_groups(i, slot)

    @pl.when(i + 1 < n_steps)
    def _():
        start_groups(i + 1, 1 - slot)

    x = x_ref[...]
    acc = alpha * x + _ple(x.astype(BF16), p_ref[...], wpg_ref, bpg_ref, wpp_ref)

    info_t = info_ref[...]
    info = jnp.concatenate([info_t, jnp.zeros((LANES - SUBLANES, tm), F32)], axis=0).T
    slot_col = lax.broadcasted_iota(jnp.int32, (tm, nc), 1)
    pick = jnp.zeros((tm, nc), F32)
    for k in range(TOP_K):
        s_k = _local_slot(info[:, k:k + 1], info[:, TOP_K + k:TOP_K + k + 1],
                          loff_ref, i * n_exp, n_exp)
        pick = jnp.where(slot_col == s_k, info[:, 4 + k:5 + k], pick)
    acc = acc + _dot(pick.astype(BF16), ycomp[slot].astype(BF16))
    o_ref[...] = _layer_norm(acc, g2_ref[...], b2_ref[...])


def _combine_call(n8, loff, gdst, x, p, layer, info, ys, wpg, bpg, wpp, g2, b2, *, alpha, tm, n_exp):
    t, d = x.shape
    nc = TOP_K * tm + SUBLANES * n_exp
    consts = (wpg, bpg, wpp, g2, b2)

    def const_spec(shape):
        nd = len(shape)
        return pl.BlockSpec(shape, lambda i, *_: (0,) * nd, pipeline_mode=pl.Buffered(1))

    return pl.pallas_call(
        functools.partial(_combine_kernel, alpha=alpha, tm=tm, n_exp=n_exp),
        grid_spec=pltpu.PrefetchScalarGridSpec(
            num_scalar_prefetch=3,
            grid=(t // tm,),
            in_specs=[pl.BlockSpec((tm, d), lambda i, *_: (i, 0)),
                      pl.BlockSpec((None, tm, p.shape[2]), lambda i, *_: (layer, i, 0)),
                      pl.BlockSpec((SUBLANES, tm), lambda i, *_: (0, i)),
                      pl.BlockSpec(memory_space=pl.ANY)]
            + [const_spec(c.shape) for c in consts],
            out_specs=pl.BlockSpec((tm, d), lambda i, *_: (i, 0)),
            scratch_shapes=[pltpu.VMEM((2, nc, d), F32), pltpu.SemaphoreType.DMA((2,))],
        ),
        out_shape=jax.ShapeDtypeStruct((t, d), F32),
        compiler_params=pltpu.CompilerParams(
            dimension_semantics=("arbitrary",), vmem_limit_bytes=VMEM_LIMIT),
        name="combine",
    )(n8, loff, gdst, x, p, info, ys, *consts)


def _moe_layer(x, p, layer, info, cnt, n_exp, we_gate, we_up, we_down, wpg, bpg, wpp, g2, b2,
               *, alpha, tm, tme, fc):
    t, d = x.shape
    n_tok_tiles = t // tm
    assert tm <= ROW_CHUNKS[0] and tme <= ROW_CHUNKS[0]

    i32 = jnp.int32
    n = cnt[:, 0].reshape(n_tok_tiles, SUBLANES)[:, :n_exp].astype(i32)
    n8 = (n + SUBLANES - 1) // SUBLANES * SUBLANES
    loff = jnp.cumsum(n8, axis=1) - n8
    tot = jnp.sum(n8, axis=0)
    reg = (tot + tme - 1) // tme * tme
    reg_end = jnp.cumsum(reg)
    base = reg_end - reg
    gdst = base[None, :] + jnp.cumsum(n8, axis=0) - n8
    n_tiles = (t * TOP_K + n_tok_tiles * n_exp * (SUBLANES - 1)) // tme + n_exp + 1
    n_rows = n_tiles * tme
    tile_start = jnp.arange(n_tiles, dtype=i32) * tme
    tile_e = jnp.minimum(jnp.sum(tile_start[:, None] >= reg_end[None, :], axis=1), n_exp - 1).astype(i32)
    mine = tile_e[:, None] == jnp.arange(n_exp, dtype=i32)[None, :]
    left = jnp.sum(jnp.where(mine, (base + tot)[None, :], 0), axis=1) - tile_start
    tile_n = jnp.clip(left, 0, tme).astype(i32)
    gap = jnp.stack([base + tot, reg - tot], axis=1).reshape(-1)
    gap = jnp.concatenate([gap, jnp.stack([reg_end[-1], (n_rows - reg_end[-1]) // tme])]).astype(i32)
    n8f, lofff, gdstf = (a.reshape(-1).astype(i32) for a in (n8, loff, gdst))

    xs = _dispatch_call(n8f, lofff, gdstf, gap, x, info, tm=tm, tme=tme, n_exp=n_exp, n_rows=n_rows)
    ys = _expert_call(tile_e, tile_n, xs, we_gate, we_up, we_down, tme=tme, fc=fc)
    return _combine_call(n8f, lofff, gdstf, x, p, layer, info, ys, wpg, bpg, wpp, g2, b2,
                         alpha=alpha, tm=tm, n_exp=n_exp)


def _row(v):
    return v.reshape(1, -1)


def kernel(x, p, w_in, b_in, conf_conv_w, conf_conv_b, conf_ln_g, conf_ln_b, w_conf_out,
           sc_conv_w, w_sc_out, w_o, ln1_g, ln1_b, w_ff_gate, w_ff_up, w_ff_down, w_router,
           b_router, we_gate, we_up, we_down, w_ple_gate, b_ple_gate, w_ple_proj, ln2_g, ln2_b,
           ):
    ts, tm, tme, fc_moe = TILES
    depth = w_in.shape[0]
    alpha = (2 * depth) ** 0.25
    bsz, seq, d = x.shape
    pt = p.reshape(depth, bsz * seq, -1)

    expert_w = (("gate", we_gate), ("up", we_up), ("down", we_down))
    queue = [((i // 2, name), w[i // 2], i) for i in range(depth) if i % 2 == 1 for name, w in expert_w]
    cast = {}
    idle = jnp.zeros((bsz * seq // min(ts, tm) * 2 * SUBLANES, LANES), F32)

    def side_job(layer, is_mixer):
        for k, (key, w, needed_in) in enumerate(queue):
            if layer < needed_in or (layer == needed_in and is_mixer):
                del queue[k]
                return key, w
        return None, idle

    def run(call, *args, layer, is_mixer, **kw):
        key, w = side_job(layer, is_mixer)
        out, w_bf16, *rest = call(*args, w.reshape(-1, w.shape[-1]), **kw)
        if key is not None:
            cast[key] = w_bf16.reshape(w.shape)
        return (out, *rest) if rest else out

    assert ts == tm
    for i in range(depth):
        moe = i % 2 == 1
        x, info, cnt = run(
            _mixer_call,
            x, w_in[i].astype(BF16), _row(b_in[i]), conf_conv_w[i], _row(conf_conv_b[i]),
            _row(conf_ln_g[i]), _row(conf_ln_b[i]), w_conf_out[i].astype(BF16), sc_conv_w[i],
            w_sc_out[i].astype(BF16), w_o[i].astype(BF16), _row(ln1_g[i]), _row(ln1_b[i]),
            w_router[i // 2] if moe else None, b_router[i // 2] if moe else None,
            layer=i, is_mixer=True, alpha=alpha, ts=ts)
        xt = x.reshape(bsz * seq, d)
        ple_w = (w_ple_gate[i].astype(BF16), _row(b_ple_gate[i]), w_ple_proj[i].astype(BF16),
                 _row(ln2_g[i]), _row(ln2_b[i]))
        j = i // 2
        if i % 2 == 0:
            xt = run(_dense_ffn_call,
                     xt, pt, i, w_ff_gate[j].astype(BF16), w_ff_up[j].astype(BF16),
                     w_ff_down[j].astype(BF16), *ple_w, layer=i, is_mixer=False, alpha=alpha, tm=tm)
        else:
            queue[:] = [q for q in queue if q[2] != i]
            ew = [cast.pop((j, name)) if (j, name) in cast else w[j].astype(BF16)
                  for name, w in expert_w]
            xt = _moe_layer(xt, pt, i, info, cnt, w_router.shape[2], *ew,
                            *ple_w, alpha=alpha, tm=tm, tme=tme, fc=fc_moe)
        x = xt.reshape(bsz, seq, d)
    return x
```
